```python
import math
import jax
import jax.numpy as jnp
from jax import lax
import numpy as np

D_MODEL = 1024
BATCH = 16
SEQ = 4096
DEPTH = 4

GRID_W = 64
CTX_LEN = 256
N_MIXERS = 4
D_FF = 4 * D_MODEL
N_MOD = 6
EPS = 1e-6
CONV_W = 3
HG_HEAD_DIM = 128
HG_HEADS = D_MODEL // HG_HEAD_DIM
HG_CHUNK = 32
S5_GROUP = 16
S5_GROUPS = D_MODEL // S5_GROUP
S5_STATE = 64
S5_DT_MIN = 1e-3
S5_DT_MAX = 1e-1
NA_HEADS = 16
NA_HEAD_DIM = D_MODEL // NA_HEADS
NA_ROWS = 8
NA_COLS = 16
NA_STRIP = 2 * NA_COLS

kernel_name = 'hybrid_interleaved_dit_block'


def _layers_of(mixer):
    return len(range(mixer, DEPTH, N_MIXERS))


def rmsnorm(x, g):
    xf = x.astype(jnp.float32)
    y = xf * lax.rsqrt(jnp.mean(xf * xf, axis=-1, keepdims=True) + EPS)
    return (y * g.astype(jnp.float32)).astype(x.dtype)


def modulate(h, shift, scale):
    return h * (1 + scale) + shift


def squared_relu_mlp(h, w_in, w_out):
    return jnp.square(jax.nn.relu(h @ w_in)) @ w_out


def depthwise_conv3(u, w):
    return lax.conv_general_dilated(u, w.astype(u.dtype)[:, None, :], window_strides=(1,),
                                    padding=((1, 1),), dimension_numbers=('NWC', 'WIO', 'NWC'),
                                    feature_group_count=u.shape[-1])


def short_gated_conv(h, w_in, conv_w, w_out):
    b_gate, c_gate, v = jnp.split(h @ w_in, 3, axis=-1)
    return (b_gate * depthwise_conv3(c_gate * v, conv_w)) @ w_out


def gla_chunked(q, k, v, log_f, s0):
    bsz, seq, nh, dk = q.shape
    dv = v.shape[-1]
    nc = seq // HG_CHUNK

    def chunks(t):
        return t.reshape(bsz, nc, HG_CHUNK, nh, t.shape[-1]).transpose(1, 0, 3, 2, 4)

    qc, kc, vc = chunks(q.astype(jnp.float32)), chunks(k.astype(jnp.float32)), chunks(v)
    b = jnp.cumsum(chunks(log_f.astype(jnp.float32)), axis=3)
    b_mid = b[:, :, :, HG_CHUNK // 2 - 1:HG_CHUNK // 2]
    b_last = b[:, :, :, -1:]
    scores = jnp.einsum('nbhtk,nbhsk->nbhts', qc * jnp.exp(b - b_mid), kc * jnp.exp(b_mid - b))
    prefix = jnp.tril(jnp.ones((HG_CHUNK, HG_CHUNK), dtype=bool))
    o_intra = jnp.einsum('nbhts,nbhsv->nbhtv', jnp.where(prefix, scores, 0.0), vc)
    q_out = qc * jnp.exp(b)
    k_state = kc * jnp.exp(b_last - b)
    decay_chunk = jnp.exp(b_last[:, :, :, 0, :])

    def step(s, xs):
        q_o, k_s, v_s, a = xs
        o = jnp.einsum('bhtk,bhkv->bhtv', q_o, s)
        s = a[..., None] * s + jnp.einsum('bhtk,bhtv->bhkv', k_s, v_s)
        return s, o

    s_fin, o_inter = lax.scan(step, s0, (q_out, k_state, vc, decay_chunk))
    o = (o_intra + o_inter).transpose(1, 0, 3, 2, 4).reshape(bsz, seq, nh, dv)
    return o, s_fin


def hgrn2_mixer(h_ctx, h_lat, w_in, lower_bound, g_norm, w_out, ctx_out):
    bsz = h_lat.shape[0]
    s0 = jnp.zeros((bsz, HG_HEADS, HG_HEAD_DIM, HG_HEAD_DIM), jnp.float32)

    def heads(t):
        return t.reshape(*t.shape[:2], HG_HEADS, HG_HEAD_DIM)

    def forget(t):
        f = lower_bound + (1 - lower_bound) * jax.nn.sigmoid(t.astype(jnp.float32))
        return heads(1 - f), heads(jnp.log(f))

    def project(h):
        q, inp, gate, f_fwd, f_bwd = jnp.split(h @ w_in, 5, axis=-1)
        return heads(q), heads(inp), gate, forget(f_fwd), forget(f_bwd)

    def flip(t):
        return t[:, ::-1]

    qc, vc, gc, (kcf, lcf), (kcb, lcb) = project(h_ctx)
    ql, vl, gl, (klf, llf), (klb, llb) = project(h_lat)
    oc_f, st_f = gla_chunked(qc, kcf, vc, lcf, s0)
    ol_f, _ = gla_chunked(ql, klf, vl, llf, st_f)
    oc_b, st_b = gla_chunked(flip(qc), flip(kcb), flip(vc), flip(lcb), s0)
    ol_b, _ = gla_chunked(flip(ql), flip(klb), flip(vl), flip(llb), st_b)

    def readout(o, gate):
        o = rmsnorm(o, g_norm.reshape(HG_HEADS, HG_HEAD_DIM))
        o = o.reshape(*o.shape[:2], D_MODEL).astype(gate.dtype)
        return (o * jax.nn.silu(gate)) @ w_out

    y_lat = readout(ol_f + flip(ol_b), gl)
    y_ctx = readout(oc_f + flip(oc_b), gc) if ctx_out else None
    return y_ctx, y_lat


def s5_discretise(lam_re, lam_im, log_dt, b_re, b_im):
    lam_re = jnp.minimum(lam_re.astype(jnp.float32), -1e-4)
    lam_im = lam_im.astype(jnp.float32)
    dt = jnp.exp(log_dt.astype(jnp.float32))[:, None]
    mag = jnp.exp(lam_re * dt)
    a_re, a_im = mag * jnp.cos(lam_im * dt), mag * jnp.sin(lam_im * dt)
    den = lam_re * lam_re + lam_im * lam_im
    f_re = ((a_re - 1) * lam_re + a_im * lam_im) / den
    f_im = (a_im * lam_re - (a_re - 1) * lam_im) / den
    b_re, b_im = b_re.astype(jnp.float32), b_im.astype(jnp.float32)
    bb_re = f_re[..., None] * b_re - f_im[..., None] * b_im
    bb_im = f_re[..., None] * b_im + f_im[..., None] * b_re
    return a_re, a_im, bb_re, bb_im


def s5_scan(bu_re, bu_im, a_re, a_im, init, reverse):
    if init is not None:
        i_re, i_im = init
        first = -1 if reverse else 0
        bu_re = bu_re.at[:, first].add(a_re * i_re - a_im * i_im)
        bu_im = bu_im.at[:, first].add(a_re * i_im + a_im * i_re)
    seq = bu_re.shape[1]
    a_seq_re = jnp.broadcast_to(a_re, (1, seq) + a_re.shape)
    a_seq_im = jnp.broadcast_to(a_im, (1, seq) + a_im.shape)

    def combine(e1, e2):
        a1r, a1i, b1r, b1i = e1
        a2r, a2i, b2r, b2i = e2
        return (a2r * a1r - a2i * a1i, a2r * a1i + a2i * a1r,
                a2r * b1r - a2i * b1i + b2r, a2r * b1i + a2i * b1r + b2i)

    _, _, x_re, x_im = lax.associative_scan(combine, (a_seq_re, a_seq_im, bu_re, bu_im),
                                            axis=1, reverse=reverse)
    return x_re, x_im


def s5_mixer(h_ctx, h_lat, lam_re, lam_im, log_dt, b_re, b_im, c_re, c_im, d_skip, w_glu, ctx_out):
    def grouped(h):
        return h.reshape(*h.shape[:2], S5_GROUPS, S5_GROUP).astype(jnp.float32)

    u_c, u_l = grouped(h_ctx), grouped(h_lat)
    d = d_skip.reshape(S5_GROUPS, S5_GROUP).astype(jnp.float32)
    y_c, y_l = d * u_c, d * u_l
    for direction in range(2):
        reverse = direction == 1
        a_re, a_im, bb_re, bb_im = s5_discretise(lam_re[direction], lam_im[direction],
                                                 log_dt[direction], b_re, b_im)
        cr, ci = c_re[direction].astype(jnp.float32), c_im[direction].astype(jnp.float32)

        def drive(u):
            return (jnp.einsum('bsgc,gnc->bsgn', u, bb_re), jnp.einsum('bsgc,gnc->bsgn', u, bb_im))

        def readout(xr, xi):
            return jnp.einsum('bsgn,gcn->bsgc', xr, cr) - jnp.einsum('bsgn,gcn->bsgc', xi, ci)

        xc_re, xc_im = s5_scan(*drive(u_c), a_re, a_im, None, reverse)
        end = 0 if reverse else -1
        xl_re, xl_im = s5_scan(*drive(u_l), a_re, a_im, (xc_re[:, end], xc_im[:, end]), reverse)
        y_l = y_l + readout(xl_re, xl_im)
        if ctx_out:
            y_c = y_c + readout(xc_re, xc_im)

    def glu(y):
        z = jax.nn.gelu(y.reshape(*y.shape[:2], D_MODEL)).astype(w_glu.dtype)
        val, gate = jnp.split(z @ w_glu, 2, axis=-1)
        return val * jax.nn.sigmoid(gate)

    return (glu(y_c) if ctx_out else None), glu(y_l)


def neighbourhood_attention(h_ctx, h_lat, w_qkv, rpb, w_out, ctx_out):
    bsz, seq, _ = h_lat.shape
    rows = seq // GRID_W
    kr = min(NA_ROWS, rows)
    n_cb = GRID_W // NA_COLS
    n_loc = kr * NA_STRIP
    scale = NA_HEAD_DIM ** -0.5

    def heads(t):
        return t.reshape(*t.shape[:2], NA_HEADS, NA_HEAD_DIM)

    q_l, k_l, v_l = (heads(t) for t in jnp.split(h_lat @ w_qkv, 3, axis=-1))
    k_c, v_c = (heads(t) for t in jnp.split(h_ctx @ w_qkv[:, D_MODEL:], 2, axis=-1))

    r = jnp.arange(rows)
    key_rows = jnp.clip(r - kr // 2, 0, rows - kr)[:, None] + jnp.arange(kr)
    qcol = (jnp.arange(n_cb) * NA_COLS)[:, None] + jnp.arange(NA_COLS)
    key_cols = (jnp.clip(jnp.arange(n_cb) * NA_COLS - NA_COLS // 2, 0, GRID_W - NA_STRIP)[:, None]
                + jnp.arange(NA_STRIP))
    q_start = jnp.clip(qcol - NA_COLS // 2, 0, GRID_W - NA_COLS)[..., None]
    kcol = key_cols[:, None, :]
    in_win = jnp.tile((kcol >= q_start) & (kcol < q_start + NA_COLS), (1, 1, kr))
    dr = key_rows - r[:, None] + (NA_ROWS - 1)
    dc = jnp.clip(kcol - qcol[..., None], 1 - NA_COLS, NA_COLS - 1) + (NA_COLS - 1)
    bias = rpb[:, dr[:, None, None, :, None], dc[None, :, :, None, :]]
    bias = bias.reshape(NA_HEADS, rows, n_cb, NA_COLS, n_loc).astype(jnp.float32)

    def gather(t):
        g = t.reshape(bsz, rows, GRID_W, NA_HEADS, NA_HEAD_DIM)
        g = g[:, key_rows[:, None, :, None], key_cols[None, :, None, :]]
        return g.reshape(bsz, rows, n_cb, n_loc, NA_HEADS, NA_HEAD_DIM)

    q_b = q_l.reshape(bsz, rows, n_cb, NA_COLS, NA_HEADS, NA_HEAD_DIM)
    k_win, v_win = gather(k_l), gather(v_l)
    s_loc = jnp.einsum('brjqhd,brjkhd->bhrjqk', q_b, k_win).astype(jnp.float32) * scale + bias
    s_loc = jnp.where(in_win, s_loc, -jnp.inf)
    s_ctx = jnp.einsum('brjqhd,bkhd->bhrjqk', q_b, k_c).astype(jnp.float32) * scale
    p = jax.nn.softmax(jnp.concatenate([s_loc, s_ctx], axis=-1), axis=-1).astype(v_l.dtype)
    o = (jnp.einsum('bhrjqk,brjkhd->brjqhd', p[..., :n_loc], v_win)
         + jnp.einsum('bhrjqk,bkhd->brjqhd', p[..., n_loc:], v_c))
    y_lat = o.reshape(bsz, seq, D_MODEL) @ w_out
    y_ctx = None
    if ctx_out:
        q_c = heads(h_ctx @ w_qkv[:, :D_MODEL])
        s_cc = jnp.einsum('bqhd,bkhd->bhqk', q_c, k_c).astype(jnp.float32) * scale
        p_cc = jax.nn.softmax(s_cc, axis=-1).astype(v_c.dtype)
        o_c = jnp.einsum('bhqk,bkhd->bqhd', p_cc, v_c)
        y_ctx = o_c.reshape(bsz, h_ctx.shape[1], D_MODEL) @ w_out
    return y_ctx, y_lat


def setup_inputs(seed: int = 0) -> dict:
    key = jax.random.key(seed)
    keys = iter(jax.random.split(key, 32))
    f32 = jnp.float32

    def normal(shape, std):
        return jax.random.normal(next(keys), shape, f32) * std

    n_a, n_b, n_c, n_d = (_layers_of(m) for m in range(N_MIXERS))
    d = D_MODEL
    lam_shape = (n_c, 2, S5_GROUPS, S5_STATE)
    return {
        'x': normal((BATCH, SEQ, d), 1.0),
        'c': normal((BATCH, d), 1.0),
        'ctx': normal((BATCH, CTX_LEN, d), 1.0),
        'c_ctx': normal((d,), 1.0),
        'ada_w': normal((DEPTH, d, N_MOD * d), 0.5 * d ** -0.5),
        'ada_b': normal((DEPTH, N_MOD * d), 0.02),
        'norm_gains': 1.0 + normal((DEPTH, 4, d), 0.02),
        'mlp_w_in': normal((DEPTH, d, D_FF), d ** -0.5),
        'mlp_w_out': normal((DEPTH, D_FF, d), D_FF ** -0.5),
        'sc_w_in': normal((n_a, d, 3 * d), d ** -0.5),
        'sc_conv': normal((n_a, CONV_W, d), CONV_W ** -0.5),
        'sc_w_out': normal((n_a, d, d), d ** -0.5),
        'hg_w_in': normal((n_b, d, 5 * d), d ** -0.5),
        'hg_lower_bound': normal((DEPTH, d), 0.1),
        'hg_norm': 1.0 + normal((n_b, d), 0.02),
        'hg_w_out': normal((n_b, d, d), d ** -0.5),
        's5_lam_re': -0.5 + normal(lam_shape, 0.01),
        's5_lam_im': math.pi * jnp.arange(S5_STATE, dtype=f32) + normal(lam_shape, 0.01),
        's5_log_dt': jax.random.uniform(next(keys), (n_c, 2, S5_GROUPS), f32,
                                        math.log(S5_DT_MIN), math.log(S5_DT_MAX)),
        's5_b_re': normal((n_c, S5_GROUPS, S5_STATE, S5_GROUP), (2 * S5_GROUP) ** -0.5),
        's5_b_im': normal((n_c, S5_GROUPS, S5_STATE, S5_GROUP), (2 * S5_GROUP) ** -0.5),
        's5_c_re': normal((n_c, 2, S5_GROUPS, S5_GROUP, S5_STATE), (2 * S5_STATE) ** -0.5),
        's5_c_im': normal((n_c, 2, S5_GROUPS, S5_GROUP, S5_STATE), (2 * S5_STATE) ** -0.5),
        's5_d': normal((n_c, d), 1.0),
        's5_w_glu': normal((n_c, d, 2 * d), d ** -0.5),
        'na_w_qkv': normal((n_d, d, 3 * d), d ** -0.5),
        'na_rpb': normal((n_d, NA_HEADS, 2 * NA_ROWS - 1, 2 * NA_COLS - 1), 0.1),
        'na_w_out': normal((n_d, d, d), d ** -0.5),
    }


def reference(x, c, ctx, c_ctx, ada_w, ada_b, norm_gains, mlp_w_in, mlp_w_out,
              sc_w_in, sc_conv, sc_w_out, hg_w_in, hg_lower_bound, hg_norm, hg_w_out,
              s5_lam_re, s5_lam_im, s5_log_dt, s5_b_re, s5_b_im, s5_c_re, s5_c_im, s5_d, s5_w_glu,
              na_w_qkv, na_rpb, na_w_out):
    lb_all = jnp.cumsum(jax.nn.softmax(hg_lower_bound.astype(jnp.float32), axis=0), axis=0)
    lb_all = lb_all - lb_all[0]
    h_lat, h_ctx = x, ctx
    for i in range(DEPTH):
        kind, j = i % N_MIXERS, i // N_MIXERS
        ctx_out = i < DEPTH - 1
        mod_lat = [m[:, None, :] for m in jnp.split(jax.nn.silu(c) @ ada_w[i] + ada_b[i], N_MOD, axis=-1)]
        mod_ctx = jnp.split(jax.nn.silu(c_ctx) @ ada_w[i] + ada_b[i], N_MOD, axis=-1)
        g_pre, g_post, g_pre_ff, g_post_ff = norm_gains[i]
        a_lat = modulate(rmsnorm(h_lat, g_pre), mod_lat[0], mod_lat[1])
        a_ctx = modulate(rmsnorm(h_ctx, g_pre), mod_ctx[0], mod_ctx[1]) if (ctx_out or kind != 0) else None
        if kind == 0:
            y_lat = short_gated_conv(a_lat, sc_w_in[j], sc_conv[j], sc_w_out[j])
            y_ctx = short_gated_conv(a_ctx, sc_w_in[j], sc_conv[j], sc_w_out[j]) if ctx_out else None
        elif kind == 1:
            y_ctx, y_lat = hgrn2_mixer(a_ctx, a_lat, hg_w_in[j], lb_all[i], hg_norm[j], hg_w_out[j], ctx_out)
        elif kind == 2:
            y_ctx, y_lat = s5_mixer(a_ctx, a_lat, s5_lam_re[j], s5_lam_im[j], s5_log_dt[j], s5_b_re[j],
                                    s5_b_im[j], s5_c_re[j], s5_c_im[j], s5_d[j], s5_w_glu[j], ctx_out)
        else:
            y_ctx, y_lat = neighbourhood_attention(a_ctx, a_lat, na_w_qkv[j], na_rpb[j], na_w_out[j], ctx_out)
        h_lat = h_lat + mod_lat[2] * rmsnorm(y_lat.astype(h_lat.dtype), g_post)
        if ctx_out:
            h_ctx = h_ctx + mod_ctx[2] * rmsnorm(y_ctx.astype(h_ctx.dtype), g_post)
        f_lat = squared_relu_mlp(modulate(rmsnorm(h_lat, g_pre_ff), mod_lat[3], mod_lat[4]), mlp_w_in[i], mlp_w_out[i])
        h_lat = h_lat + mod_lat[5] * rmsnorm(f_lat.astype(h_lat.dtype), g_post_ff)
        if ctx_out:
            f_ctx = squared_relu_mlp(modulate(rmsnorm(h_ctx, g_pre_ff), mod_ctx[3], mod_ctx[4]), mlp_w_in[i], mlp_w_out[i])
            h_ctx = h_ctx + mod_ctx[5] * rmsnorm(f_ctx.astype(h_ctx.dtype), g_post_ff)
    return h_lat
```

```python
import functools
import math

import jax
import jax.numpy as jnp
from jax import lax
from jax.experimental import pallas as pl
from jax.experimental.pallas import tpu as pltpu

EPS = 1e-6
N_MOD = 6
MOD_ROWS = 24
HG_HEAD_DIM = 128
HG_CHUNK = 32
S5_GROUP = 16
S5_SLAB = 256
S5_TSTEPS = 16
GRID_W = 64
NA_HEADS = 16
NA_ROWS = 8
NA_COLS = 16
NEG_BIG = -1e30
VMEM_LIMIT = 56 * 1024 * 1024

bf16 = jnp.bfloat16
f32 = jnp.float32


def _cparams(sem):
    return pltpu.CompilerParams(dimension_semantics=sem, vmem_limit_bytes=VMEM_LIMIT)


def _rms(x, g):
    return x * lax.rsqrt(jnp.mean(x * x, axis=-1, keepdims=True) + EPS) * g


def _dot(a, b):
    return jnp.dot(a, b, preferred_element_type=f32)


def _dot_nt(a, b):
    return lax.dot_general(a, b, (((1,), (1,)), ((), ())), preferred_element_type=f32)


def _dot_tn(a, b):
    return lax.dot_general(a, b, (((0,), (0,)), ((), ())), preferred_element_type=f32)


def _split3(x):
    hi = x.astype(bf16)
    r = x - hi.astype(f32)
    mid = r.astype(bf16)
    lo = (r - mid.astype(f32)).astype(bf16)
    return hi, mid, lo


def _mods_kernel(cc_ref, w_ref, b_ref, o_ref):
    x = cc_ref[...]
    a = (x * jax.nn.sigmoid(x)).astype(bf16)
    o_ref[...] = _dot(a, w_ref[...].astype(bf16)) + b_ref[...]


def _mods(cc, ada_w, ada_b):
    depth, d, _ = ada_w.shape
    out = pl.pallas_call(
        _mods_kernel,
        out_shape=jax.ShapeDtypeStruct((depth, N_MOD, MOD_ROWS, d), f32),
        grid=(depth, N_MOD),
        in_specs=[pl.BlockSpec((MOD_ROWS, d), lambda i, j: (0, 0)),
                  pl.BlockSpec((None, d, d), lambda i, j: (i, 0, j)),
                  pl.BlockSpec((None, 1, d), lambda i, j: (i * N_MOD + j, 0, 0))],
        out_specs=pl.BlockSpec((None, None, MOD_ROWS, d), lambda i, j: (i, j, 0, 0)),
        compiler_params=_cparams(("arbitrary", "arbitrary")),
        name="ada_mods",
    )(cc, ada_w, ada_b.reshape(depth * N_MOD, 1, d))
    return out


def _prep_kernel(lbp_ref, lre_ref, lim_ref, ldt_ref, cre_ref, cim_ref,
                 lb_ref, are_ref, aim_ref, cfre_ref, cfim_ref):
    x = lbp_ref[...]
    rows = [x[i:i + 1, :] for i in range(x.shape[0])]
    m = functools.reduce(jnp.maximum, rows)
    es = [jnp.exp(r - m) for r in rows]
    tot = functools.reduce(lambda a, b: a + b, es)
    acc = None
    first = None
    for i, e in enumerate(es):
        sm = e / tot
        acc = sm if acc is None else acc + sm
        if first is None:
            first = acc
        lb_ref[i:i + 1, :] = acc - first
    lam_re = jnp.minimum(lre_ref[...], -1e-4)
    lam_im = lim_ref[...]
    dt = jnp.exp(ldt_ref[...])
    mag = jnp.exp(lam_re * dt)
    a_re = mag * jnp.cos(lam_im * dt)
    a_im = mag * jnp.sin(lam_im * dt)
    den = lam_re * lam_re + lam_im * lam_im
    f_re = ((a_re - 1) * lam_re + a_im * lam_im) / den
    f_im = (a_im * lam_re - (a_re - 1) * lam_im) / den
    are_ref[...] = a_re
    aim_ref[...] = a_im
    c_re, c_im = cre_ref[...], cim_ref[...]
    cfre_ref[...] = c_re * f_re - c_im * f_im
    cfim_ref[...] = c_re * f_im + c_im * f_re


def _prep(hg_lower_bound, lam_re, lam_im, log_dt, c_re, c_im):
    shape = c_re.shape
    flat = (shape[0] * shape[1] * shape[2], shape[3])

    def expand(t):
        return jnp.broadcast_to(t[:, :, None, :], shape).reshape(flat)

    ldt = jnp.broadcast_to(log_dt[:, :, None, None], shape).reshape(flat)
    outs = pl.pallas_call(
        _prep_kernel,
        out_shape=[jax.ShapeDtypeStruct(hg_lower_bound.shape, f32)] + [jax.ShapeDtypeStruct(flat, f32)] * 4,
        name="param_prep",
    )(hg_lower_bound.astype(f32), expand(lam_re.astype(f32)), expand(lam_im.astype(f32)), ldt.astype(f32),
      c_re.astype(f32).reshape(flat), c_im.astype(f32).reshape(flat))
    lb, a_re, a_im, cf_re, cf_im = outs
    a_re = a_re.reshape(shape)[:, :, 0, :]
    a_im = a_im.reshape(shape)[:, :, 0, :]
    return lb, a_re, a_im, cf_re.reshape(shape), cf_im.reshape(shape)


def _mod_row(b, t):
    return jnp.where(t == 0, MOD_ROWS - 8, b)


def _mods_spec(d):
    return pl.BlockSpec((None, N_MOD, d), lambda b, t: (_mod_row(b, t), 0, 0))


def _full_spec(arr):
    nd = arr.ndim
    return pl.BlockSpec(arr.shape, lambda b, t: (0,) * nd)


def _tile_spec(d, nt, off=0):
    return pl.BlockSpec((TILE_ROWS[0], d), lambda b, t: (b * nt + t + off, 0))


TILE_ROWS = [256]


def _prenorm(h, mods_ref, gains_ref, which):
    g = gains_ref[2 * which:2 * which + 1, :]
    shift = mods_ref[3 * which:3 * which + 1, :]
    scale = mods_ref[3 * which + 1:3 * which + 2, :]
    return _rms(h, g) * (1 + scale) + shift


def _residual(h, y, mods_ref, gains_ref, which):
    g = gains_ref[2 * which + 1:2 * which + 2, :]
    gate = mods_ref[3 * which + 2:3 * which + 3, :]
    return h + gate * _rms(y, g)


def _first_layer_h(t, hc_ref, hl_ref):
    return jnp.where(t == 0, hc_ref[...], hl_ref[...])


def _proj_kernel(*refs, split, first, conv):
    if first:
        hc_ref, hl_ref, mods_ref, gains_ref, w_ref = refs[:5]
        outs = refs[5:]
        h = _first_layer_h(pl.program_id(1), hc_ref, hl_ref)
    else:
        h_ref, mods_ref, gains_ref, w_ref = refs[:4]
        outs = refs[4:]
        h = h_ref[...]
    d = h.shape[-1]
    a = _prenorm(h, mods_ref, gains_ref, 0).astype(bf16)
    res = _dot(a, w_ref[...])
    if conv:
        bg_ref, u_ref = outs
        bg_ref[...] = res[:, :d].astype(bg_ref.dtype)
        u_ref[...] = (res[:, d:2 * d] * res[:, 2 * d:3 * d]).astype(u_ref.dtype)
    else:
        for k, (o_ref, sc) in enumerate(zip(outs, split)):
            piece = res[:, k * d:(k + 1) * d]
            if sc != 1.0:
                piece = piece * sc
            o_ref[...] = piece.astype(o_ref.dtype)


def _proj(h_in, mods, gains, w, out_dtypes, *, bsz, nt, first=False, conv=False, scales=None):
    d = w.shape[0]
    tile = TILE_ROWS[0]
    rows = bsz * nt * tile
    if first:
        hc, hl = h_in
        in_arrays = [hc, hl]
        in_specs = [pl.BlockSpec((tile, d), lambda b, t: (b, 0)),
                    pl.BlockSpec((tile, d), lambda b, t: (b * (nt - 1) + jnp.maximum(t - 1, 0), 0))]
    else:
        in_arrays = [h_in]
        in_specs = [_tile_spec(d, nt)]
    in_arrays += [mods, gains, w]
    in_specs += [_mods_spec(d), _full_spec(gains), _full_spec(w)]
    scales = scales or (1.0,) * len(out_dtypes)
    return pl.pallas_call(
        functools.partial(_proj_kernel, split=tuple(scales), first=first, conv=conv),
        out_shape=[jax.ShapeDtypeStruct((rows, d), dt) for dt in out_dtypes],
        grid=(bsz, nt),
        in_specs=in_specs,
        out_specs=[_tile_spec(d, nt) for _ in out_dtypes],
        compiler_params=_cparams(("parallel", "arbitrary")),
        name="prenorm_proj",
    )(*in_arrays)


def _mlp_body(h, mods_ref, gains_ref, w1_ref, w2_ref, ff_chunk):
    a = _prenorm(h, mods_ref, gains_ref, 1).astype(bf16)
    d_ff = w1_ref.shape[1]
    acc = None
    for c in range(d_ff // ff_chunk):
        hid = jnp.maximum(_dot(a, w1_ref[:, c * ff_chunk:(c + 1) * ff_chunk]), 0.0)
        part = _dot((hid * hid).astype(bf16), w2_ref[c * ff_chunk:(c + 1) * ff_chunk, :])
        acc = part if acc is None else acc + part
    return _residual(h, acc, mods_ref, gains_ref, 1)


def _mlp_kernel(h_ref, mods_ref, gains_ref, w1_ref, w2_ref, o_ref, *, ff_chunk):
    o_ref[...] = _mlp_body(h_ref[...], mods_ref, gains_ref, w1_ref, w2_ref, ff_chunk)


def _mlp(h, mods, gains, w1, w2, *, bsz, nt, out_layout="bm", lat_only=False):
    d = w1.shape[0]
    tile = TILE_ROWS[0]
    t0 = 1 if lat_only else 0
    grid = (bsz, nt - t0)
    if lat_only:
        out_shape = jax.ShapeDtypeStruct((bsz * (nt - 1) * tile, d), f32)
        out_spec = pl.BlockSpec((tile, d), lambda b, t: (b * (nt - 1) + t, 0))
    elif out_layout == "tm":
        out_shape = jax.ShapeDtypeStruct((nt * tile, bsz * d), f32)
        out_spec = pl.BlockSpec((tile, d), lambda b, t: (t, b))
    else:
        out_shape = jax.ShapeDtypeStruct((bsz * nt * tile, d), f32)
        out_spec = _tile_spec(d, nt)
    return pl.pallas_call(
        functools.partial(_mlp_kernel, ff_chunk=min(1024, w1.shape[1])),
        out_shape=out_shape,
        grid=grid,
        in_specs=[pl.BlockSpec((tile, d), lambda b, t: (b * nt + t + t0, 0)),
                  pl.BlockSpec((None, N_MOD, d), lambda b, t: (_mod_row(b, t + t0), 0, 0)),
                  _full_spec(gains), _full_spec(w1), _full_spec(w2)],
        out_specs=out_spec,
        compiler_params=_cparams(("parallel", "arbitrary")),
        name="relu2_mlp",
    )(h, mods, gains, w1, w2)


def _conv_post_kernel(hc_ref, hl_ref, bg_ref, u_ref, up_ref, un_ref, cw_ref, mods_ref, gains_ref, wo_ref, o_ref, *, nt):
    t = pl.program_id(1)
    h = _first_layer_h(t, hc_ref, hl_ref)
    u = u_ref[...].astype(f32)
    rows = u.shape[0]
    hal = up_ref.shape[0]
    prev_row = jnp.where(t <= 1, 0.0, up_ref[hal - 1:hal, :].astype(f32))
    next_row = jnp.where((t == 0) | (t == nt - 1), 0.0, un_ref[0:1, :].astype(f32))
    ridx = lax.broadcasted_iota(jnp.int32, u.shape, 0)
    u_prev = jnp.where(ridx == 0, prev_row, pltpu.roll(u, 1, 0))
    u_next = jnp.where(ridx == rows - 1, next_row, pltpu.roll(u, rows - 1, 0))
    cw = cw_ref[...]
    conv = cw[0:1, :] * u_prev + cw[1:2, :] * u + cw[2:3, :] * u_next
    y = _dot((bg_ref[...].astype(f32) * conv).astype(bf16), wo_ref[...])
    o_ref[...] = _residual(h, y, mods_ref, gains_ref, 0)


def _conv_post(hc, hl, bg, u, conv_w, mods, gains, wo, *, bsz, nt):
    d = wo.shape[0]
    tile = TILE_ROWS[0]
    hal = 16
    per = tile // hal
    nblk = bsz * nt * per
    return pl.pallas_call(
        functools.partial(_conv_post_kernel, nt=nt),
        out_shape=jax.ShapeDtypeStruct((bsz * nt * tile, d), f32),
        grid=(bsz, nt),
        in_specs=[pl.BlockSpec((tile, d), lambda b, t: (b, 0)),
                  pl.BlockSpec((tile, d), lambda b, t: (b * (nt - 1) + jnp.maximum(t - 1, 0), 0)),
                  _tile_spec(d, nt), _tile_spec(d, nt),
                  pl.BlockSpec((hal, d), lambda b, t: (jnp.maximum((b * nt + t) * per - 1, 0), 0)),
                  pl.BlockSpec((hal, d), lambda b, t: (jnp.minimum((b * nt + t + 1) * per, nblk - 1), 0)),
                  _full_spec(conv_w), _mods_spec(d), _full_spec(gains), _full_spec(wo)],
        out_specs=_tile_spec(d, nt),
        compiler_params=_cparams(("parallel", "arbitrary")),
        name="conv_gate_out",
    )(hc, hl, bg, u, u, u, conv_w, mods, gains, wo)


def _gla_kernel(*refs, reverse, accumulate):
    if accumulate:
        q_ref, v_ref, f_ref, lb_ref, prev_ref, o_ref, st_ref = refs
    else:
        q_ref, v_ref, f_ref, lb_ref, o_ref, st_ref = refs
        prev_ref = None
    tile, d = q_ref.shape
    nh = d // HG_HEAD_DIM
    c = HG_CHUNK
    nchunk = tile // c

    @pl.when(pl.program_id(1) == 0)
    def _():
        st_ref[...] = jnp.zeros_like(st_ref)

    ri = lax.broadcasted_iota(jnp.int32, (c, c), 0)
    ci = lax.broadcasted_iota(jnp.int32, (c, c), 1)
    causal = (ci >= ri) if reverse else (ci <= ri)
    tri = jnp.where(causal, 1.0, 0.0).astype(bf16)
    mid = c // 2 if reverse else c // 2 - 1
    last = 0 if reverse else c - 1
    lb = lb_ref[...]

    def chunk(i, carry):
        ci_ = (nchunk - 1 - i) if reverse else i
        r0 = pl.multiple_of(ci_ * c, c)
        fr = f_ref[pl.ds(r0, c), :]
        fg = lb + (1 - lb) * jax.nn.sigmoid(fr)
        kk = 1 - fg
        lf = jnp.log(fg)
        hi, md, lo = _split3(lf)
        bcum = _dot(tri, hi) + _dot(tri, md) + _dot(tri, lo)
        b_mid = bcum[mid:mid + 1, :]
        b_last = bcum[last:last + 1, :]
        qq = q_ref[pl.ds(r0, c), :].astype(f32)
        vv = v_ref[pl.ds(r0, c), :]
        q_in = (qq * jnp.exp(bcum - b_mid)).astype(bf16)
        k_in = (kk * jnp.exp(b_mid - bcum)).astype(bf16)
        q_out = (qq * jnp.exp(bcum)).astype(bf16)
        k_st = (kk * jnp.exp(b_last - bcum)).astype(bf16)
        decay = jnp.exp(b_last)
        outs = []
        for h in range(nh):
            sl = slice(h * HG_HEAD_DIM, (h + 1) * HG_HEAD_DIM)
            sc = _dot_nt(q_in[:, sl], k_in[:, sl])
            sc = jnp.where(causal, sc, 0.0).astype(bf16)
            st = st_ref[h]
            o_h = _dot(sc, vv[:, sl]) + _dot_nt(q_out[:, sl], st.astype(bf16))
            st_ref[h] = st * decay[:, sl] + _dot_tn(vv[:, sl], k_st[:, sl])
            outs.append(o_h)
        o = jnp.concatenate(outs, axis=-1)
        if prev_ref is not None:
            o = o + prev_ref[pl.ds(r0, c), :]
        o_ref[pl.ds(r0, c), :] = o
        return carry

    lax.fori_loop(0, nchunk, chunk, 0)


def _gla(q, v, fraw, lb, prev, *, bsz, nt, reverse):
    d = q.shape[1]
    tile = TILE_ROWS[0]
    nh = d // HG_HEAD_DIM

    def tmap(b, s):
        t = jnp.where(s == 0, 0, nt - s) if reverse else s
        return (b * nt + t, 0)

    spec = pl.BlockSpec((tile, d), tmap)
    arrays = [q, v, fraw, lb]
    specs = [spec, spec, spec, _full_spec(lb)]
    aliases = {}
    if prev is not None:
        arrays.append(prev)
        specs.append(spec)
        aliases = {4: 0}
    return pl.pallas_call(
        functools.partial(_gla_kernel, reverse=reverse, accumulate=prev is not None),
        out_shape=jax.ShapeDtypeStruct(q.shape, f32),
        grid=(bsz, nt),
        in_specs=specs,
        out_specs=spec,
        scratch_shapes=[pltpu.VMEM((nh, HG_HEAD_DIM, HG_HEAD_DIM), f32)],
        input_output_aliases=aliases,
        compiler_params=_cparams(("parallel", "arbitrary")),
        name="hgrn2_gla_bwd" if reverse else "hgrn2_gla_fwd",
    )(*arrays)


def _hg_post_kernel(h_ref, o_ref_in, gate_ref, gn_ref, mods_ref, gains_ref, wo_ref, out_ref):
    o = o_ref_in[...]
    d = o.shape[-1]
    gn = gn_ref[...]
    pieces = []
    for hh in range(d // HG_HEAD_DIM):
        sl = slice(hh * HG_HEAD_DIM, (hh + 1) * HG_HEAD_DIM)
        pieces.append(_rms(o[:, sl], gn[:, sl]))
    on = jnp.concatenate(pieces, axis=-1)
    g = gate_ref[...].astype(f32)
    y = _dot((on * (g * jax.nn.sigmoid(g))).astype(bf16), wo_ref[...])
    out_ref[...] = _residual(h_ref[...], y, mods_ref, gains_ref, 0)


def _hg_post(h, o, gate, gnorm, mods, gains, wo, *, bsz, nt):
    d = wo.shape[0]
    return pl.pallas_call(
        _hg_post_kernel,
        out_shape=jax.ShapeDtypeStruct(h.shape, f32),
        grid=(bsz, nt),
        in_specs=[_tile_spec(d, nt), _tile_spec(d, nt), _tile_spec(d, nt), _full_spec(gnorm),
                  _mods_spec(d), _full_spec(gains), _full_spec(wo)],
        out_specs=_tile_spec(d, nt),
        compiler_params=_cparams(("parallel", "arbitrary")),
        name="hgrn2_readout",
    )(h, o, gate, gnorm, mods, gains, wo)


def _s5_kernel(*refs, reverse, accumulate, nct, bsz):
    if accumulate:
        h_ref, modsb_ref, gains_ref, wd_ref, wr_ref, are_ref, aim_ref, dsk_ref, prev_ref, y_ref, z_ref, st_ref = refs
    else:
        h_ref, modsb_ref, gains_ref, wd_ref, wr_ref, are_ref, aim_ref, dsk_ref, y_ref, z_ref, st_ref = refs
        prev_ref = None
    s = pl.program_id(0)
    rows, d = h_ref.shape
    ts = rows // bsz
    nslab = d // S5_SLAB
    lanes = z_ref.shape[2]
    nchunk = z_ref.shape[0] // 2
    per_slab = nchunk // nslab

    @pl.when(s == 0)
    def _():
        st_ref[...] = jnp.zeros_like(st_ref)

    is_ctx = s < nct
    h3 = h_ref[...].reshape(ts, bsz, d)
    g = gains_ref[0:1, :]
    shift = jnp.where(is_ctx, modsb_ref[0, MOD_ROWS - 8:MOD_ROWS - 7, :], modsb_ref[0, 0:bsz, :])
    scale = jnp.where(is_ctx, modsb_ref[1, MOD_ROWS - 8:MOD_ROWS - 7, :], modsb_ref[1, 0:bsz, :])
    u3 = h3 * lax.rsqrt(jnp.mean(h3 * h3, axis=-1, keepdims=True) + EPS) * g * (1 + scale) + shift
    u = u3.reshape(rows, d)
    ub = u.astype(bf16)
    for kb in range(nslab):
        v = _dot(ub[:, kb * S5_SLAB:(kb + 1) * S5_SLAB], wd_ref[kb])
        for cc in range(per_slab):
            z_ref[kb * per_slab + cc] = v[:, cc * lanes:(cc + 1) * lanes]
            z_ref[nchunk + kb * per_slab + cc] = v[:, (per_slab + cc) * lanes:(per_slab + cc + 1) * lanes]

    def chunk(ck, carry):
        ar = are_ref[ck]
        ai = aim_ref[ck]
        zr = st_ref[ck]
        zi = st_ref[nchunk + ck]
        order = range(ts - 1, -1, -1) if reverse else range(ts)
        for t in order:
            rs = slice(t * bsz, (t + 1) * bsz)
            br = z_ref[ck, rs, :]
            bi = z_ref[nchunk + ck, rs, :]
            nzr = ar * zr - ai * zi + br
            nzi = ar * zi + ai * zr + bi
            zr, zi = nzr, nzi
            z_ref[ck, rs, :] = zr
            z_ref[nchunk + ck, rs, :] = zi
        st_ref[ck] = zr
        st_ref[nchunk + ck] = zi
        return carry

    lax.fori_loop(0, nchunk, chunk, 0)

    for kb in range(nslab):
        acc = None
        for cc in range(per_slab):
            ck = kb * per_slab + cc
            p = (_dot(z_ref[ck].astype(bf16), wr_ref[kb, cc * lanes:(cc + 1) * lanes, :])
                 + _dot(z_ref[nchunk + ck].astype(bf16), wr_ref[kb, (per_slab + cc) * lanes:(per_slab + cc + 1) * lanes, :]))
            acc = p if acc is None else acc + p
        sl = slice(kb * S5_SLAB, (kb + 1) * S5_SLAB)
        if prev_ref is not None:
            acc = acc + prev_ref[:, sl]
        else:
            acc = acc + dsk_ref[:, sl] * u[:, sl]
        y_ref[:, sl] = acc


def _s5_scan(h_tm, modsb, gains, wd, wr, a_re, a_im, dskip, prev, *, bsz, nct, nlt, reverse):
    rows_total, d = h_tm.shape
    rows = S5_TSTEPS * bsz
    ntot = nct + nlt
    nchunk2, lanes = a_re.shape[0] * 2, a_re.shape[2]

    def tmap(s):
        if reverse:
            return (jnp.where(s < nct, nct - 1 - s, ntot + nct - 1 - s), 0)
        return (s, 0)

    spec = pl.BlockSpec((rows, d), tmap)

    def full(arr):
        nd = arr.ndim
        return pl.BlockSpec(arr.shape, lambda s: (0,) * nd)

    arrays = [h_tm, modsb, gains, wd, wr, a_re, a_im, dskip]
    specs = [spec, full(modsb), full(gains), full(wd), full(wr), full(a_re), full(a_im), full(dskip)]
    aliases = {}
    if prev is not None:
        arrays.append(prev)
        specs.append(spec)
        aliases = {8: 0}
    return pl.pallas_call(
        functools.partial(_s5_kernel, reverse=reverse, accumulate=prev is not None, nct=nct, bsz=bsz),
        out_shape=jax.ShapeDtypeStruct((rows_total, d), f32),
        grid=(ntot,),
        in_specs=specs,
        out_specs=spec,
        scratch_shapes=[pltpu.VMEM((nchunk2, rows, lanes), f32), pltpu.VMEM((nchunk2, bsz, lanes), f32)],
        input_output_aliases=aliases,
        compiler_params=_cparams(("arbitrary",)),
        name="s5_scan_bwd" if reverse else "s5_scan_fwd",
    )(*arrays)


def _s5_post_kernel(h_ref, y_ref, mods_ref, gains_ref, wg_ref, o_ref):
    d = h_ref.shape[-1]
    z = jax.nn.gelu(y_ref[...]).astype(bf16)
    r = _dot(z, wg_ref[...])
    out = r[:, :d] * jax.nn.sigmoid(r[:, d:])
    o_ref[...] = _residual(h_ref[...], out, mods_ref, gains_ref, 0)


def _s5_post(h_tm, y_tm, mods, gains, wg, *, bsz, nt):
    d = wg.shape[0]
    tile = TILE_ROWS[0]
    tm_spec = pl.BlockSpec((tile, d), lambda b, t: (t, b))
    return pl.pallas_call(
        _s5_post_kernel,
        out_shape=jax.ShapeDtypeStruct((bsz * nt * tile, d), f32),
        grid=(bsz, nt),
        in_specs=[tm_spec, tm_spec, _mods_spec(d), _full_spec(gains), _full_spec(wg)],
        out_specs=_tile_spec(d, nt),
        compiler_params=_cparams(("parallel", "arbitrary")),
        name="s5_glu",
    )(h_tm, y_tm, mods, gains, wg)


def _s5_weights(b_re, b_im, cf_re, cf_im, a_re, a_im, lanes, bsz):
    g, n, c = b_re.shape
    gps = S5_SLAB // c
    nslab = g // gps
    eye = jnp.eye(gps, dtype=f32)

    def drive(bm):
        t = bm.astype(f32).reshape(nslab, gps, n, c).transpose(0, 1, 3, 2)
        t = t[:, :, :, None, :] * eye[None, :, None, :, None]
        return t.reshape(nslab, gps * c, gps * n)

    wd = jnp.concatenate([drive(b_re), drive(b_im)], axis=-1).astype(bf16)

    def read(cm):
        t = cm.reshape(nslab, gps, c, n).transpose(0, 1, 3, 2)
        t = t[:, :, :, None, :] * eye[None, :, None, :, None]
        return t.reshape(nslab, gps * n, gps * c)

    wr = jnp.stack([jnp.concatenate([read(cf_re[dd]), -read(cf_im[dd])], axis=1) for dd in range(2)]).astype(bf16)

    def decay(a):
        t = a.reshape(2, (g * n) // lanes, 1, lanes)
        return jnp.broadcast_to(t, (2, (g * n) // lanes, bsz, lanes))

    return wd, wr, decay(a_re), decay(a_im)


def _na_bias_table(rpb):
    nh = rpb.shape[0]
    qc = jnp.arange(GRID_W)[:, None]
    kc = jnp.arange(GRID_W)[None, :]
    dc = jnp.clip(kc - qc, 1 - NA_COLS, NA_COLS - 1) + (NA_COLS - 1)
    q_start = jnp.clip(qc - NA_COLS // 2, 0, GRID_W - NA_COLS)
    in_win = (kc >= q_start) & (kc < q_start + NA_COLS)
    tab = rpb.astype(f32)[:, :, dc]
    tab = jnp.where(in_win[None, None], tab, NEG_BIG)
    offs = jnp.arange(NA_ROWS)[:, None] + jnp.arange(NA_ROWS)[None, :]
    t2 = tab[:, offs]
    t2 = t2.transpose(1, 0, 3, 2, 4).reshape(NA_ROWS, nh, GRID_W, NA_ROWS * GRID_W)
    return t2


def _na_kernel(q_ref, k_ref, v_ref, bias_ref, o_ref, *, lc):
    r = pl.program_id(1)
    rows_total = (k_ref.shape[0] - lc) // GRID_W
    kr0 = jnp.clip(r - NA_ROWS // 2, 0, rows_total - NA_ROWS)
    k0 = pl.multiple_of(lc + kr0 * GRID_W, GRID_W)
    nloc = NA_ROWS * GRID_W
    d = q_ref.shape[-1]
    dh = d // NA_HEADS
    lane = lax.broadcasted_iota(jnp.int32, (GRID_W, 2 * dh), 1)
    outs = []
    for p in range(NA_HEADS // 2):
        sl = slice(p * 2 * dh, (p + 1) * 2 * dh)
        qp = q_ref[:, sl]
        kl = k_ref[pl.ds(k0, nloc), sl]
        vl = v_ref[pl.ds(k0, nloc), sl]
        kc = k_ref[0:lc, sl]
        vc = v_ref[0:lc, sl]
        halves = []
        for sub in range(2):
            qm = jnp.where((lane >= dh) if sub else (lane < dh), qp, jnp.zeros_like(qp))
            s_loc = _dot_nt(qm, kl) + bias_ref[2 * p + sub]
            s_ctx = _dot_nt(qm, kc)
            m = jnp.maximum(jnp.max(s_loc, axis=-1, keepdims=True), jnp.max(s_ctx, axis=-1, keepdims=True))
            e_loc = jnp.exp(s_loc - m)
            e_ctx = jnp.exp(s_ctx - m)
            den = jnp.sum(e_loc, axis=-1, keepdims=True) + jnp.sum(e_ctx, axis=-1, keepdims=True)
            o = _dot(e_loc.astype(bf16), vl) + _dot(e_ctx.astype(bf16), vc)
            halves.append(o / den)
        outs.append(jnp.where(lane >= dh, halves[1], halves[0]))
    o_ref[...] = jnp.concatenate(outs, axis=-1).astype(o_ref.dtype)


def _na_core(q, k, v, bias, *, bsz, ls, lc):
    d = q.shape[1]
    nrow = (ls - lc) // GRID_W
    cblk = lc // GRID_W

    def bias_map(b, r):
        kr0 = jnp.clip(r - NA_ROWS // 2, 0, nrow - NA_ROWS)
        return (kr0 - r + NA_ROWS - 1, 0, 0, 0)

    kv_spec = pl.BlockSpec((ls, d), lambda b, r: (b, 0))
    return pl.pallas_call(
        functools.partial(_na_kernel, lc=lc),
        out_shape=jax.ShapeDtypeStruct((bsz * nrow * GRID_W, d), bf16),
        grid=(bsz, nrow),
        in_specs=[pl.BlockSpec((GRID_W, d), lambda b, r: (b * (ls // GRID_W) + cblk + r, 0)),
                  kv_spec, kv_spec,
                  pl.BlockSpec((None,) + bias.shape[1:], bias_map)],
        out_specs=pl.BlockSpec((GRID_W, d), lambda b, r: (b * nrow + r, 0)),
        compiler_params=_cparams(("parallel", "arbitrary")),
        name="na_core",
    )(q, k, v, bias)


def _na_post_kernel(h_ref, o_ref_in, mods_ref, gains_ref, wo_ref, out_ref):
    y = _dot(o_ref_in[...], wo_ref[...])
    out_ref[...] = _residual(h_ref[...], y, mods_ref, gains_ref, 0)


def _na_post(h, o, mods, gains, wo, *, bsz, nt):
    d = wo.shape[0]
    tile = TILE_ROWS[0]
    lat_spec = pl.BlockSpec((tile, d), lambda b, t: (b * (nt - 1) + t, 0))
    return pl.pallas_call(
        _na_post_kernel,
        out_shape=jax.ShapeDtypeStruct((bsz * (nt - 1) * tile, d), f32),
        grid=(bsz, nt - 1),
        in_specs=[pl.BlockSpec((tile, d), lambda b, t: (b * nt + t + 1, 0)), lat_spec,
                  pl.BlockSpec((None, N_MOD, d), lambda b, t: (b, 0, 0)), _full_spec(gains), _full_spec(wo)],
        out_specs=lat_spec,
        compiler_params=_cparams(("parallel", "arbitrary")),
        name="na_out",
    )(h, o, mods, gains, wo)


def _mlp_lat(h_lat, mods, gains, w1, w2, *, bsz, ntl):
    d = w1.shape[0]
    tile = TILE_ROWS[0]
    spec = pl.BlockSpec((tile, d), lambda b, t: (b * ntl + t, 0))
    return pl.pallas_call(
        functools.partial(_mlp_kernel, ff_chunk=min(1024, w1.shape[1])),
        out_shape=jax.ShapeDtypeStruct(h_lat.shape, f32),
        grid=(bsz, ntl),
        in_specs=[spec, pl.BlockSpec((None, N_MOD, d), lambda b, t: (b, 0, 0)),
                  _full_spec(gains), _full_spec(w1), _full_spec(w2)],
        out_specs=spec,
        compiler_params=_cparams(("parallel", "arbitrary")),
        name="relu2_mlp_lat",
    )(h_lat, mods, gains, w1, w2)


def kernel(x, c, ctx, c_ctx, ada_w, ada_b, norm_gains, mlp_w_in, mlp_w_out, sc_w_in, sc_conv, sc_w_out, hg_w_in, hg_lower_bound, hg_norm, hg_w_out, s5_lam_re, s5_lam_im, s5_log_dt, s5_b_re, s5_b_im, s5_c_re, s5_c_im, s5_d, s5_w_glu, na_w_qkv, na_rpb, na_w_out):
    bsz, seq, d = x.shape
    lc = ctx.shape[1]
    depth = ada_w.shape[0]
    tile = TILE_ROWS[0]
    assert depth == 4 and lc == tile and seq % tile == 0 and seq % GRID_W == 0 and bsz <= MOD_ROWS - 8
    assert bsz % 8 == 0 and tile % S5_TSTEPS == 0
    ls = lc + seq
    nt = ls // tile

    cc = jnp.zeros((MOD_ROWS, d), f32).at[:bsz].set(c.astype(f32)).at[MOD_ROWS - 8].set(c_ctx.astype(f32))
    mods_b = _mods(cc, ada_w, ada_b)
    mods_a = mods_b.transpose(0, 2, 1, 3)
    lb_all, a_re, a_im, cf_re, cf_im = _prep(hg_lower_bound, s5_lam_re[0], s5_lam_im[0], s5_log_dt[0],
                                             s5_c_re[0], s5_c_im[0])
    gains = norm_gains.astype(f32)
    w1 = mlp_w_in.astype(bf16)
    w2 = mlp_w_out.astype(bf16)
    xl = x.reshape(bsz * seq, d)
    xc = ctx.reshape(bsz * lc, d)

    bg, u = _proj((xc, xl), mods_a[0], gains[0], sc_w_in[0].astype(bf16), (bf16, bf16),
                  bsz=bsz, nt=nt, first=True, conv=True)
    h = _conv_post(xc, xl, bg, u, sc_conv[0].astype(f32), mods_a[0], gains[0], sc_w_out[0].astype(bf16), bsz=bsz, nt=nt)
    h = _mlp(h, mods_a[0], gains[0], w1[0], w2[0], bsz=bsz, nt=nt)

    q, v, gate, f_fwd, f_bwd = _proj(h, mods_a[1], gains[1], hg_w_in[0].astype(bf16), (bf16, bf16, bf16, f32, f32),
                                     bsz=bsz, nt=nt)
    lb1 = lb_all[1:2]
    o = _gla(q, v, f_fwd, lb1, None, bsz=bsz, nt=nt, reverse=False)
    o = _gla(q, v, f_bwd, lb1, o, bsz=bsz, nt=nt, reverse=True)
    h = _hg_post(h, o, gate, hg_norm[0].astype(f32).reshape(1, d), mods_a[1], gains[1], hg_w_out[0].astype(bf16),
                 bsz=bsz, nt=nt)
    h_tm = _mlp(h, mods_a[1], gains[1], w1[1], w2[1], bsz=bsz, nt=nt, out_layout="tm")

    lanes = 256
    wd, wr, ar, ai = _s5_weights(s5_b_re[0], s5_b_im[0], cf_re, cf_im, a_re, a_im, lanes, bsz)
    h_rows = h_tm.reshape(ls * bsz, d)
    nct, nlt = lc // S5_TSTEPS, seq // S5_TSTEPS
    dsk = s5_d[0].astype(f32).reshape(1, d)
    y = _s5_scan(h_rows, mods_b[2], gains[2], wd, wr[0], ar[0], ai[0], dsk, None, bsz=bsz, nct=nct, nlt=nlt, reverse=False)
    y = _s5_scan(h_rows, mods_b[2], gains[2], wd, wr[1], ar[1], ai[1], dsk, y, bsz=bsz, nct=nct, nlt=nlt, reverse=True)
    h = _s5_post(h_tm, y.reshape(ls, bsz * d), mods_a[2], gains[2], s5_w_glu[0].astype(bf16), bsz=bsz, nt=nt)
    h = _mlp(h, mods_a[2], gains[2], w1[2], w2[2], bsz=bsz, nt=nt)

    dh = d // NA_HEADS
    qq, kk, vv = _proj(h, mods_a[3], gains[3], na_w_qkv[0].astype(bf16), (bf16, bf16, bf16), bsz=bsz, nt=nt,
                       scales=(dh ** -0.5, 1.0, 1.0))
    bias = _na_bias_table(na_rpb[0])
    o = _na_core(qq, kk, vv, bias, bsz=bsz, ls=ls, lc=lc)
    hl = _na_post(h, o, mods_a[3], gains[3], na_w_out[0].astype(bf16), bsz=bsz, nt=nt)
    hl = _mlp_lat(hl, mods_a[3], gains[3], w1[3], w2[3], bsz=bsz, ntl=nt - 1)
    return hl.reshape(bsz, seq, d)
```

```python
import functools
import math

import jax
import jax.numpy as jnp
from jax import lax
from jax.experimental import pallas as pl
from jax.experimental.pallas import tpu as pltpu

EPS = 1e-6
N_MOD = 6
MOD_ROWS = 24
HG_HEAD_DIM = 128
HG_CHUNK = 32
S5_GROUP = 16
S5_SLAB = 256
S5_TSTEPS = 16
GRID_W = 64
NA_HEADS = 16
NA_ROWS = 8
NA_COLS = 16
NEG_BIG = -1e30
VMEM_LIMIT = 56 * 1024 * 1024

bf16 = jnp.bfloat16
f32 = jnp.float32


def _cparams(sem):
    return pltpu.CompilerParams(dimension_semantics=sem, vmem_limit_bytes=VMEM_LIMIT)


def _rms(x, g):
    return x * lax.rsqrt(jnp.mean(x * x, axis=-1, keepdims=True) + EPS) * g


def _dot(a, b):
    return jnp.dot(a, b, preferred_element_type=f32)


def _dot_nt(a, b):
    return lax.dot_general(a, b, (((1,), (1,)), ((), ())), preferred_element_type=f32)


def _dot_tn(a, b):
    return lax.dot_general(a, b, (((0,), (0,)), ((), ())), preferred_element_type=f32)


def _split3(x):
    hi = x.astype(bf16)
    r = x - hi.astype(f32)
    mid = r.astype(bf16)
    lo = (r - mid.astype(f32)).astype(bf16)
    return hi, mid, lo


def _mods_kernel(cc_ref, w_ref, b_ref, o_ref):
    x = cc_ref[...]
    a = (x * jax.nn.sigmoid(x)).astype(bf16)
    o_ref[...] = _dot(a, w_ref[...].astype(bf16)) + b_ref[...]


def _mods(cc, ada_w, ada_b):
    depth, d, _ = ada_w.shape
    out = pl.pallas_call(
        _mods_kernel,
        out_shape=jax.ShapeDtypeStruct((depth, N_MOD, MOD_ROWS, d), f32),
        grid=(depth, N_MOD),
        in_specs=[pl.BlockSpec((MOD_ROWS, d), lambda i, j: (0, 0)),
                  pl.BlockSpec((None, d, d), lambda i, j: (i, 0, j)),
                  pl.BlockSpec((None, 1, d), lambda i, j: (i * N_MOD + j, 0, 0))],
        out_specs=pl.BlockSpec((None, None, MOD_ROWS, d), lambda i, j: (i, j, 0, 0)),
        compiler_params=_cparams(("arbitrary", "arbitrary")),
        name="ada_mods",
    )(cc, ada_w, ada_b.reshape(depth * N_MOD, 1, d))
    return out


def _prep_kernel(lbp_ref, lre_ref, lim_ref, ldt_ref, cre_ref, cim_ref,
                 lb_ref, are_ref, aim_ref, cfre_ref, cfim_ref):
    x = lbp_ref[...]
    rows = [x[i:i + 1, :] for i in range(x.shape[0])]
    m = functools.reduce(jnp.maximum, rows)
    es = [jnp.exp(r - m) for r in rows]
    tot = functools.reduce(lambda a, b: a + b, es)
    acc = None
    first = None
    for i, e in enumerate(es):
        sm = e / tot
        acc = sm if acc is None else acc + sm
        if first is None:
            first = acc
        lb_ref[i:i + 1, :] = acc - first
    lam_re = jnp.minimum(lre_ref[...], -1e-4)
    lam_im = lim_ref[...]
    dt = jnp.exp(ldt_ref[...])
    mag = jnp.exp(lam_re * dt)
    a_re = mag * jnp.cos(lam_im * dt)
    a_im = mag * jnp.sin(lam_im * dt)
    den = lam_re * lam_re + lam_im * lam_im
    f_re = ((a_re - 1) * lam_re + a_im * lam_im) / den
    f_im = (a_im * lam_re - (a_re - 1) * lam_im) / den
    are_ref[...] = a_re
    aim_ref[...] = a_im
    c_re, c_im = cre_ref[...], cim_ref[...]
    cfre_ref[...] = c_re * f_re - c_im * f_im
    cfim_ref[...] = c_re * f_im + c_im * f_re


def _prep(hg_lower_bound, lam_re, lam_im, log_dt, c_re, c_im):
    shape = c_re.shape
    flat = (shape[0] * shape[1] * shape[2], shape[3])

    def expand(t):
        return jnp.broadcast_to(t[:, :, None, :], shape).reshape(flat)

    ldt = jnp.broadcast_to(log_dt[:, :, None, None], shape).reshape(flat)
    outs = pl.pallas_call(
        _prep_kernel,
        out_shape=[jax.ShapeDtypeStruct(hg_lower_bound.shape, f32)] + [jax.ShapeDtypeStruct(flat, f32)] * 4,
        name="param_prep",
    )(hg_lower_bound.astype(f32), expand(lam_re.astype(f32)), expand(lam_im.astype(f32)), ldt.astype(f32),
      c_re.astype(f32).reshape(flat), c_im.astype(f32).reshape(flat))
    lb, a_re, a_im, cf_re, cf_im = outs
    a_re = a_re.reshape(shape)[:, :, 0, :]
    a_im = a_im.reshape(shape)[:, :, 0, :]
    return lb, a_re, a_im, cf_re.reshape(shape), cf_im.reshape(shape)


def _mod_row(b, t):
    return jnp.where(t == 0, MOD_ROWS - 8, b)


def _mods_spec(d):
    return pl.BlockSpec((None, N_MOD, d), lambda b, t: (_mod_row(b, t), 0, 0))


def _full_spec(arr):
    nd = arr.ndim
    return pl.BlockSpec(arr.shape, lambda b, t: (0,) * nd)


def _tile_spec(d, nt, off=0):
    return pl.BlockSpec((TILE_ROWS[0], d), lambda b, t: (b * nt + t + off, 0))


TILE_ROWS = [256]


def _prenorm(h, mods_ref, gains_ref, which):
    g = gains_ref[2 * which:2 * which + 1, :]
    shift = mods_ref[3 * which:3 * which + 1, :]
    scale = mods_ref[3 * which + 1:3 * which + 2, :]
    return _rms(h, g) * (1 + scale) + shift


def _residual(h, y, mods_ref, gains_ref, which):
    g = gains_ref[2 * which + 1:2 * which + 2, :]
    gate = mods_ref[3 * which + 2:3 * which + 3, :]
    return h + gate * _rms(y, g)


def _first_layer_h(t, hc_ref, hl_ref):
    return jnp.where(t == 0, hc_ref[...], hl_ref[...])


def _proj_kernel(*refs, split, first, conv):
    if first:
        hc_ref, hl_ref, mods_ref, gains_ref, w_ref = refs[:5]
        outs = refs[5:]
        h = _first_layer_h(pl.program_id(1), hc_ref, hl_ref)
    else:
        h_ref, mods_ref, gains_ref, w_ref = refs[:4]
        outs = refs[4:]
        h = h_ref[...]
    d = h.shape[-1]
    a = _prenorm(h, mods_ref, gains_ref, 0).astype(bf16)
    res = _dot(a, w_ref[...])
    if conv:
        bg_ref, u_ref = outs
        bg_ref[...] = res[:, :d].astype(bg_ref.dtype)
        u_ref[...] = (res[:, d:2 * d] * res[:, 2 * d:3 * d]).astype(u_ref.dtype)
    else:
        for k, (o_ref, sc) in enumerate(zip(outs, split)):
            piece = res[:, k * d:(k + 1) * d]
            if sc != 1.0:
                piece = piece * sc
            o_ref[...] = piece.astype(o_ref.dtype)


def _proj(h_in, mods, gains, w, out_dtypes, *, bsz, nt, first=False, conv=False, scales=None):
    d = w.shape[0]
    tile = TILE_ROWS[0]
    rows = bsz * nt * tile
    if first:
        hc, hl = h_in
        in_arrays = [hc, hl]
        in_specs = [pl.BlockSpec((tile, d), lambda b, t: (b, 0)),
                    pl.BlockSpec((tile, d), lambda b, t: (b * (nt - 1) + jnp.maximum(t - 1, 0), 0))]
    else:
        in_arrays = [h_in]
        in_specs = [_tile_spec(d, nt)]
    in_arrays += [mods, gains, w]
    in_specs += [_mods_spec(d), _full_spec(gains), _full_spec(w)]
    scales = scales or (1.0,) * len(out_dtypes)
    return pl.pallas_call(
        functools.partial(_proj_kernel, split=tuple(scales), first=first, conv=conv),
        out_shape=[jax.ShapeDtypeStruct((rows, d), dt) for dt in out_dtypes],
        grid=(bsz, nt),
        in_specs=in_specs,
        out_specs=[_tile_spec(d, nt) for _ in out_dtypes],
        compiler_params=_cparams(("parallel", "arbitrary")),
        name="prenorm_proj",
    )(*in_arrays)


def _mlp_body(h, mods_ref, gains_ref, w1_ref, w2_ref, ff_chunk):
    a = _prenorm(h, mods_ref, gains_ref, 1).astype(bf16)
    d_ff = w1_ref.shape[1]
    acc = None
    for c in range(d_ff // ff_chunk):
        hid = jnp.maximum(_dot(a, w1_ref[:, c * ff_chunk:(c + 1) * ff_chunk]), 0.0)
        part = _dot((hid * hid).astype(bf16), w2_ref[c * ff_chunk:(c + 1) * ff_chunk, :])
        acc = part if acc is None else acc + part
    return _residual(h, acc, mods_ref, gains_ref, 1)


def _mlp_kernel(h_ref, mods_ref, gains_ref, w1_ref, w2_ref, o_ref, *, ff_chunk):
    o_ref[...] = _mlp_body(h_ref[...], mods_ref, gains_ref, w1_ref, w2_ref, ff_chunk)


def _mlp(h, mods, gains, w1, w2, *, bsz, nt, out_layout="bm", lat_only=False):
    d = w1.shape[0]
    tile = TILE_ROWS[0]
    t0 = 1 if lat_only else 0
    grid = (bsz, nt - t0)
    if lat_only:
        out_shape = jax.ShapeDtypeStruct((bsz * (nt - 1) * tile, d), f32)
        out_spec = pl.BlockSpec((tile, d), lambda b, t: (b * (nt - 1) + t, 0))
    elif out_layout == "tm":
        out_shape = jax.ShapeDtypeStruct((nt * tile, bsz * d), f32)
        out_spec = pl.BlockSpec((tile, d), lambda b, t: (t, b))
    else:
        out_shape = jax.ShapeDtypeStruct((bsz * nt * tile, d), f32)
        out_spec = _tile_spec(d, nt)
    return pl.pallas_call(
        functools.partial(_mlp_kernel, ff_chunk=min(1024, w1.shape[1])),
        out_shape=out_shape,
        grid=grid,
        in_specs=[pl.BlockSpec((tile, d), lambda b, t: (b * nt + t + t0, 0)),
                  pl.BlockSpec((None, N_MOD, d), lambda b, t: (_mod_row(b, t + t0), 0, 0)),
                  _full_spec(gains), _full_spec(w1), _full_spec(w2)],
        out_specs=out_spec,
        compiler_params=_cparams(("parallel", "arbitrary")),
        name="relu2_mlp",
    )(h, mods, gains, w1, w2)


def _conv_post_kernel(hc_ref, hl_ref, bg_ref, u_ref, up_ref, un_ref, cw_ref, mods_ref, gains_ref, wo_ref, o_ref, *, nt):
    t = pl.program_id(1)
    h = _first_layer_h(t, hc_ref, hl_ref)
    u = u_ref[...].astype(f32)
    rows = u.shape[0]
    hal = up_ref.shape[0]
    prev_row = jnp.where(t <= 1, 0.0, up_ref[hal - 1:hal, :].astype(f32))
    next_row = jnp.where((t == 0) | (t == nt - 1), 0.0, un_ref[0:1, :].astype(f32))
    ridx = lax.broadcasted_iota(jnp.int32, u.shape, 0)
    u_prev = jnp.where(ridx == 0, prev_row, pltpu.roll(u, 1, 0))
    u_next = jnp.where(ridx == rows - 1, next_row, pltpu.roll(u, rows - 1, 0))
    cw = cw_ref[...]
    conv = cw[0:1, :] * u_prev + cw[1:2, :] * u + cw[2:3, :] * u_next
    y = _dot((bg_ref[...].astype(f32) * conv).astype(bf16), wo_ref[...])
    o_ref[...] = _residual(h, y, mods_ref, gains_ref, 0)


def _conv_post(hc, hl, bg, u, conv_w, mods, gains, wo, *, bsz, nt):
    d = wo.shape[0]
    tile = TILE_ROWS[0]
    hal = 16
    per = tile // hal
    nblk = bsz * nt * per
    return pl.pallas_call(
        functools.partial(_conv_post_kernel, nt=nt),
        out_shape=jax.ShapeDtypeStruct((bsz * nt * tile, d), f32),
        grid=(bsz, nt),
        in_specs=[pl.BlockSpec((tile, d), lambda b, t: (b, 0)),
                  pl.BlockSpec((tile, d), lambda b, t: (b * (nt - 1) + jnp.maximum(t - 1, 0), 0)),
                  _tile_spec(d, nt), _tile_spec(d, nt),
                  pl.BlockSpec((hal, d), lambda b, t: (jnp.maximum((b * nt + t) * per - 1, 0), 0)),
                  pl.BlockSpec((hal, d), lambda b, t: (jnp.minimum((b * nt + t + 1) * per, nblk - 1), 0)),
                  _full_spec(conv_w), _mods_spec(d), _full_spec(gains), _full_spec(wo)],
        out_specs=_tile_spec(d, nt),
        compiler_params=_cparams(("parallel", "arbitrary")),
        name="conv_gate_out",
    )(hc, hl, bg, u, u, u, conv_w, mods, gains, wo)


def _gla_kernel(*refs, reverse, accumulate):
    if accumulate:
        q_ref, v_ref, f_ref, lb_ref, prev_ref, o_ref, st_ref, qin_s, kin_s, qout_s, kst_s, dec_s, kv_s, sc_s = refs
    else:
        q_ref, v_ref, f_ref, lb_ref, o_ref, st_ref, qin_s, kin_s, qout_s, kst_s, dec_s, kv_s, sc_s = refs
        prev_ref = None
    tile, d = q_ref.shape
    nh = d // HG_HEAD_DIM
    c = HG_CHUNK
    nchunk = tile // c

    @pl.when(pl.program_id(1) == 0)
    def _():
        st_ref[...] = jnp.zeros_like(st_ref)

    ri = lax.broadcasted_iota(jnp.int32, (c, c), 0)
    ci = lax.broadcasted_iota(jnp.int32, (c, c), 1)
    causal = (ci >= ri) if reverse else (ci <= ri)
    tri = jnp.where(causal, 1.0, 0.0).astype(bf16)
    mid = c // 2 if reverse else c // 2 - 1
    last = 0 if reverse else c - 1
    lb = lb_ref[...]

    for k in range(nchunk):
        rows = slice(k * c, (k + 1) * c)
        fg = lb + (1 - lb) * jax.nn.sigmoid(f_ref[rows, :])
        kk = 1 - fg
        hi, md, lo = _split3(jnp.log(fg))
        bcum = _dot(tri, hi) + _dot(tri, md) + _dot(tri, lo)
        b_mid = bcum[mid:mid + 1, :]
        b_last = bcum[last:last + 1, :]
        qq = q_ref[rows, :].astype(f32)
        qin_s[rows, :] = (qq * jnp.exp(bcum - b_mid)).astype(bf16)
        kin_s[rows, :] = (kk * jnp.exp(b_mid - bcum)).astype(bf16)
        qout_s[rows, :] = (qq * jnp.exp(bcum)).astype(bf16)
        kst_s[rows, :] = (kk * jnp.exp(b_last - bcum)).astype(bf16)
        dec_s[k:k + 1, :] = jnp.exp(b_last)

    units = [(k, h) for k in range(nchunk) for h in range(nh)]
    for k, h in units:
        rows = slice(k * c, (k + 1) * c)
        sl = slice(h * HG_HEAD_DIM, (h + 1) * HG_HEAD_DIM)
        sc = _dot_nt(qin_s[rows, sl], kin_s[rows, sl])
        sc_s[k, h] = jnp.where(causal, sc, 0.0).astype(bf16)
    for k, h in units:
        rows = slice(k * c, (k + 1) * c)
        sl = slice(h * HG_HEAD_DIM, (h + 1) * HG_HEAD_DIM)
        o_h = _dot(sc_s[k, h], v_ref[rows, sl])
        if prev_ref is not None:
            o_h = o_h + prev_ref[rows, sl]
        o_ref[rows, sl] = o_h
    for k, h in units:
        rows = slice(k * c, (k + 1) * c)
        sl = slice(h * HG_HEAD_DIM, (h + 1) * HG_HEAD_DIM)
        kv_s[k, h] = _dot_tn(v_ref[rows, sl], kst_s[rows, sl])

    for h in range(nh):
        sl = slice(h * HG_HEAD_DIM, (h + 1) * HG_HEAD_DIM)
        st = st_ref[h]
        for i in range(nchunk):
            k = (nchunk - 1 - i) if reverse else i
            rows = slice(k * c, (k + 1) * c)
            o_ref[rows, sl] += _dot_nt(qout_s[rows, sl], st.astype(bf16))
            st = st * dec_s[k:k + 1, sl] + kv_s[k, h]
        st_ref[h] = st


def _gla(q, v, fraw, lb, prev, *, bsz, nt, reverse):
    d = q.shape[1]
    tile = TILE_ROWS[0]
    nh = d // HG_HEAD_DIM

    def tmap(b, s):
        t = jnp.where(s == 0, 0, nt - s) if reverse else s
        return (b * nt + t, 0)

    spec = pl.BlockSpec((tile, d), tmap)
    arrays = [q, v, fraw, lb]
    specs = [spec, spec, spec, _full_spec(lb)]
    aliases = {}
    if prev is not None:
        arrays.append(prev)
        specs.append(spec)
        aliases = {4: 0}
    return pl.pallas_call(
        functools.partial(_gla_kernel, reverse=reverse, accumulate=prev is not None),
        out_shape=jax.ShapeDtypeStruct(q.shape, f32),
        grid=(bsz, nt),
        in_specs=specs,
        out_specs=spec,
        scratch_shapes=[pltpu.VMEM((nh, HG_HEAD_DIM, HG_HEAD_DIM), f32)]
        + [pltpu.VMEM((tile, d), bf16)] * 4
        + [pltpu.VMEM((tile // HG_CHUNK, d), f32),
           pltpu.VMEM((tile // HG_CHUNK, nh, HG_HEAD_DIM, HG_HEAD_DIM), f32),
           pltpu.VMEM((tile // HG_CHUNK, nh, HG_CHUNK, HG_CHUNK), bf16)],
        input_output_aliases=aliases,
        compiler_params=_cparams(("parallel", "arbitrary")),
        name="hgrn2_gla_bwd" if reverse else "hgrn2_gla_fwd",
    )(*arrays)


def _hg_post_kernel(h_ref, o_ref_in, gate_ref, gn_ref, mods_ref, gains_ref, wo_ref, out_ref):
    o = o_ref_in[...]
    d = o.shape[-1]
    gn = gn_ref[...]
    pieces = []
    for hh in range(d // HG_HEAD_DIM):
        sl = slice(hh * HG_HEAD_DIM, (hh + 1) * HG_HEAD_DIM)
        pieces.append(_rms(o[:, sl], gn[:, sl]))
    on = jnp.concatenate(pieces, axis=-1)
    g = gate_ref[...].astype(f32)
    y = _dot((on * (g * jax.nn.sigmoid(g))).astype(bf16), wo_ref[...])
    out_ref[...] = _residual(h_ref[...], y, mods_ref, gains_ref, 0)


def _hg_post(h, o, gate, gnorm, mods, gains, wo, *, bsz, nt):
    d = wo.shape[0]
    return pl.pallas_call(
        _hg_post_kernel,
        out_shape=jax.ShapeDtypeStruct(h.shape, f32),
        grid=(bsz, nt),
        in_specs=[_tile_spec(d, nt), _tile_spec(d, nt), _tile_spec(d, nt), _full_spec(gnorm),
                  _mods_spec(d), _full_spec(gains), _full_spec(wo)],
        out_specs=_tile_spec(d, nt),
        compiler_params=_cparams(("parallel", "arbitrary")),
        name="hgrn2_readout",
    )(h, o, gate, gnorm, mods, gains, wo)


def _s5_kernel(*refs, reverse, accumulate, nct, bsz):
    if accumulate:
        (h_ref, modsa_ref, gains_ref, wd_ref, wr_ref, are_ref, aim_ref, dsk_ref, prev_ref,
         y_ref, z_ref, st_ref, ytb_ref) = refs
    else:
        h_ref, modsa_ref, gains_ref, wd_ref, wr_ref, are_ref, aim_ref, dsk_ref, y_ref, z_ref, st_ref, ytb_ref = refs
        prev_ref = None
    s = pl.program_id(0)
    _, ts, d = h_ref.shape
    rows = bsz * ts
    nslab = d // S5_SLAB
    lanes = z_ref.shape[2]
    nchunk = z_ref.shape[0] // 2
    per_slab = nchunk // nslab

    @pl.when(s == 0)
    def _():
        st_ref[...] = jnp.zeros_like(st_ref)

    is_ctx = s < nct
    h3 = h_ref[...]
    g = gains_ref[0:1, :]
    crow = slice(MOD_ROWS - 8, MOD_ROWS - 7)
    shift = jnp.where(is_ctx, modsa_ref[crow, 0:1, :], modsa_ref[0:bsz, 0:1, :])
    scale = jnp.where(is_ctx, modsa_ref[crow, 1:2, :], modsa_ref[0:bsz, 1:2, :])
    u3 = h3 * lax.rsqrt(jnp.mean(h3 * h3, axis=-1, keepdims=True) + EPS) * g * (1 + scale) + shift
    u = u3.reshape(rows, d)
    ro = lax.broadcasted_iota(jnp.int32, (rows, rows), 0)
    co = lax.broadcasted_iota(jnp.int32, (rows, rows), 1)
    to_tb = jnp.where((ro // bsz == co % ts) & (ro % bsz == co // ts), 1.0, 0.0).astype(bf16)
    to_bt = jnp.where((co // bsz == ro % ts) & (co % bsz == ro // ts), 1.0, 0.0).astype(bf16)
    ub = _dot(to_tb, u.astype(bf16)).astype(bf16)

    def drive(kb):
        v = _dot(ub[:, kb * S5_SLAB:(kb + 1) * S5_SLAB], wd_ref[kb])
        for cc in range(per_slab):
            z_ref[kb * per_slab + cc] = v[:, cc * lanes:(cc + 1) * lanes]
            z_ref[nchunk + kb * per_slab + cc] = v[:, (per_slab + cc) * lanes:(per_slab + cc + 1) * lanes]

    def scan(ck):
        ar = are_ref[ck]
        ai = aim_ref[ck]
        zr = st_ref[ck]
        zi = st_ref[nchunk + ck]
        for t in (range(ts - 1, -1, -1) if reverse else range(ts)):
            rs = slice(t * bsz, (t + 1) * bsz)
            nzr = ar * zr - ai * zi + z_ref[ck, rs, :]
            nzi = ar * zi + ai * zr + z_ref[nchunk + ck, rs, :]
            zr, zi = nzr, nzi
            z_ref[ck, rs, :] = zr
            z_ref[nchunk + ck, rs, :] = zi
        st_ref[ck] = zr
        st_ref[nchunk + ck] = zi

    def readout(kb):
        acc = None
        for cc in range(per_slab):
            ck = kb * per_slab + cc
            p = (_dot(z_ref[ck].astype(bf16), wr_ref[kb, cc * lanes:(cc + 1) * lanes, :])
                 + _dot(z_ref[nchunk + ck].astype(bf16), wr_ref[kb, (per_slab + cc) * lanes:(per_slab + cc + 1) * lanes, :]))
            acc = p if acc is None else acc + p
        ytb_ref[:, kb * S5_SLAB:(kb + 1) * S5_SLAB] = acc

    drive(0)
    for kb in range(nslab):
        if kb + 1 < nslab:
            drive(kb + 1)
        for cc in range(per_slab):
            scan(kb * per_slab + cc)
        readout(kb)

    ytb = ytb_ref[...]
    hi = ytb.astype(bf16)
    lo = (ytb - hi.astype(f32)).astype(bf16)
    y = _dot(to_bt, hi) + _dot(to_bt, lo)
    if prev_ref is not None:
        y = y + prev_ref[...].reshape(rows, d)
    else:
        y = y + dsk_ref[...] * u
    y_ref[...] = y.reshape(bsz, ts, d)


def _s5_scan(h, modsa, gains, wd, wr, a_re, a_im, dskip, prev, *, bsz, nct, nlt, reverse):
    _, ls, d = h.shape
    ts = S5_TSTEPS
    ntot = nct + nlt
    nchunk2, lanes = a_re.shape[0] * 2, a_re.shape[2]

    def tmap(s):
        if reverse:
            return (0, jnp.where(s < nct, nct - 1 - s, ntot + nct - 1 - s), 0)
        return (0, s, 0)

    spec = pl.BlockSpec((bsz, ts, d), tmap)

    def full(arr):
        nd = arr.ndim
        return pl.BlockSpec(arr.shape, lambda s: (0,) * nd)

    arrays = [h, modsa, gains, wd, wr, a_re, a_im, dskip]
    specs = [spec, full(modsa), full(gains), full(wd), full(wr), full(a_re), full(a_im), full(dskip)]
    aliases = {}
    if prev is not None:
        arrays.append(prev)
        specs.append(spec)
        aliases = {8: 0}
    return pl.pallas_call(
        functools.partial(_s5_kernel, reverse=reverse, accumulate=prev is not None, nct=nct, bsz=bsz),
        out_shape=jax.ShapeDtypeStruct(h.shape, f32),
        grid=(ntot,),
        in_specs=specs,
        out_specs=spec,
        scratch_shapes=[pltpu.VMEM((nchunk2, bsz * ts, lanes), f32), pltpu.VMEM((nchunk2, bsz, lanes), f32),
                        pltpu.VMEM((bsz * ts, d), f32)],
        input_output_aliases=aliases,
        compiler_params=_cparams(("arbitrary",)),
        name="s5_scan_bwd" if reverse else "s5_scan_fwd",
    )(*arrays)


def _s5_post_kernel(h_ref, y_ref, mods_ref, gains_ref, wg_ref, o_ref):
    d = h_ref.shape[-1]
    z = jax.nn.gelu(y_ref[...]).astype(bf16)
    r = _dot(z, wg_ref[...])
    out = r[:, :d] * jax.nn.sigmoid(r[:, d:])
    o_ref[...] = _residual(h_ref[...], out, mods_ref, gains_ref, 0)


def _s5_post(h, y, mods, gains, wg, *, bsz, nt):
    d = wg.shape[0]
    return pl.pallas_call(
        _s5_post_kernel,
        out_shape=jax.ShapeDtypeStruct(h.shape, f32),
        grid=(bsz, nt),
        in_specs=[_tile_spec(d, nt), _tile_spec(d, nt), _mods_spec(d), _full_spec(gains), _full_spec(wg)],
        out_specs=_tile_spec(d, nt),
        compiler_params=_cparams(("parallel", "arbitrary")),
        name="s5_glu",
    )(h, y, mods, gains, wg)


def _s5_weights(b_re, b_im, cf_re, cf_im, a_re, a_im, lanes, bsz):
    g, n, c = b_re.shape
    gps = S5_SLAB // c
    nslab = g // gps
    eye = jnp.eye(gps, dtype=f32)

    def drive(bm):
        t = bm.astype(f32).reshape(nslab, gps, n, c).transpose(0, 1, 3, 2)
        t = t[:, :, :, None, :] * eye[None, :, None, :, None]
        return t.reshape(nslab, gps * c, gps * n)

    wd = jnp.concatenate([drive(b_re), drive(b_im)], axis=-1).astype(bf16)

    def read(cm):
        t = cm.reshape(nslab, gps, c, n).transpose(0, 1, 3, 2)
        t = t[:, :, :, None, :] * eye[None, :, None, :, None]
        return t.reshape(nslab, gps * n, gps * c)

    wr = jnp.stack([jnp.concatenate([read(cf_re[dd]), -read(cf_im[dd])], axis=1) for dd in range(2)]).astype(bf16)

    def decay(a):
        t = a.reshape(2, (g * n) // lanes, 1, lanes)
        return jnp.broadcast_to(t, (2, (g * n) // lanes, bsz, lanes))

    return wd, wr, decay(a_re), decay(a_im)


def _na_bias_table(rpb):
    nh = rpb.shape[0]
    qc = jnp.arange(GRID_W)[:, None]
    kc = jnp.arange(GRID_W)[None, :]
    dc = jnp.clip(kc - qc, 1 - NA_COLS, NA_COLS - 1) + (NA_COLS - 1)
    q_start = jnp.clip(qc - NA_COLS // 2, 0, GRID_W - NA_COLS)
    in_win = (kc >= q_start) & (kc < q_start + NA_COLS)
    tab = rpb.astype(f32)[:, :, dc]
    tab = jnp.where(in_win[None, None], tab, NEG_BIG)
    offs = jnp.arange(NA_ROWS)[:, None] + jnp.arange(NA_ROWS)[None, :]
    t2 = tab[:, offs]
    t2 = t2.transpose(1, 0, 3, 2, 4).reshape(NA_ROWS, nh, GRID_W, NA_ROWS * GRID_W)
    return t2


def _na_kernel(q_ref, k_ref, v_ref, bias_ref, o_ref, s_scr, p_scr, l_scr, *, lc):
    r = pl.program_id(1)
    rows_total = (k_ref.shape[0] - lc) // GRID_W
    kr0 = jnp.clip(r - NA_ROWS // 2, 0, rows_total - NA_ROWS)
    k0 = pl.multiple_of(lc + kr0 * GRID_W, GRID_W)
    nloc = NA_ROWS * GRID_W
    d = q_ref.shape[-1]
    dh = d // NA_HEADS
    lane = lax.broadcasted_iota(jnp.int32, (GRID_W, 2 * dh), 1)
    for p in range(NA_HEADS // 2):
        sl = slice(p * 2 * dh, (p + 1) * 2 * dh)
        qp = q_ref[:, sl]
        kl = k_ref[pl.ds(k0, nloc), sl]
        kc = k_ref[0:lc, sl]
        for sub in range(2):
            h = 2 * p + sub
            qm = jnp.where((lane >= dh) if sub else (lane < dh), qp, jnp.zeros_like(qp))
            s_scr[h, :, 0:nloc] = _dot_nt(qm, kl) + bias_ref[h]
            s_scr[h, :, nloc:nloc + lc] = _dot_nt(qm, kc)
    for h in range(NA_HEADS):
        s = s_scr[h]
        m = jnp.max(s, axis=-1, keepdims=True)
        e = jnp.exp(s - m)
        l_scr[h] = jnp.broadcast_to(jnp.sum(e, axis=-1, keepdims=True), (GRID_W, 2 * dh))
        p_scr[h] = e.astype(bf16)
    for p in range(NA_HEADS // 2):
        sl = slice(p * 2 * dh, (p + 1) * 2 * dh)
        vl = v_ref[pl.ds(k0, nloc), sl]
        vc = v_ref[0:lc, sl]
        halves = []
        for sub in range(2):
            h = 2 * p + sub
            o = _dot(p_scr[h, :, 0:nloc], vl) + _dot(p_scr[h, :, nloc:nloc + lc], vc)
            halves.append(o / l_scr[h])
        o_ref[:, sl] = jnp.where(lane >= dh, halves[1], halves[0]).astype(o_ref.dtype)


def _na_core(q, k, v, bias, *, bsz, ls, lc):
    d = q.shape[1]
    nrow = (ls - lc) // GRID_W
    cblk = lc // GRID_W

    def bias_map(b, r):
        kr0 = jnp.clip(r - NA_ROWS // 2, 0, nrow - NA_ROWS)
        return (kr0 - r + NA_ROWS - 1, 0, 0, 0)

    kv_spec = pl.BlockSpec((ls, d), lambda b, r: (b, 0))
    return pl.pallas_call(
        functools.partial(_na_kernel, lc=lc),
        out_shape=jax.ShapeDtypeStruct((bsz * nrow * GRID_W, d), bf16),
        grid=(bsz, nrow),
        in_specs=[pl.BlockSpec((GRID_W, d), lambda b, r: (b * (ls // GRID_W) + cblk + r, 0)),
                  kv_spec, kv_spec,
                  pl.BlockSpec((None,) + bias.shape[1:], bias_map)],
        out_specs=pl.BlockSpec((GRID_W, d), lambda b, r: (b * nrow + r, 0)),
        scratch_shapes=[pltpu.VMEM((NA_HEADS, GRID_W, NA_ROWS * GRID_W + lc), f32),
                        pltpu.VMEM((NA_HEADS, GRID_W, NA_ROWS * GRID_W + lc), bf16),
                        pltpu.VMEM((NA_HEADS, GRID_W, 2 * (d // NA_HEADS)), f32)],
        compiler_params=_cparams(("parallel", "arbitrary")),
        name="na_core",
    )(q, k, v, bias)


def _na_post_kernel(h_ref, o_ref_in, mods_ref, gains_ref, wo_ref, out_ref):
    y = _dot(o_ref_in[...], wo_ref[...])
    out_ref[...] = _residual(h_ref[...], y, mods_ref, gains_ref, 0)


def _na_post(h, o, mods, gains, wo, *, bsz, nt):
    d = wo.shape[0]
    tile = TILE_ROWS[0]
    lat_spec = pl.BlockSpec((tile, d), lambda b, t: (b * (nt - 1) + t, 0))
    return pl.pallas_call(
        _na_post_kernel,
        out_shape=jax.ShapeDtypeStruct((bsz * (nt - 1) * tile, d), f32),
        grid=(bsz, nt - 1),
        in_specs=[pl.BlockSpec((tile, d), lambda b, t: (b * nt + t + 1, 0)), lat_spec,
                  pl.BlockSpec((None, N_MOD, d), lambda b, t: (b, 0, 0)), _full_spec(gains), _full_spec(wo)],
        out_specs=lat_spec,
        compiler_params=_cparams(("parallel", "arbitrary")),
        name="na_out",
    )(h, o, mods, gains, wo)


def _mlp_lat(h_lat, mods, gains, w1, w2, *, bsz, ntl):
    d = w1.shape[0]
    tile = TILE_ROWS[0]
    spec = pl.BlockSpec((tile, d), lambda b, t: (b * ntl + t, 0))
    return pl.pallas_call(
        functools.partial(_mlp_kernel, ff_chunk=min(1024, w1.shape[1])),
        out_shape=jax.ShapeDtypeStruct(h_lat.shape, f32),
        grid=(bsz, ntl),
        in_specs=[spec, pl.BlockSpec((None, N_MOD, d), lambda b, t: (b, 0, 0)),
                  _full_spec(gains), _full_spec(w1), _full_spec(w2)],
        out_specs=spec,
        compiler_params=_cparams(("parallel", "arbitrary")),
        name="relu2_mlp_lat",
    )(h_lat, mods, gains, w1, w2)


def kernel(x, c, ctx, c_ctx, ada_w, ada_b, norm_gains, mlp_w_in, mlp_w_out, sc_w_in, sc_conv, sc_w_out, hg_w_in, hg_lower_bound, hg_norm, hg_w_out, s5_lam_re, s5_lam_im, s5_log_dt, s5_b_re, s5_b_im, s5_c_re, s5_c_im, s5_d, s5_w_glu, na_w_qkv, na_rpb, na_w_out):
    bsz, seq, d = x.shape
    lc = ctx.shape[1]
    depth = ada_w.shape[0]
    tile = TILE_ROWS[0]
    assert depth == 4 and lc == tile and seq % tile == 0 and seq % GRID_W == 0 and bsz <= MOD_ROWS - 8
    assert bsz % 8 == 0 and tile % S5_TSTEPS == 0
    ls = lc + seq
    nt = ls // tile

    cc = jnp.zeros((MOD_ROWS, d), f32).at[:bsz].set(c.astype(f32)).at[MOD_ROWS - 8].set(c_ctx.astype(f32))
    mods_b = _mods(cc, ada_w, ada_b)
    mods_a = mods_b.transpose(0, 2, 1, 3)
    lb_all, a_re, a_im, cf_re, cf_im = _prep(hg_lower_bound, s5_lam_re[0], s5_lam_im[0], s5_log_dt[0],
                                             s5_c_re[0], s5_c_im[0])
    gains = norm_gains.astype(f32)
    w1 = mlp_w_in.astype(bf16)
    w2 = mlp_w_out.astype(bf16)
    xl = x.reshape(bsz * seq, d)
    xc = ctx.reshape(bsz * lc, d)

    bg, u = _proj((xc, xl), mods_a[0], gains[0], sc_w_in[0].astype(bf16), (bf16, bf16),
                  bsz=bsz, nt=nt, first=True, conv=True)
    h = _conv_post(xc, xl, bg, u, sc_conv[0].astype(f32), mods_a[0], gains[0], sc_w_out[0].astype(bf16), bsz=bsz, nt=nt)
    h = _mlp(h, mods_a[0], gains[0], w1[0], w2[0], bsz=bsz, nt=nt)

    q, v, gate, f_fwd, f_bwd = _proj(h, mods_a[1], gains[1], hg_w_in[0].astype(bf16), (bf16, bf16, bf16, f32, f32),
                                     bsz=bsz, nt=nt)
    lb1 = lb_all[1:2]
    o = _gla(q, v, f_fwd, lb1, None, bsz=bsz, nt=nt, reverse=False)
    o = _gla(q, v, f_bwd, lb1, o, bsz=bsz, nt=nt, reverse=True)
    h = _hg_post(h, o, gate, hg_norm[0].astype(f32).reshape(1, d), mods_a[1], gains[1], hg_w_out[0].astype(bf16),
                 bsz=bsz, nt=nt)
    h = _mlp(h, mods_a[1], gains[1], w1[1], w2[1], bsz=bsz, nt=nt)

    lanes = 256
    wd, wr, ar, ai = _s5_weights(s5_b_re[0], s5_b_im[0], cf_re, cf_im, a_re, a_im, lanes, bsz)
    h3 = h.reshape(bsz, ls, d)
    nct, nlt = lc // S5_TSTEPS, seq // S5_TSTEPS
    dsk = s5_d[0].astype(f32).reshape(1, d)
    y = _s5_scan(h3, mods_a[2], gains[2], wd, wr[0], ar[0], ai[0], dsk, None, bsz=bsz, nct=nct, nlt=nlt, reverse=False)
    y = _s5_scan(h3, mods_a[2], gains[2], wd, wr[1], ar[1], ai[1], dsk, y, bsz=bsz, nct=nct, nlt=nlt, reverse=True)
    h = _s5_post(h, y.reshape(bsz * ls, d), mods_a[2], gains[2], s5_w_glu[0].astype(bf16), bsz=bsz, nt=nt)
    h = _mlp(h, mods_a[2], gains[2], w1[2], w2[2], bsz=bsz, nt=nt)

    dh = d // NA_HEADS
    qq, kk, vv = _proj(h, mods_a[3], gains[3], na_w_qkv[0].astype(bf16), (bf16, bf16, bf16), bsz=bsz, nt=nt,
                       scales=(dh ** -0.5, 1.0, 1.0))
    bias = _na_bias_table(na_rpb[0])
    o = _na_core(qq, kk, vv, bias, bsz=bsz, ls=ls, lc=lc)
    hl = _na_post(h, o, mods_a[3], gains[3], na_w_out[0].astype(bf16), bsz=bsz, nt=nt)
    hl = _mlp_lat(hl, mods_a[3], gains[3], w1[3], w2[3], bsz=bsz, ntl=nt - 1)
    return hl.reshape(bsz, seq, d)
```

```python
import functools
import math

import jax
import jax.numpy as jnp
from jax import lax
from jax.experimental import pallas as pl
from jax.experimental.pallas import tpu as pltpu

EPS = 1e-6
N_MOD = 6
MOD_ROWS = 24
HG_HEAD_DIM = 128
HG_CHUNK = 32
S5_GROUP = 16
S5_SLAB = 256
S5_TSTEPS = 16
GRID_W = 64
NA_HEADS = 16
NA_ROWS = 8
NA_COLS = 16
NEG_BIG = -1e30
VMEM_LIMIT = 56 * 1024 * 1024

bf16 = jnp.bfloat16
f32 = jnp.float32


def _cparams(sem):
    return pltpu.CompilerParams(dimension_semantics=sem, vmem_limit_bytes=VMEM_LIMIT)


def _rms(x, g):
    return x * lax.rsqrt(jnp.mean(x * x, axis=-1, keepdims=True) + EPS) * g


def _dot(a, b):
    return jnp.dot(a, b, preferred_element_type=f32)


def _dot_nt(a, b):
    return lax.dot_general(a, b, (((1,), (1,)), ((), ())), preferred_element_type=f32)


def _dot_tn(a, b):
    return lax.dot_general(a, b, (((0,), (0,)), ((), ())), preferred_element_type=f32)


def _split3(x):
    hi = x.astype(bf16)
    r = x - hi.astype(f32)
    mid = r.astype(bf16)
    lo = (r - mid.astype(f32)).astype(bf16)
    return hi, mid, lo


def _mods_kernel(cc_ref, w_ref, b_ref, o_ref):
    x = cc_ref[...]
    a = (x * jax.nn.sigmoid(x)).astype(bf16)
    o_ref[...] = _dot(a, w_ref[...].astype(bf16)) + b_ref[...]


def _mods(cc, ada_w, ada_b):
    depth, d, _ = ada_w.shape
    out = pl.pallas_call(
        _mods_kernel,
        out_shape=jax.ShapeDtypeStruct((depth, N_MOD, MOD_ROWS, d), f32),
        grid=(depth, N_MOD),
        in_specs=[pl.BlockSpec((MOD_ROWS, d), lambda i, j: (0, 0)),
                  pl.BlockSpec((None, d, d), lambda i, j: (i, 0, j)),
                  pl.BlockSpec((None, 1, d), lambda i, j: (i * N_MOD + j, 0, 0))],
        out_specs=pl.BlockSpec((None, None, MOD_ROWS, d), lambda i, j: (i, j, 0, 0)),
        compiler_params=_cparams(("arbitrary", "arbitrary")),
        name="ada_mods",
    )(cc, ada_w, ada_b.reshape(depth * N_MOD, 1, d))
    return out


def _prep_kernel(lbp_ref, lre_ref, lim_ref, ldt_ref, cre_ref, cim_ref,
                 lb_ref, are_ref, aim_ref, cfre_ref, cfim_ref):
    x = lbp_ref[...]
    rows = [x[i:i + 1, :] for i in range(x.shape[0])]
    m = functools.reduce(jnp.maximum, rows)
    es = [jnp.exp(r - m) for r in rows]
    tot = functools.reduce(lambda a, b: a + b, es)
    acc = None
    first = None
    for i, e in enumerate(es):
        sm = e / tot
        acc = sm if acc is None else acc + sm
        if first is None:
            first = acc
        lb_ref[i:i + 1, :] = acc - first
    lam_re = jnp.minimum(lre_ref[...], -1e-4)
    lam_im = lim_ref[...]
    dt = jnp.exp(ldt_ref[...])
    mag = jnp.exp(lam_re * dt)
    a_re = mag * jnp.cos(lam_im * dt)
    a_im = mag * jnp.sin(lam_im * dt)
    den = lam_re * lam_re + lam_im * lam_im
    f_re = ((a_re - 1) * lam_re + a_im * lam_im) / den
    f_im = (a_im * lam_re - (a_re - 1) * lam_im) / den
    are_ref[...] = a_re
    aim_ref[...] = a_im
    c_re, c_im = cre_ref[...], cim_ref[...]
    cfre_ref[...] = c_re * f_re - c_im * f_im
    cfim_ref[...] = c_re * f_im + c_im * f_re


def _prep(hg_lower_bound, lam_re, lam_im, log_dt, c_re, c_im):
    shape = c_re.shape
    flat = (shape[0] * shape[1] * shape[2], shape[3])

    def expand(t):
        return jnp.broadcast_to(t[:, :, None, :], shape).reshape(flat)

    ldt = jnp.broadcast_to(log_dt[:, :, None, None], shape).reshape(flat)
    outs = pl.pallas_call(
        _prep_kernel,
        out_shape=[jax.ShapeDtypeStruct(hg_lower_bound.shape, f32)] + [jax.ShapeDtypeStruct(flat, f32)] * 4,
        name="param_prep",
    )(hg_lower_bound.astype(f32), expand(lam_re.astype(f32)), expand(lam_im.astype(f32)), ldt.astype(f32),
      c_re.astype(f32).reshape(flat), c_im.astype(f32).reshape(flat))
    lb, a_re, a_im, cf_re, cf_im = outs
    a_re = a_re.reshape(shape)[:, :, 0, :]
    a_im = a_im.reshape(shape)[:, :, 0, :]
    return lb, a_re, a_im, cf_re.reshape(shape), cf_im.reshape(shape)


def _mod_row(b, t):
    return jnp.where(t == 0, MOD_ROWS - 8, b)


def _mods_spec(d):
    return pl.BlockSpec((None, N_MOD, d), lambda b, t: (_mod_row(b, t), 0, 0))


def _full_spec(arr):
    nd = arr.ndim
    return pl.BlockSpec(arr.shape, lambda *_: (0,) * nd, pipeline_mode=pl.Buffered(1))


SUB = 256


def _sub(j):
    return slice(j * SUB, (j + 1) * SUB)


def _flat_specs(nsub, d, nt, lat_only):
    def where(i, j):
        g = i * nsub + j
        if lat_only:
            b, t = g // (nt - 1), g % (nt - 1) + 1
        else:
            b, t = g // nt, g % nt
        return b, t

    h_specs = [pl.BlockSpec((SUB, d), lambda i, j=j: (where(i, j)[0] * nt + where(i, j)[1], 0)) for j in range(nsub)]
    m_specs = [pl.BlockSpec((None, N_MOD, d), lambda i, j=j: (_mod_row(*where(i, j)), 0, 0)) for j in range(nsub)]
    return h_specs, m_specs


def _rows_spec(nsub, d):
    return pl.BlockSpec((nsub * SUB, d), lambda i: (i, 0))


def _tile_spec(d, nt, off=0):
    return pl.BlockSpec((TILE_ROWS[0], d), lambda b, t: (b * nt + t + off, 0))


TILE_ROWS = [256]


def _prenorm(h, mods_ref, gains_ref, which):
    g = gains_ref[2 * which:2 * which + 1, :]
    shift = mods_ref[3 * which:3 * which + 1, :]
    scale = mods_ref[3 * which + 1:3 * which + 2, :]
    return _rms(h, g) * (1 + scale) + shift


def _residual(h, y, mods_ref, gains_ref, which):
    g = gains_ref[2 * which + 1:2 * which + 2, :]
    gate = mods_ref[3 * which + 2:3 * which + 3, :]
    return h + gate * _rms(y, g)


def _first_layer_h(t, hc_ref, hl_ref):
    return jnp.where(t == 0, hc_ref[...], hl_ref[...])


def _proj_kernel(*refs, split, first, conv):
    if first:
        hc_ref, hl_ref, mods_ref, gains_ref, w_ref = refs[:5]
        outs = refs[5:]
        h = _first_layer_h(pl.program_id(1), hc_ref, hl_ref)
    else:
        h_ref, mods_ref, gains_ref, w_ref = refs[:4]
        outs = refs[4:]
        h = h_ref[...]
    d = h.shape[-1]
    a = _prenorm(h, mods_ref, gains_ref, 0).astype(bf16)
    res = _dot(a, w_ref[...])
    if conv:
        bg_ref, u_ref = outs
        bg_ref[...] = res[:, :d].astype(bg_ref.dtype)
        u_ref[...] = (res[:, d:2 * d] * res[:, 2 * d:3 * d]).astype(u_ref.dtype)
    else:
        for k, (o_ref, sc) in enumerate(zip(outs, split)):
            piece = res[:, k * d:(k + 1) * d]
            if sc != 1.0:
                piece = piece * sc
            o_ref[...] = piece.astype(o_ref.dtype)


def _proj(h_in, mods, gains, w, out_dtypes, *, bsz, nt, first=False, conv=False, scales=None):
    d = w.shape[0]
    tile = TILE_ROWS[0]
    rows = bsz * nt * tile
    if first:
        hc, hl = h_in
        in_arrays = [hc, hl]
        in_specs = [pl.BlockSpec((tile, d), lambda b, t: (b, 0)),
                    pl.BlockSpec((tile, d), lambda b, t: (b * (nt - 1) + jnp.maximum(t - 1, 0), 0))]
    else:
        in_arrays = [h_in]
        in_specs = [_tile_spec(d, nt)]
    in_arrays += [mods, gains, w]
    in_specs += [_mods_spec(d), _full_spec(gains), _full_spec(w)]
    scales = scales or (1.0,) * len(out_dtypes)
    return pl.pallas_call(
        functools.partial(_proj_kernel, split=tuple(scales), first=first, conv=conv),
        out_shape=[jax.ShapeDtypeStruct((rows, d), dt) for dt in out_dtypes],
        grid=(bsz, nt),
        in_specs=in_specs,
        out_specs=[_tile_spec(d, nt) for _ in out_dtypes],
        compiler_params=_cparams(("parallel", "arbitrary")),
        name="prenorm_proj",
    )(*in_arrays)


def _proj_flat_kernel(*refs, nsub, scales):
    h_refs, mods_refs = refs[:nsub], refs[nsub:2 * nsub]
    gains_ref, w_ref = refs[2 * nsub:2 * nsub + 2]
    outs, a_s = refs[2 * nsub + 2:-1], refs[-1]
    d = a_s.shape[-1]
    for j in range(nsub):
        a_s[_sub(j), :] = _prenorm(h_refs[j][...], mods_refs[j], gains_ref, 0).astype(bf16)
    for k, (o_ref, sc) in enumerate(zip(outs, scales)):
        piece = _dot(a_s[...], w_ref[:, k * d:(k + 1) * d])
        if sc != 1.0:
            piece = piece * sc
        o_ref[...] = piece.astype(o_ref.dtype)


def _proj_flat(h, mods, gains, w, out_dtypes, *, bsz, nt, nsub, scales=None):
    d = w.shape[0]
    rows = bsz * nt * SUB
    h_specs, m_specs = _flat_specs(nsub, d, nt, False)
    scales = tuple(scales or (1.0,) * len(out_dtypes))
    return pl.pallas_call(
        functools.partial(_proj_flat_kernel, nsub=nsub, scales=scales),
        out_shape=[jax.ShapeDtypeStruct((rows, d), dt) for dt in out_dtypes],
        grid=(rows // (nsub * SUB),),
        in_specs=h_specs + m_specs + [_full_spec(gains), _full_spec(w)],
        out_specs=[_rows_spec(nsub, d) for _ in out_dtypes],
        scratch_shapes=[pltpu.VMEM((nsub * SUB, d), bf16)],
        compiler_params=_cparams(("arbitrary",)),
        name="prenorm_proj",
    )(*([h] * nsub + [mods] * nsub + [gains, w]))


def _mlp_chunks(a, w1_ref, w2_ref, ff_chunk):
    nchunk = w1_ref.shape[1] // ff_chunk

    def hidden(c):
        hid = jnp.maximum(_dot(a, w1_ref[:, c * ff_chunk:(c + 1) * ff_chunk]), 0.0)
        return (hid * hid).astype(bf16)

    acc = None
    nxt = hidden(0)
    for c in range(nchunk):
        cur = nxt
        if c + 1 < nchunk:
            nxt = hidden(c + 1)
        part = _dot(cur, w2_ref[c * ff_chunk:(c + 1) * ff_chunk, :])
        acc = part if acc is None else acc + part
    return acc


def _post_mlp_kernel(*refs, kind, nsub, ff_chunk):
    h_refs, mods_refs, gains_ref = refs[:nsub], refs[nsub:2 * nsub], refs[2 * nsub]
    ins, (w1_ref, w2_ref, out_ref, h1_s, a_s) = refs[2 * nsub + 1:-5], refs[-5:]
    d = out_ref.shape[-1]
    if kind == "hg":
        o_in, gate_ref, gn_ref, wo_ref = ins
        o, gn = o_in[...], gn_ref[...]
        on = jnp.concatenate([_rms(o[:, k * HG_HEAD_DIM:(k + 1) * HG_HEAD_DIM], gn[:, k * HG_HEAD_DIM:(k + 1) * HG_HEAD_DIM])
                              for k in range(d // HG_HEAD_DIM)], axis=-1)
        g = gate_ref[...].astype(f32)
        y = _dot((on * (g * jax.nn.sigmoid(g))).astype(bf16), wo_ref[...])
    elif kind == "s5":
        y_in, wg_ref = ins
        r = _dot(jax.nn.gelu(y_in[...]).astype(bf16), wg_ref[...])
        y = r[:, :d] * jax.nn.sigmoid(r[:, d:])
    else:
        o_in, wo_ref = ins
        y = _dot(o_in[...], wo_ref[...])
    for j in range(nsub):
        h1 = _residual(h_refs[j][...], y[_sub(j)], mods_refs[j], gains_ref, 0)
        h1_s[_sub(j), :] = h1
        a_s[_sub(j), :] = _prenorm(h1, mods_refs[j], gains_ref, 1).astype(bf16)
    acc = _mlp_chunks(a_s[...], w1_ref, w2_ref, ff_chunk)
    for j in range(nsub):
        out_ref[_sub(j), :] = _residual(h1_s[_sub(j), :], acc[_sub(j)], mods_refs[j], gains_ref, 1)


def _post_mlp(kind, h, ins, consts, mods, gains, w1, w2, *, bsz, nt, nsub, lat_only=False):
    d = w1.shape[0]
    rows = bsz * (nt - 1 if lat_only else nt) * SUB
    h_specs, m_specs = _flat_specs(nsub, d, nt, lat_only)
    return pl.pallas_call(
        functools.partial(_post_mlp_kernel, kind=kind, nsub=nsub, ff_chunk=min(1024, w1.shape[1])),
        out_shape=jax.ShapeDtypeStruct((rows, d), f32),
        grid=(rows // (nsub * SUB),),
        in_specs=(h_specs + m_specs + [_full_spec(gains)] + [_rows_spec(nsub, d) for _ in ins]
                  + [_full_spec(c_) for c_ in consts] + [_full_spec(w1), _full_spec(w2)]),
        out_specs=_rows_spec(nsub, d),
        scratch_shapes=[pltpu.VMEM((nsub * SUB, d), f32), pltpu.VMEM((nsub * SUB, d), bf16)],
        compiler_params=_cparams(("arbitrary",)),
        name=kind + "_out_mlp",
    )(*([h] * nsub + [mods] * nsub + [gains] + list(ins) + list(consts) + [w1, w2]))


def _mlp_body(h, mods_ref, gains_ref, w1_ref, w2_ref, ff_chunk):
    a = _prenorm(h, mods_ref, gains_ref, 1).astype(bf16)
    return _residual(h, _mlp_chunks(a, w1_ref, w2_ref, ff_chunk), mods_ref, gains_ref, 1)


def _conv_post_kernel(hc_ref, hl_ref, bg_ref, u_ref, up_ref, un_ref, cw_ref, mods_ref, gains_ref, wo_ref,
                      w1_ref, w2_ref, o_ref, *, nt, ff_chunk):
    t = pl.program_id(1)
    h = _first_layer_h(t, hc_ref, hl_ref)
    u = u_ref[...].astype(f32)
    rows = u.shape[0]
    hal = up_ref.shape[0]
    prev_row = jnp.where(t <= 1, 0.0, up_ref[hal - 1:hal, :].astype(f32))
    next_row = jnp.where((t == 0) | (t == nt - 1), 0.0, un_ref[0:1, :].astype(f32))
    ridx = lax.broadcasted_iota(jnp.int32, u.shape, 0)
    u_prev = jnp.where(ridx == 0, prev_row, pltpu.roll(u, 1, 0))
    u_next = jnp.where(ridx == rows - 1, next_row, pltpu.roll(u, rows - 1, 0))
    cw = cw_ref[...]
    conv = cw[0:1, :] * u_prev + cw[1:2, :] * u + cw[2:3, :] * u_next
    y = _dot((bg_ref[...].astype(f32) * conv).astype(bf16), wo_ref[...])
    h1 = _residual(h, y, mods_ref, gains_ref, 0)
    o_ref[...] = _mlp_body(h1, mods_ref, gains_ref, w1_ref, w2_ref, ff_chunk)


def _conv_post(hc, hl, bg, u, conv_w, mods, gains, wo, w1, w2, *, bsz, nt):
    d = wo.shape[0]
    tile = TILE_ROWS[0]
    hal = 16
    per = tile // hal
    nblk = bsz * nt * per
    return pl.pallas_call(
        functools.partial(_conv_post_kernel, nt=nt, ff_chunk=min(1024, w1.shape[1])),
        out_shape=jax.ShapeDtypeStruct((bsz * nt * tile, d), f32),
        grid=(bsz, nt),
        in_specs=[pl.BlockSpec((tile, d), lambda b, t: (b, 0)),
                  pl.BlockSpec((tile, d), lambda b, t: (b * (nt - 1) + jnp.maximum(t - 1, 0), 0)),
                  _tile_spec(d, nt), _tile_spec(d, nt),
                  pl.BlockSpec((hal, d), lambda b, t: (jnp.maximum((b * nt + t) * per - 1, 0), 0)),
                  pl.BlockSpec((hal, d), lambda b, t: (jnp.minimum((b * nt + t + 1) * per, nblk - 1), 0)),
                  _full_spec(conv_w), _mods_spec(d), _full_spec(gains), _full_spec(wo), _full_spec(w1), _full_spec(w2)],
        out_specs=_tile_spec(d, nt),
        compiler_params=_cparams(("parallel", "arbitrary")),
        name="conv_out_mlp",
    )(hc, hl, bg, u, u, u, conv_w, mods, gains, wo, w1, w2)


def _gla_kernel(*refs, reverse, accumulate):
    if accumulate:
        q_ref, v_ref, f_ref, lb_ref, prev_ref, o_ref, st_ref, qin_s, kin_s, qout_s, kst_s, dec_s, kv_s, sc_s = refs
    else:
        q_ref, v_ref, f_ref, lb_ref, o_ref, st_ref, qin_s, kin_s, qout_s, kst_s, dec_s, kv_s, sc_s = refs
        prev_ref = None
    tile, d = q_ref.shape
    nh = d // HG_HEAD_DIM
    c = HG_CHUNK
    nchunk = tile // c

    @pl.when(pl.program_id(1) == 0)
    def _():
        st_ref[...] = jnp.zeros_like(st_ref)

    ri = lax.broadcasted_iota(jnp.int32, (c, c), 0)
    ci = lax.broadcasted_iota(jnp.int32, (c, c), 1)
    causal = (ci >= ri) if reverse else (ci <= ri)
    tri = jnp.where(causal, 1.0, 0.0).astype(bf16)
    mid = c // 2 if reverse else c // 2 - 1
    last = 0 if reverse else c - 1
    lb = lb_ref[...]

    for k in range(nchunk):
        rows = slice(k * c, (k + 1) * c)
        fg = lb + (1 - lb) * jax.nn.sigmoid(f_ref[rows, :])
        kk = 1 - fg
        hi, md, lo = _split3(jnp.log(fg))
        bcum = _dot(tri, hi) + _dot(tri, md) + _dot(tri, lo)
        b_mid = bcum[mid:mid + 1, :]
        b_last = bcum[last:last + 1, :]
        qq = q_ref[rows, :].astype(f32)
        qin_s[rows, :] = (qq * jnp.exp(bcum - b_mid)).astype(bf16)
        kin_s[rows, :] = (kk * jnp.exp(b_mid - bcum)).astype(bf16)
        qout_s[rows, :] = (qq * jnp.exp(bcum)).astype(bf16)
        kst_s[rows, :] = (kk * jnp.exp(b_last - bcum)).astype(bf16)
        dec_s[k:k + 1, :] = jnp.exp(b_last)

    units = [(k, h) for k in range(nchunk) for h in range(nh)]
    for k, h in units:
        rows = slice(k * c, (k + 1) * c)
        sl = slice(h * HG_HEAD_DIM, (h + 1) * HG_HEAD_DIM)
        sc = _dot_nt(qin_s[rows, sl], kin_s[rows, sl])
        sc_s[k, h] = jnp.where(causal, sc, 0.0).astype(bf16)
    for k, h in units:
        rows = slice(k * c, (k + 1) * c)
        sl = slice(h * HG_HEAD_DIM, (h + 1) * HG_HEAD_DIM)
        o_h = _dot(sc_s[k, h], v_ref[rows, sl])
        if prev_ref is not None:
            o_h = o_h + prev_ref[rows, sl]
        o_ref[rows, sl] = o_h
    for k, h in units:
        rows = slice(k * c, (k + 1) * c)
        sl = slice(h * HG_HEAD_DIM, (h + 1) * HG_HEAD_DIM)
        kv_s[k, h] = _dot_tn(v_ref[rows, sl], kst_s[rows, sl])

    for h in range(nh):
        sl = slice(h * HG_HEAD_DIM, (h + 1) * HG_HEAD_DIM)
        st = st_ref[h]
        for i in range(nchunk):
            k = (nchunk - 1 - i) if reverse else i
            rows = slice(k * c, (k + 1) * c)
            o_ref[rows, sl] += _dot_nt(qout_s[rows, sl], st.astype(bf16))
            st = st * dec_s[k:k + 1, sl] + kv_s[k, h]
        st_ref[h] = st


def _gla(q, v, fraw, lb, prev, *, bsz, nt, reverse):
    d = q.shape[1]
    tile = TILE_ROWS[0]
    nh = d // HG_HEAD_DIM

    def tmap(b, s):
        t = jnp.where(s == 0, 0, nt - s) if reverse else s
        return (b * nt + t, 0)

    spec = pl.BlockSpec((tile, d), tmap)
    arrays = [q, v, fraw, lb]
    specs = [spec, spec, spec, _full_spec(lb)]
    aliases = {}
    if prev is not None:
        arrays.append(prev)
        specs.append(spec)
        aliases = {4: 0}
    return pl.pallas_call(
        functools.partial(_gla_kernel, reverse=reverse, accumulate=prev is not None),
        out_shape=jax.ShapeDtypeStruct(q.shape, f32),
        grid=(bsz, nt),
        in_specs=specs,
        out_specs=spec,
        scratch_shapes=[pltpu.VMEM((nh, HG_HEAD_DIM, HG_HEAD_DIM), f32)]
        + [pltpu.VMEM((tile, d), bf16)] * 4
        + [pltpu.VMEM((tile // HG_CHUNK, d), f32),
           pltpu.VMEM((tile // HG_CHUNK, nh, HG_HEAD_DIM, HG_HEAD_DIM), f32),
           pltpu.VMEM((tile // HG_CHUNK, nh, HG_CHUNK, HG_CHUNK), bf16)],
        input_output_aliases=aliases,
        compiler_params=_cparams(("parallel", "arbitrary")),
        name="hgrn2_gla_bwd" if reverse else "hgrn2_gla_fwd",
    )(*arrays)


def _hg_post_kernel(h_ref, o_ref_in, gate_ref, gn_ref, mods_ref, gains_ref, wo_ref, out_ref):
    o = o_ref_in[...]
    d = o.shape[-1]
    gn = gn_ref[...]
    pieces = []
    for hh in range(d // HG_HEAD_DIM):
        sl = slice(hh * HG_HEAD_DIM, (hh + 1) * HG_HEAD_DIM)
        pieces.append(_rms(o[:, sl], gn[:, sl]))
    on = jnp.concatenate(pieces, axis=-1)
    g = gate_ref[...].astype(f32)
    y = _dot((on * (g * jax.nn.sigmoid(g))).astype(bf16), wo_ref[...])
    out_ref[...] = _residual(h_ref[...], y, mods_ref, gains_ref, 0)


def _hg_post(h, o, gate, gnorm, mods, gains, wo, *, bsz, nt):
    d = wo.shape[0]
    return pl.pallas_call(
        _hg_post_kernel,
        out_shape=jax.ShapeDtypeStruct(h.shape, f32),
        grid=(bsz, nt),
        in_specs=[_tile_spec(d, nt), _tile_spec(d, nt), _tile_spec(d, nt), _full_spec(gnorm),
                  _mods_spec(d), _full_spec(gains), _full_spec(wo)],
        out_specs=_tile_spec(d, nt),
        compiler_params=_cparams(("parallel", "arbitrary")),
        name="hgrn2_readout",
    )(h, o, gate, gnorm, mods, gains, wo)


def _s5_kernel(*refs, reverse, accumulate, nct, bsz):
    if accumulate:
        (h_ref, modsa_ref, gains_ref, wd_ref, wr_ref, are_ref, aim_ref, dsk_ref, prev_ref,
         y_ref, z_ref, st_ref, ytb_ref) = refs
    else:
        h_ref, modsa_ref, gains_ref, wd_ref, wr_ref, are_ref, aim_ref, dsk_ref, y_ref, z_ref, st_ref, ytb_ref = refs
        prev_ref = None
    s = pl.program_id(0)
    _, ts, d = h_ref.shape
    rows = bsz * ts
    nslab = d // S5_SLAB
    lanes = z_ref.shape[2]
    nchunk = z_ref.shape[0] // 2
    per_slab = nchunk // nslab

    @pl.when(s == 0)
    def _():
        st_ref[...] = jnp.zeros_like(st_ref)

    is_ctx = s < nct
    h3 = h_ref[...]
    g = gains_ref[0:1, :]
    crow = slice(MOD_ROWS - 8, MOD_ROWS - 7)
    shift = jnp.where(is_ctx, modsa_ref[crow, 0:1, :], modsa_ref[0:bsz, 0:1, :])
    scale = jnp.where(is_ctx, modsa_ref[crow, 1:2, :], modsa_ref[0:bsz, 1:2, :])
    u3 = h3 * lax.rsqrt(jnp.mean(h3 * h3, axis=-1, keepdims=True) + EPS) * g * (1 + scale) + shift
    u = u3.reshape(rows, d)
    ro = lax.broadcasted_iota(jnp.int32, (rows, rows), 0)
    co = lax.broadcasted_iota(jnp.int32, (rows, rows), 1)
    to_tb = jnp.where((ro // bsz == co % ts) & (ro % bsz == co // ts), 1.0, 0.0).astype(bf16)
    to_bt = jnp.where((co // bsz == ro % ts) & (co % bsz == ro // ts), 1.0, 0.0).astype(bf16)
    ub = _dot(to_tb, u.astype(bf16)).astype(bf16)

    def drive(kb):
        v = _dot(ub[:, kb * S5_SLAB:(kb + 1) * S5_SLAB], wd_ref[kb])
        for cc in range(per_slab):
            z_ref[kb * per_slab + cc] = v[:, cc * lanes:(cc + 1) * lanes]
            z_ref[nchunk + kb * per_slab + cc] = v[:, (per_slab + cc) * lanes:(per_slab + cc + 1) * lanes]

    def scan(ck):
        ar = are_ref[ck]
        ai = aim_ref[ck]
        zr = st_ref[ck]
        zi = st_ref[nchunk + ck]
        for t in (range(ts - 1, -1, -1) if reverse else range(ts)):
            rs = slice(t * bsz, (t + 1) * bsz)
            nzr = ar * zr - ai * zi + z_ref[ck, rs, :]
            nzi = ar * zi + ai * zr + z_ref[nchunk + ck, rs, :]
            zr, zi = nzr, nzi
            z_ref[ck, rs, :] = zr
            z_ref[nchunk + ck, rs, :] = zi
        st_ref[ck] = zr
        st_ref[nchunk + ck] = zi

    def readout(kb):
        acc = None
        for cc in range(per_slab):
            ck = kb * per_slab + cc
            p = (_dot(z_ref[ck].astype(bf16), wr_ref[kb, cc * lanes:(cc + 1) * lanes, :])
                 + _dot(z_ref[nchunk + ck].astype(bf16), wr_ref[kb, (per_slab + cc) * lanes:(per_slab + cc + 1) * lanes, :]))
            acc = p if acc is None else acc + p
        ytb_ref[:, kb * S5_SLAB:(kb + 1) * S5_SLAB] = acc

    drive(0)
    for kb in range(nslab):
        if kb + 1 < nslab:
            drive(kb + 1)
        for cc in range(per_slab):
            scan(kb * per_slab + cc)
        readout(kb)

    ytb = ytb_ref[...]
    hi = ytb.astype(bf16)
    lo = (ytb - hi.astype(f32)).astype(bf16)
    y = _dot(to_bt, hi) + _dot(to_bt, lo)
    if prev_ref is not None:
        y = y + prev_ref[...].reshape(rows, d)
    else:
        y = y + dsk_ref[...] * u
    y_ref[...] = y.reshape(bsz, ts, d)


def _s5_scan(h, modsa, gains, wd, wr, a_re, a_im, dskip, prev, *, bsz, nct, nlt, reverse):
    _, ls, d = h.shape
    ts = S5_TSTEPS
    ntot = nct + nlt
    nchunk2, lanes = a_re.shape[0] * 2, a_re.shape[2]

    def tmap(s):
        if reverse:
            return (0, jnp.where(s < nct, nct - 1 - s, ntot + nct - 1 - s), 0)
        return (0, s, 0)

    spec = pl.BlockSpec((bsz, ts, d), tmap)

    def full(arr):
        nd = arr.ndim
        return pl.BlockSpec(arr.shape, lambda s: (0,) * nd)

    arrays = [h, modsa, gains, wd, wr, a_re, a_im, dskip]
    specs = [spec, full(modsa), full(gains), full(wd), full(wr), full(a_re), full(a_im), full(dskip)]
    aliases = {}
    if prev is not None:
        arrays.append(prev)
        specs.append(spec)
        aliases = {8: 0}
    return pl.pallas_call(
        functools.partial(_s5_kernel, reverse=reverse, accumulate=prev is not None, nct=nct, bsz=bsz),
        out_shape=jax.ShapeDtypeStruct(h.shape, f32),
        grid=(ntot,),
        in_specs=specs,
        out_specs=spec,
        scratch_shapes=[pltpu.VMEM((nchunk2, bsz * ts, lanes), f32), pltpu.VMEM((nchunk2, bsz, lanes), f32),
                        pltpu.VMEM((bsz * ts, d), f32)],
        input_output_aliases=aliases,
        compiler_params=_cparams(("arbitrary",)),
        name="s5_scan_bwd" if reverse else "s5_scan_fwd",
    )(*arrays)


def _s5_post_kernel(h_ref, y_ref, mods_ref, gains_ref, wg_ref, o_ref):
    d = h_ref.shape[-1]
    z = jax.nn.gelu(y_ref[...]).astype(bf16)
    r = _dot(z, wg_ref[...])
    out = r[:, :d] * jax.nn.sigmoid(r[:, d:])
    o_ref[...] = _residual(h_ref[...], out, mods_ref, gains_ref, 0)


def _s5_post(h, y, mods, gains, wg, *, bsz, nt):
    d = wg.shape[0]
    return pl.pallas_call(
        _s5_post_kernel,
        out_shape=jax.ShapeDtypeStruct(h.shape, f32),
        grid=(bsz, nt),
        in_specs=[_tile_spec(d, nt), _tile_spec(d, nt), _mods_spec(d), _full_spec(gains), _full_spec(wg)],
        out_specs=_tile_spec(d, nt),
        compiler_params=_cparams(("parallel", "arbitrary")),
        name="s5_glu",
    )(h, y, mods, gains, wg)


def _s5_weights(b_re, b_im, cf_re, cf_im, a_re, a_im, lanes, bsz):
    g, n, c = b_re.shape
    gps = S5_SLAB // c
    nslab = g // gps
    eye = jnp.eye(gps, dtype=f32)

    def drive(bm):
        t = bm.astype(f32).reshape(nslab, gps, n, c).transpose(0, 1, 3, 2)
        t = t[:, :, :, None, :] * eye[None, :, None, :, None]
        return t.reshape(nslab, gps * c, gps * n)

    wd = jnp.concatenate([drive(b_re), drive(b_im)], axis=-1).astype(bf16)

    def read(cm):
        t = cm.reshape(nslab, gps, c, n).transpose(0, 1, 3, 2)
        t = t[:, :, :, None, :] * eye[None, :, None, :, None]
        return t.reshape(nslab, gps * n, gps * c)

    wr = jnp.stack([jnp.concatenate([read(cf_re[dd]), -read(cf_im[dd])], axis=1) for dd in range(2)]).astype(bf16)

    def decay(a):
        t = a.reshape(2, (g * n) // lanes, 1, lanes)
        return jnp.broadcast_to(t, (2, (g * n) // lanes, bsz, lanes))

    return wd, wr, decay(a_re), decay(a_im)


def _na_bias_table(rpb):
    nh = rpb.shape[0]
    qc = jnp.arange(GRID_W)[:, None]
    kc = jnp.arange(GRID_W)[None, :]
    dc = jnp.clip(kc - qc, 1 - NA_COLS, NA_COLS - 1) + (NA_COLS - 1)
    q_start = jnp.clip(qc - NA_COLS // 2, 0, GRID_W - NA_COLS)
    in_win = (kc >= q_start) & (kc < q_start + NA_COLS)
    tab = rpb.astype(f32)[:, :, dc]
    tab = jnp.where(in_win[None, None], tab, NEG_BIG)
    offs = jnp.arange(NA_ROWS)[:, None] + jnp.arange(NA_ROWS)[None, :]
    t2 = tab[:, offs]
    t2 = t2.transpose(1, 0, 3, 2, 4).reshape(NA_ROWS, nh, GRID_W, NA_ROWS * GRID_W)
    return t2


def _na_kernel(*refs, lc, nrows):
    q_ref, k_ref, v_ref = refs[:3]
    bias_refs = refs[3:3 + nrows]
    o_ref, s_scr, p_scr, l_scr = refs[3 + nrows:]
    r0 = pl.program_id(1) * nrows
    rows_total = (k_ref.shape[0] - lc) // GRID_W
    nloc = NA_ROWS * GRID_W
    d = q_ref.shape[-1]
    dh = d // NA_HEADS
    nq = nrows * GRID_W
    lane = lax.broadcasted_iota(jnp.int32, (nq, 2 * dh), 1)
    k0 = [pl.multiple_of(lc + jnp.clip(r0 + j - NA_ROWS // 2, 0, rows_total - NA_ROWS) * GRID_W, GRID_W)
          for j in range(nrows)]
    for p in range(NA_HEADS // 2):
        sl = slice(p * 2 * dh, (p + 1) * 2 * dh)
        qp = q_ref[:, sl]
        kc = k_ref[0:lc, sl]
        for sub in range(2):
            h = 2 * p + sub
            qm = jnp.where((lane >= dh) if sub else (lane < dh), qp, jnp.zeros_like(qp))
            s_scr[h, :, nloc:nloc + lc] = _dot_nt(qm, kc)
            for j in range(nrows):
                rq = slice(j * GRID_W, (j + 1) * GRID_W)
                s_scr[h, rq, 0:nloc] = _dot_nt(qm[rq], k_ref[pl.ds(k0[j], nloc), sl]) + bias_refs[j][h]
    for h in range(NA_HEADS):
        s = s_scr[h]
        m = jnp.max(s, axis=-1, keepdims=True)
        e = jnp.exp(s - m)
        l_scr[h] = jnp.broadcast_to(jnp.sum(e, axis=-1, keepdims=True), (nq, 2 * dh))
        p_scr[h] = e.astype(bf16)
    for p in range(NA_HEADS // 2):
        sl = slice(p * 2 * dh, (p + 1) * 2 * dh)
        vc = v_ref[0:lc, sl]
        halves = []
        for sub in range(2):
            h = 2 * p + sub
            o_ctx = _dot(p_scr[h, :, nloc:nloc + lc], vc)
            o_loc = jnp.concatenate(
                [_dot(p_scr[h, j * GRID_W:(j + 1) * GRID_W, 0:nloc], v_ref[pl.ds(k0[j], nloc), sl]) for j in range(nrows)],
                axis=0)
            halves.append((o_loc + o_ctx) / l_scr[h])
        o_ref[:, sl] = jnp.where(lane >= dh, halves[1], halves[0]).astype(o_ref.dtype)


def _na_core(q, k, v, bias, *, bsz, ls, lc, nrows):
    d = q.shape[1]
    nrow = (ls - lc) // GRID_W
    nq = nrows * GRID_W
    nkeys = NA_ROWS * GRID_W + lc

    def bias_spec(j):
        def bias_map(b, rp):
            r = rp * nrows + j
            return (jnp.clip(r - NA_ROWS // 2, 0, nrow - NA_ROWS) - r + NA_ROWS - 1, 0, 0, 0)
        return pl.BlockSpec((None,) + bias.shape[1:], bias_map, pipeline_mode=pl.Buffered(1))

    kv_spec = pl.BlockSpec((ls, d), lambda b, rp: (b, 0))
    return pl.pallas_call(
        functools.partial(_na_kernel, lc=lc, nrows=nrows),
        out_shape=jax.ShapeDtypeStruct((bsz * nrow * GRID_W, d), bf16),
        grid=(bsz, nrow // nrows),
        in_specs=[pl.BlockSpec((nq, d), lambda b, rp: (b * (ls // nq) + lc // nq + rp, 0)), kv_spec, kv_spec]
        + [bias_spec(j) for j in range(nrows)],
        out_specs=pl.BlockSpec((nq, d), lambda b, rp: (b * (nrow // nrows) + rp, 0)),
        scratch_shapes=[pltpu.VMEM((NA_HEADS, nq, nkeys), f32),
                        pltpu.VMEM((NA_HEADS, nq, nkeys), bf16),
                        pltpu.VMEM((NA_HEADS, nq, 2 * (d // NA_HEADS)), f32)],
        compiler_params=_cparams(("parallel", "arbitrary")),
        name="na_core",
    )(q, k, v, *([bias] * nrows))


def _na_post_kernel(h_ref, o_ref_in, mods_ref, gains_ref, wo_ref, out_ref):
    y = _dot(o_ref_in[...], wo_ref[...])
    out_ref[...] = _residual(h_ref[...], y, mods_ref, gains_ref, 0)


def _na_post(h, o, mods, gains, wo, *, bsz, nt):
    d = wo.shape[0]
    tile = TILE_ROWS[0]
    lat_spec = pl.BlockSpec((tile, d), lambda b, t: (b * (nt - 1) + t, 0))
    return pl.pallas_call(
        _na_post_kernel,
        out_shape=jax.ShapeDtypeStruct((bsz * (nt - 1) * tile, d), f32),
        grid=(bsz, nt - 1),
        in_specs=[pl.BlockSpec((tile, d), lambda b, t: (b * nt + t + 1, 0)), lat_spec,
                  pl.BlockSpec((None, N_MOD, d), lambda b, t: (b, 0, 0)), _full_spec(gains), _full_spec(wo)],
        out_specs=lat_spec,
        compiler_params=_cparams(("parallel", "arbitrary")),
        name="na_out",
    )(h, o, mods, gains, wo)


def _mlp_lat(h_lat, mods, gains, w1, w2, *, bsz, ntl):
    d = w1.shape[0]
    tile = TILE_ROWS[0]
    spec = pl.BlockSpec((tile, d), lambda b, t: (b * ntl + t, 0))
    return pl.pallas_call(
        functools.partial(_mlp_kernel, ff_chunk=min(1024, w1.shape[1])),
        out_shape=jax.ShapeDtypeStruct(h_lat.shape, f32),
        grid=(bsz, ntl),
        in_specs=[spec, pl.BlockSpec((None, N_MOD, d), lambda b, t: (b, 0, 0)),
                  _full_spec(gains), _full_spec(w1), _full_spec(w2)],
        out_specs=spec,
        compiler_params=_cparams(("parallel", "arbitrary")),
        name="relu2_mlp_lat",
    )(h_lat, mods, gains, w1, w2)


def kernel(x, c, ctx, c_ctx, ada_w, ada_b, norm_gains, mlp_w_in, mlp_w_out, sc_w_in, sc_conv, sc_w_out, hg_w_in, hg_lower_bound, hg_norm, hg_w_out, s5_lam_re, s5_lam_im, s5_log_dt, s5_b_re, s5_b_im, s5_c_re, s5_c_im, s5_d, s5_w_glu, na_w_qkv, na_rpb, na_w_out):
    bsz, seq, d = x.shape
    lc = ctx.shape[1]
    depth = ada_w.shape[0]
    tile = TILE_ROWS[0]
    assert depth == 4 and lc == tile and seq % tile == 0 and seq % GRID_W == 0 and bsz <= MOD_ROWS - 8
    assert bsz % 8 == 0 and tile % S5_TSTEPS == 0
    ls = lc + seq
    nt = ls // tile
    nsub = 2
    assert (bsz * nt) % nsub == 0 and (nt - 1) % nsub == 0 and SUB == tile

    cc = jnp.zeros((MOD_ROWS, d), f32).at[:bsz].set(c.astype(f32)).at[MOD_ROWS - 8].set(c_ctx.astype(f32))
    mods_b = _mods(cc, ada_w, ada_b)
    mods_a = mods_b.transpose(0, 2, 1, 3)
    lb_all, a_re, a_im, cf_re, cf_im = _prep(hg_lower_bound, s5_lam_re[0], s5_lam_im[0], s5_log_dt[0],
                                             s5_c_re[0], s5_c_im[0])
    gains = norm_gains.astype(f32)
    w1 = mlp_w_in.astype(bf16)
    w2 = mlp_w_out.astype(bf16)
    xl = x.reshape(bsz * seq, d)
    xc = ctx.reshape(bsz * lc, d)

    bg, u = _proj((xc, xl), mods_a[0], gains[0], sc_w_in[0].astype(bf16), (bf16, bf16),
                  bsz=bsz, nt=nt, first=True, conv=True)
    h = _conv_post(xc, xl, bg, u, sc_conv[0].astype(f32), mods_a[0], gains[0], sc_w_out[0].astype(bf16),
                   w1[0], w2[0], bsz=bsz, nt=nt)

    q, v, gate, f_fwd, f_bwd = _proj_flat(h, mods_a[1], gains[1], hg_w_in[0].astype(bf16),
                                          (bf16, bf16, bf16, f32, f32), bsz=bsz, nt=nt, nsub=nsub)
    lb1 = lb_all[1:2]
    o = _gla(q, v, f_fwd, lb1, None, bsz=bsz, nt=nt, reverse=False)
    o = _gla(q, v, f_bwd, lb1, o, bsz=bsz, nt=nt, reverse=True)
    h = _post_mlp("hg", h, [o, gate], [hg_norm[0].astype(f32).reshape(1, d), hg_w_out[0].astype(bf16)],
                  mods_a[1], gains[1], w1[1], w2[1], bsz=bsz, nt=nt, nsub=nsub)

    lanes = 256
    wd, wr, ar, ai = _s5_weights(s5_b_re[0], s5_b_im[0], cf_re, cf_im, a_re, a_im, lanes, bsz)
    h3 = h.reshape(bsz, ls, d)
    nct, nlt = lc // S5_TSTEPS, seq // S5_TSTEPS
    dsk = s5_d[0].astype(f32).reshape(1, d)
    y = _s5_scan(h3, mods_a[2], gains[2], wd, wr[0], ar[0], ai[0], dsk, None, bsz=bsz, nct=nct, nlt=nlt, reverse=False)
    y = _s5_scan(h3, mods_a[2], gains[2], wd, wr[1], ar[1], ai[1], dsk, y, bsz=bsz, nct=nct, nlt=nlt, reverse=True)
    h = _post_mlp("s5", h, [y.reshape(bsz * ls, d)], [s5_w_glu[0].astype(bf16)],
                  mods_a[2], gains[2], w1[2], w2[2], bsz=bsz, nt=nt, nsub=nsub)

    dh = d // NA_HEADS
    qq, kk, vv = _proj_flat(h, mods_a[3], gains[3], na_w_qkv[0].astype(bf16), (bf16, bf16, bf16), bsz=bsz, nt=nt,
                            nsub=nsub, scales=(dh ** -0.5, 1.0, 1.0))
    bias = _na_bias_table(na_rpb[0])
    o = _na_core(qq, kk, vv, bias, bsz=bsz, ls=ls, lc=lc, nrows=2)
    hl = _post_mlp("na", h, [o], [na_w_out[0].astype(bf16)], mods_a[3], gains[3], w1[3], w2[3],
                   bsz=bsz, nt=nt, nsub=nsub, lat_only=True)
    return hl.reshape(bsz, seq, d)
```

```python
import functools
import math

import jax
import jax.numpy as jnp
from jax import lax
from jax.experimental import pallas as pl
from jax.experimental.pallas import tpu as pltpu

EPS = 1e-6
N_MOD = 6
MOD_ROWS = 24
HG_HEAD_DIM = 128
HG_CHUNK = 32
S5_GROUP = 16
S5_SLAB = 256
S5_TSTEPS = 16
GRID_W = 64
NA_HEADS = 16
NA_ROWS = 8
NA_COLS = 16
NEG_BIG = -1e30
VMEM_LIMIT = 56 * 1024 * 1024

bf16 = jnp.bfloat16
f32 = jnp.float32


def _cparams(sem):
    return pltpu.CompilerParams(dimension_semantics=sem, vmem_limit_bytes=VMEM_LIMIT)


def _rms(x, g):
    return x * lax.rsqrt(jnp.mean(x * x, axis=-1, keepdims=True) + EPS) * g


def _dot(a, b):
    return jnp.dot(a, b, preferred_element_type=f32)


def _dot_nt(a, b):
    return lax.dot_general(a, b, (((1,), (1,)), ((), ())), preferred_element_type=f32)


def _dot_tn(a, b):
    return lax.dot_general(a, b, (((0,), (0,)), ((), ())), preferred_element_type=f32)


def _split3(x):
    hi = x.astype(bf16)
    r = x - hi.astype(f32)
    mid = r.astype(bf16)
    lo = (r - mid.astype(f32)).astype(bf16)
    return hi, mid, lo


def _mods_kernel(cc_ref, w_ref, b_ref, o_ref):
    x = cc_ref[...]
    a = (x * jax.nn.sigmoid(x)).astype(bf16)
    o_ref[...] = _dot(a, w_ref[...].astype(bf16)) + b_ref[...]


def _mods(cc, ada_w, ada_b):
    depth, d, _ = ada_w.shape
    out = pl.pallas_call(
        _mods_kernel,
        out_shape=jax.ShapeDtypeStruct((depth, N_MOD, MOD_ROWS, d), f32),
        grid=(depth, N_MOD),
        in_specs=[pl.BlockSpec((MOD_ROWS, d), lambda i, j: (0, 0)),
                  pl.BlockSpec((None, d, d), lambda i, j: (i, 0, j)),
                  pl.BlockSpec((None, 1, d), lambda i, j: (i * N_MOD + j, 0, 0))],
        out_specs=pl.BlockSpec((None, None, MOD_ROWS, d), lambda i, j: (i, j, 0, 0)),
        compiler_params=_cparams(("arbitrary", "arbitrary")),
        name="ada_mods",
    )(cc, ada_w, ada_b.reshape(depth * N_MOD, 1, d))
    return out


def _prep_kernel(lbp_ref, lre_ref, lim_ref, ldt_ref, cre_ref, cim_ref,
                 lb_ref, are_ref, aim_ref, cfre_ref, cfim_ref):
    x = lbp_ref[...]
    rows = [x[i:i + 1, :] for i in range(x.shape[0])]
    m = functools.reduce(jnp.maximum, rows)
    es = [jnp.exp(r - m) for r in rows]
    tot = functools.reduce(lambda a, b: a + b, es)
    acc = None
    first = None
    for i, e in enumerate(es):
        sm = e / tot
        acc = sm if acc is None else acc + sm
        if first is None:
            first = acc
        lb_ref[i:i + 1, :] = acc - first
    lam_re = jnp.minimum(lre_ref[...], -1e-4)
    lam_im = lim_ref[...]
    dt = jnp.exp(ldt_ref[...])
    mag = jnp.exp(lam_re * dt)
    a_re = mag * jnp.cos(lam_im * dt)
    a_im = mag * jnp.sin(lam_im * dt)
    den = lam_re * lam_re + lam_im * lam_im
    f_re = ((a_re - 1) * lam_re + a_im * lam_im) / den
    f_im = (a_im * lam_re - (a_re - 1) * lam_im) / den
    are_ref[...] = a_re
    aim_ref[...] = a_im
    c_re, c_im = cre_ref[...], cim_ref[...]
    cfre_ref[...] = c_re * f_re - c_im * f_im
    cfim_ref[...] = c_re * f_im + c_im * f_re


def _prep(hg_lower_bound, lam_re, lam_im, log_dt, c_re, c_im):
    shape = c_re.shape
    flat = (shape[0] * shape[1] * shape[2], shape[3])

    def expand(t):
        return jnp.broadcast_to(t[:, :, None, :], shape).reshape(flat)

    ldt = jnp.broadcast_to(log_dt[:, :, None, None], shape).reshape(flat)
    outs = pl.pallas_call(
        _prep_kernel,
        out_shape=[jax.ShapeDtypeStruct(hg_lower_bound.shape, f32)] + [jax.ShapeDtypeStruct(flat, f32)] * 4,
        name="param_prep",
    )(hg_lower_bound.astype(f32), expand(lam_re.astype(f32)), expand(lam_im.astype(f32)), ldt.astype(f32),
      c_re.astype(f32).reshape(flat), c_im.astype(f32).reshape(flat))
    lb, a_re, a_im, cf_re, cf_im = outs
    a_re = a_re.reshape(shape)[:, :, 0, :]
    a_im = a_im.reshape(shape)[:, :, 0, :]
    return lb, a_re, a_im, cf_re.reshape(shape), cf_im.reshape(shape)


def _mod_row(b, t):
    return jnp.where(t == 0, MOD_ROWS - 8, b)


def _mods_spec(d):
    return pl.BlockSpec((None, N_MOD, d), lambda b, t: (_mod_row(b, t), 0, 0))


def _full_spec(arr):
    nd = arr.ndim
    return pl.BlockSpec(arr.shape, lambda *_: (0,) * nd, pipeline_mode=pl.Buffered(1))


SUB = 256


def _sub(j):
    return slice(j * SUB, (j + 1) * SUB)


def _flat_specs(nsub, d, nt, lat_only):
    def where(i, j):
        g = i * nsub + j
        if lat_only:
            b, t = g // (nt - 1), g % (nt - 1) + 1
        else:
            b, t = g // nt, g % nt
        return b, t

    h_specs = [pl.BlockSpec((SUB, d), lambda i, j=j: (where(i, j)[0] * nt + where(i, j)[1], 0)) for j in range(nsub)]
    m_specs = [pl.BlockSpec((None, N_MOD, d), lambda i, j=j: (_mod_row(*where(i, j)), 0, 0)) for j in range(nsub)]
    return h_specs, m_specs


def _rows_spec(nsub, d):
    return pl.BlockSpec((nsub * SUB, d), lambda i: (i, 0))


def _tile_spec(d, nt, off=0):
    return pl.BlockSpec((TILE_ROWS[0], d), lambda b, t: (b * nt + t + off, 0))


TILE_ROWS = [256]


def _prenorm(h, mods_ref, gains_ref, which):
    g = gains_ref[2 * which:2 * which + 1, :]
    shift = mods_ref[3 * which:3 * which + 1, :]
    scale = mods_ref[3 * which + 1:3 * which + 2, :]
    return _rms(h, g) * (1 + scale) + shift


def _residual(h, y, mods_ref, gains_ref, which):
    g = gains_ref[2 * which + 1:2 * which + 2, :]
    gate = mods_ref[3 * which + 2:3 * which + 3, :]
    return h + gate * _rms(y, g)


def _first_layer_h(t, hc_ref, hl_ref):
    return jnp.where(t == 0, hc_ref[...], hl_ref[...])


def _proj_kernel(*refs, split, first, conv):
    if first:
        hc_ref, hl_ref, mods_ref, gains_ref, w_ref = refs[:5]
        outs = refs[5:]
        h = _first_layer_h(pl.program_id(1), hc_ref, hl_ref)
    else:
        h_ref, mods_ref, gains_ref, w_ref = refs[:4]
        outs = refs[4:]
        h = h_ref[...]
    d = h.shape[-1]
    a = _prenorm(h, mods_ref, gains_ref, 0).astype(bf16)
    res = _dot(a, w_ref[...])
    if conv:
        bg_ref, u_ref = outs
        bg_ref[...] = res[:, :d].astype(bg_ref.dtype)
        u_ref[...] = (res[:, d:2 * d] * res[:, 2 * d:3 * d]).astype(u_ref.dtype)
    else:
        for k, (o_ref, sc) in enumerate(zip(outs, split)):
            piece = res[:, k * d:(k + 1) * d]
            if sc != 1.0:
                piece = piece * sc
            o_ref[...] = piece.astype(o_ref.dtype)


def _proj(h_in, mods, gains, w, out_dtypes, *, bsz, nt, first=False, conv=False, scales=None):
    d = w.shape[0]
    tile = TILE_ROWS[0]
    rows = bsz * nt * tile
    if first:
        hc, hl = h_in
        in_arrays = [hc, hl]
        in_specs = [pl.BlockSpec((tile, d), lambda b, t: (b, 0)),
                    pl.BlockSpec((tile, d), lambda b, t: (b * (nt - 1) + jnp.maximum(t - 1, 0), 0))]
    else:
        in_arrays = [h_in]
        in_specs = [_tile_spec(d, nt)]
    in_arrays += [mods, gains, w]
    in_specs += [_mods_spec(d), _full_spec(gains), _full_spec(w)]
    scales = scales or (1.0,) * len(out_dtypes)
    return pl.pallas_call(
        functools.partial(_proj_kernel, split=tuple(scales), first=first, conv=conv),
        out_shape=[jax.ShapeDtypeStruct((rows, d), dt) for dt in out_dtypes],
        grid=(bsz, nt),
        in_specs=in_specs,
        out_specs=[_tile_spec(d, nt) for _ in out_dtypes],
        compiler_params=_cparams(("parallel", "arbitrary")),
        name="prenorm_proj",
    )(*in_arrays)


def _proj_flat_kernel(*refs, nsub, scales):
    h_refs, mods_refs = refs[:nsub], refs[nsub:2 * nsub]
    gains_ref, w_ref = refs[2 * nsub:2 * nsub + 2]
    outs, a_s = refs[2 * nsub + 2:-1], refs[-1]
    d = a_s.shape[-1]
    for j in range(nsub):
        a_s[_sub(j), :] = _prenorm(h_refs[j][...], mods_refs[j], gains_ref, 0).astype(bf16)
    for k, (o_ref, sc) in enumerate(zip(outs, scales)):
        piece = _dot(a_s[...], w_ref[:, k * d:(k + 1) * d])
        if sc != 1.0:
            piece = piece * sc
        o_ref[...] = piece.astype(o_ref.dtype)


def _proj_flat(h, mods, gains, w, out_dtypes, *, bsz, nt, nsub, scales=None):
    d = w.shape[0]
    rows = bsz * nt * SUB
    h_specs, m_specs = _flat_specs(nsub, d, nt, False)
    scales = tuple(scales or (1.0,) * len(out_dtypes))
    return pl.pallas_call(
        functools.partial(_proj_flat_kernel, nsub=nsub, scales=scales),
        out_shape=[jax.ShapeDtypeStruct((rows, d), dt) for dt in out_dtypes],
        grid=(rows // (nsub * SUB),),
        in_specs=h_specs + m_specs + [_full_spec(gains), _full_spec(w)],
        out_specs=[_rows_spec(nsub, d) for _ in out_dtypes],
        scratch_shapes=[pltpu.VMEM((nsub * SUB, d), bf16)],
        compiler_params=_cparams(("arbitrary",)),
        name="prenorm_proj",
    )(*([h] * nsub + [mods] * nsub + [gains, w]))


def _mlp_chunks(a, w1_ref, w2_ref, ff_chunk):
    nchunk = w1_ref.shape[1] // ff_chunk

    def hidden(c):
        hid = jnp.maximum(_dot(a, w1_ref[:, c * ff_chunk:(c + 1) * ff_chunk]), 0.0)
        return (hid * hid).astype(bf16)

    acc = None
    nxt = hidden(0)
    for c in range(nchunk):
        cur = nxt
        if c + 1 < nchunk:
            nxt = hidden(c + 1)
        part = _dot(cur, w2_ref[c * ff_chunk:(c + 1) * ff_chunk, :])
        acc = part if acc is None else acc + part
    return acc


def _post_mlp_kernel(*refs, kind, nsub, ff_chunk):
    h_refs, mods_refs, gains_ref = refs[:nsub], refs[nsub:2 * nsub], refs[2 * nsub]
    ins, (w1_ref, w2_ref, out_ref, h1_s, a_s) = refs[2 * nsub + 1:-5], refs[-5:]
    d = out_ref.shape[-1]
    if kind == "hg":
        o_in, gate_ref, gn_ref, wo_ref = ins
        o, gn = o_in[...], gn_ref[...]
        on = jnp.concatenate([_rms(o[:, k * HG_HEAD_DIM:(k + 1) * HG_HEAD_DIM], gn[:, k * HG_HEAD_DIM:(k + 1) * HG_HEAD_DIM])
                              for k in range(d // HG_HEAD_DIM)], axis=-1)
        g = gate_ref[...].astype(f32)
        y = _dot((on * (g * jax.nn.sigmoid(g))).astype(bf16), wo_ref[...])
    elif kind == "s5":
        y_in, wg_ref = ins
        r = _dot(jax.nn.gelu(y_in[...]).astype(bf16), wg_ref[...])
        y = r[:, :d] * jax.nn.sigmoid(r[:, d:])
    else:
        o_in, wo_ref = ins
        y = _dot(o_in[...], wo_ref[...])
    for j in range(nsub):
        h1 = _residual(h_refs[j][...], y[_sub(j)], mods_refs[j], gains_ref, 0)
        h1_s[_sub(j), :] = h1
        a_s[_sub(j), :] = _prenorm(h1, mods_refs[j], gains_ref, 1).astype(bf16)
    acc = _mlp_chunks(a_s[...], w1_ref, w2_ref, ff_chunk)
    for j in range(nsub):
        out_ref[_sub(j), :] = _residual(h1_s[_sub(j), :], acc[_sub(j)], mods_refs[j], gains_ref, 1)


def _post_mlp(kind, h, ins, consts, mods, gains, w1, w2, *, bsz, nt, nsub, lat_only=False):
    d = w1.shape[0]
    rows = bsz * (nt - 1 if lat_only else nt) * SUB
    h_specs, m_specs = _flat_specs(nsub, d, nt, lat_only)
    return pl.pallas_call(
        functools.partial(_post_mlp_kernel, kind=kind, nsub=nsub, ff_chunk=min(1024, w1.shape[1])),
        out_shape=jax.ShapeDtypeStruct((rows, d), f32),
        grid=(rows // (nsub * SUB),),
        in_specs=(h_specs + m_specs + [_full_spec(gains)] + [_rows_spec(nsub, d) for _ in ins]
                  + [_full_spec(c_) for c_ in consts] + [_full_spec(w1), _full_spec(w2)]),
        out_specs=_rows_spec(nsub, d),
        scratch_shapes=[pltpu.VMEM((nsub * SUB, d), f32), pltpu.VMEM((nsub * SUB, d), bf16)],
        compiler_params=_cparams(("arbitrary",)),
        name=kind + "_out_mlp",
    )(*([h] * nsub + [mods] * nsub + [gains] + list(ins) + list(consts) + [w1, w2]))


def _mlp_body(h, mods_ref, gains_ref, w1_ref, w2_ref, ff_chunk):
    a = _prenorm(h, mods_ref, gains_ref, 1).astype(bf16)
    return _residual(h, _mlp_chunks(a, w1_ref, w2_ref, ff_chunk), mods_ref, gains_ref, 1)


def _conv_post_kernel(hc_ref, hl_ref, bg_ref, u_ref, up_ref, un_ref, cw_ref, mods_ref, gains_ref, wo_ref,
                      w1_ref, w2_ref, o_ref, *, nt, ff_chunk):
    t = pl.program_id(1)
    h = _first_layer_h(t, hc_ref, hl_ref)
    u = u_ref[...].astype(f32)
    rows = u.shape[0]
    hal = up_ref.shape[0]
    prev_row = jnp.where(t <= 1, 0.0, up_ref[hal - 1:hal, :].astype(f32))
    next_row = jnp.where((t == 0) | (t == nt - 1), 0.0, un_ref[0:1, :].astype(f32))
    ridx = lax.broadcasted_iota(jnp.int32, u.shape, 0)
    u_prev = jnp.where(ridx == 0, prev_row, pltpu.roll(u, 1, 0))
    u_next = jnp.where(ridx == rows - 1, next_row, pltpu.roll(u, rows - 1, 0))
    cw = cw_ref[...]
    conv = cw[0:1, :] * u_prev + cw[1:2, :] * u + cw[2:3, :] * u_next
    y = _dot((bg_ref[...].astype(f32) * conv).astype(bf16), wo_ref[...])
    h1 = _residual(h, y, mods_ref, gains_ref, 0)
    o_ref[...] = _mlp_body(h1, mods_ref, gains_ref, w1_ref, w2_ref, ff_chunk)


def _conv_post(hc, hl, bg, u, conv_w, mods, gains, wo, w1, w2, *, bsz, nt):
    d = wo.shape[0]
    tile = TILE_ROWS[0]
    hal = 16
    per = tile // hal
    nblk = bsz * nt * per
    return pl.pallas_call(
        functools.partial(_conv_post_kernel, nt=nt, ff_chunk=min(1024, w1.shape[1])),
        out_shape=jax.ShapeDtypeStruct((bsz * nt * tile, d), f32),
        grid=(bsz, nt),
        in_specs=[pl.BlockSpec((tile, d), lambda b, t: (b, 0)),
                  pl.BlockSpec((tile, d), lambda b, t: (b * (nt - 1) + jnp.maximum(t - 1, 0), 0)),
                  _tile_spec(d, nt), _tile_spec(d, nt),
                  pl.BlockSpec((hal, d), lambda b, t: (jnp.maximum((b * nt + t) * per - 1, 0), 0)),
                  pl.BlockSpec((hal, d), lambda b, t: (jnp.minimum((b * nt + t + 1) * per, nblk - 1), 0)),
                  _full_spec(conv_w), _mods_spec(d), _full_spec(gains), _full_spec(wo), _full_spec(w1), _full_spec(w2)],
        out_specs=_tile_spec(d, nt),
        compiler_params=_cparams(("parallel", "arbitrary")),
        name="conv_out_mlp",
    )(hc, hl, bg, u, u, u, conv_w, mods, gains, wo, w1, w2)


def _gla_kernel(*refs, reverse, accumulate):
    if accumulate:
        q_ref, v_ref, f_ref, lb_ref, prev_ref, o_ref, st_ref, qin_s, kin_s, qout_s, kst_s, dec_s, kv_s, sc_s = refs
    else:
        q_ref, v_ref, f_ref, lb_ref, o_ref, st_ref, qin_s, kin_s, qout_s, kst_s, dec_s, kv_s, sc_s = refs
        prev_ref = None
    tile, d = q_ref.shape
    nh = d // HG_HEAD_DIM
    c = HG_CHUNK
    nchunk = tile // c

    @pl.when(pl.program_id(1) == 0)
    def _():
        st_ref[...] = jnp.zeros_like(st_ref)

    ri = lax.broadcasted_iota(jnp.int32, (c, c), 0)
    ci = lax.broadcasted_iota(jnp.int32, (c, c), 1)
    causal = (ci >= ri) if reverse else (ci <= ri)
    tri = jnp.where(causal, 1.0, 0.0).astype(bf16)
    mid = c // 2 if reverse else c // 2 - 1
    last = 0 if reverse else c - 1
    lb = lb_ref[...]

    for k in range(nchunk):
        rows = slice(k * c, (k + 1) * c)
        fg = lb + (1 - lb) * jax.nn.sigmoid(f_ref[rows, :])
        kk = 1 - fg
        hi, md, lo = _split3(jnp.log(fg))
        bcum = _dot(tri, hi) + _dot(tri, md) + _dot(tri, lo)
        b_mid = bcum[mid:mid + 1, :]
        b_last = bcum[last:last + 1, :]
        qq = q_ref[rows, :].astype(f32)
        qin_s[rows, :] = (qq * jnp.exp(bcum - b_mid)).astype(bf16)
        kin_s[rows, :] = (kk * jnp.exp(b_mid - bcum)).astype(bf16)
        qout_s[rows, :] = (qq * jnp.exp(bcum)).astype(bf16)
        kst_s[rows, :] = (kk * jnp.exp(b_last - bcum)).astype(bf16)
        dec_s[k:k + 1, :] = jnp.exp(b_last)

    units = [(k, h) for k in range(nchunk) for h in range(nh)]
    for k, h in units:
        rows = slice(k * c, (k + 1) * c)
        sl = slice(h * HG_HEAD_DIM, (h + 1) * HG_HEAD_DIM)
        sc = _dot_nt(qin_s[rows, sl], kin_s[rows, sl])
        sc_s[k, h] = jnp.where(causal, sc, 0.0).astype(bf16)
    for k, h in units:
        rows = slice(k * c, (k + 1) * c)
        sl = slice(h * HG_HEAD_DIM, (h + 1) * HG_HEAD_DIM)
        o_h = _dot(sc_s[k, h], v_ref[rows, sl])
        if prev_ref is not None:
            o_h = o_h + prev_ref[rows, sl]
        o_ref[rows, sl] = o_h
    for k, h in units:
        rows = slice(k * c, (k + 1) * c)
        sl = slice(h * HG_HEAD_DIM, (h + 1) * HG_HEAD_DIM)
        kv_s[k, h] = _dot_tn(v_ref[rows, sl], kst_s[rows, sl])

    for h in range(nh):
        sl = slice(h * HG_HEAD_DIM, (h + 1) * HG_HEAD_DIM)
        st = st_ref[h]
        for i in range(nchunk):
            k = (nchunk - 1 - i) if reverse else i
            rows = slice(k * c, (k + 1) * c)
            o_ref[rows, sl] += _dot_nt(qout_s[rows, sl], st.astype(bf16))
            st = st * dec_s[k:k + 1, sl] + kv_s[k, h]
        st_ref[h] = st


def _gla(q, v, fraw, lb, prev, *, bsz, nt, reverse):
    d = q.shape[1]
    tile = TILE_ROWS[0]
    nh = d // HG_HEAD_DIM

    def tmap(b, s):
        t = jnp.where(s == 0, 0, nt - s) if reverse else s
        return (b * nt + t, 0)

    spec = pl.BlockSpec((tile, d), tmap)
    arrays = [q, v, fraw, lb]
    specs = [spec, spec, spec, _full_spec(lb)]
    aliases = {}
    if prev is not None:
        arrays.append(prev)
        specs.append(spec)
        aliases = {4: 0}
    return pl.pallas_call(
        functools.partial(_gla_kernel, reverse=reverse, accumulate=prev is not None),
        out_shape=jax.ShapeDtypeStruct(q.shape, f32),
        grid=(bsz, nt),
        in_specs=specs,
        out_specs=spec,
        scratch_shapes=[pltpu.VMEM((nh, HG_HEAD_DIM, HG_HEAD_DIM), f32)]
        + [pltpu.VMEM((tile, d), bf16)] * 4
        + [pltpu.VMEM((tile // HG_CHUNK, d), f32),
           pltpu.VMEM((tile // HG_CHUNK, nh, HG_HEAD_DIM, HG_HEAD_DIM), f32),
           pltpu.VMEM((tile // HG_CHUNK, nh, HG_CHUNK, HG_CHUNK), bf16)],
        input_output_aliases=aliases,
        compiler_params=_cparams(("parallel", "arbitrary")),
        name="hgrn2_gla_bwd" if reverse else "hgrn2_gla_fwd",
    )(*arrays)


def _hg_post_kernel(h_ref, o_ref_in, gate_ref, gn_ref, mods_ref, gains_ref, wo_ref, out_ref):
    o = o_ref_in[...]
    d = o.shape[-1]
    gn = gn_ref[...]
    pieces = []
    for hh in range(d // HG_HEAD_DIM):
        sl = slice(hh * HG_HEAD_DIM, (hh + 1) * HG_HEAD_DIM)
        pieces.append(_rms(o[:, sl], gn[:, sl]))
    on = jnp.concatenate(pieces, axis=-1)
    g = gate_ref[...].astype(f32)
    y = _dot((on * (g * jax.nn.sigmoid(g))).astype(bf16), wo_ref[...])
    out_ref[...] = _residual(h_ref[...], y, mods_ref, gains_ref, 0)


def _hg_post(h, o, gate, gnorm, mods, gains, wo, *, bsz, nt):
    d = wo.shape[0]
    return pl.pallas_call(
        _hg_post_kernel,
        out_shape=jax.ShapeDtypeStruct(h.shape, f32),
        grid=(bsz, nt),
        in_specs=[_tile_spec(d, nt), _tile_spec(d, nt), _tile_spec(d, nt), _full_spec(gnorm),
                  _mods_spec(d), _full_spec(gains), _full_spec(wo)],
        out_specs=_tile_spec(d, nt),
        compiler_params=_cparams(("parallel", "arbitrary")),
        name="hgrn2_readout",
    )(h, o, gate, gnorm, mods, gains, wo)


def _s5_kernel(*refs, reverse, accumulate, nct, bsz):
    if accumulate:
        (h_ref, modsa_ref, gains_ref, wd_ref, wr_ref, are_ref, aim_ref, dsk_ref, prev_ref,
         y_ref, z_ref, st_ref, ytb_ref) = refs
    else:
        h_ref, modsa_ref, gains_ref, wd_ref, wr_ref, are_ref, aim_ref, dsk_ref, y_ref, z_ref, st_ref, ytb_ref = refs
        prev_ref = None
    s = pl.program_id(0)
    _, ts, d = h_ref.shape
    rows = bsz * ts
    nslab = d // S5_SLAB
    lanes = z_ref.shape[2]
    nchunk = z_ref.shape[0] // 2
    per_slab = nchunk // nslab

    @pl.when(s == 0)
    def _():
        st_ref[...] = jnp.zeros_like(st_ref)

    is_ctx = s < nct
    h3 = h_ref[...]
    g = gains_ref[0:1, :]
    crow = slice(MOD_ROWS - 8, MOD_ROWS - 7)
    shift = jnp.where(is_ctx, modsa_ref[crow, 0:1, :], modsa_ref[0:bsz, 0:1, :])
    scale = jnp.where(is_ctx, modsa_ref[crow, 1:2, :], modsa_ref[0:bsz, 1:2, :])
    u3 = h3 * lax.rsqrt(jnp.mean(h3 * h3, axis=-1, keepdims=True) + EPS) * g * (1 + scale) + shift
    u = u3.reshape(rows, d)
    ro = lax.broadcasted_iota(jnp.int32, (rows, rows), 0)
    co = lax.broadcasted_iota(jnp.int32, (rows, rows), 1)
    to_tb = jnp.where((ro // bsz == co % ts) & (ro % bsz == co // ts), 1.0, 0.0).astype(bf16)
    to_bt = jnp.where((co // bsz == ro % ts) & (co % bsz == ro // ts), 1.0, 0.0).astype(bf16)
    ub = _dot(to_tb, u.astype(bf16)).astype(bf16)

    def drive(kb):
        v = _dot(ub[:, kb * S5_SLAB:(kb + 1) * S5_SLAB], wd_ref[kb])
        for cc in range(per_slab):
            z_ref[kb * per_slab + cc] = v[:, cc * lanes:(cc + 1) * lanes]
            z_ref[nchunk + kb * per_slab + cc] = v[:, (per_slab + cc) * lanes:(per_slab + cc + 1) * lanes]

    def scan(ck):
        ar = are_ref[ck]
        ai = aim_ref[ck]
        zr = st_ref[ck]
        zi = st_ref[nchunk + ck]
        for t in (range(ts - 1, -1, -1) if reverse else range(ts)):
            rs = slice(t * bsz, (t + 1) * bsz)
            nzr = ar * zr - ai * zi + z_ref[ck, rs, :]
            nzi = ar * zi + ai * zr + z_ref[nchunk + ck, rs, :]
            zr, zi = nzr, nzi
            z_ref[ck, rs, :] = zr
            z_ref[nchunk + ck, rs, :] = zi
        st_ref[ck] = zr
        st_ref[nchunk + ck] = zi

    def readout(kb):
        acc = None
        for cc in range(per_slab):
            ck = kb * per_slab + cc
            p = (_dot(z_ref[ck].astype(bf16), wr_ref[kb, cc * lanes:(cc + 1) * lanes, :])
                 + _dot(z_ref[nchunk + ck].astype(bf16), wr_ref[kb, (per_slab + cc) * lanes:(per_slab + cc + 1) * lanes, :]))
            acc = p if acc is None else acc + p
        ytb_ref[:, kb * S5_SLAB:(kb + 1) * S5_SLAB] = acc

    drive(0)
    for kb in range(nslab):
        if kb + 1 < nslab:
            drive(kb + 1)
        for cc in range(per_slab):
            scan(kb * per_slab + cc)
        readout(kb)

    ytb = ytb_ref[...]
    hi = ytb.astype(bf16)
    lo = (ytb - hi.astype(f32)).astype(bf16)
    y = _dot(to_bt, hi) + _dot(to_bt, lo)
    if prev_ref is not None:
        y = y + prev_ref[...].reshape(rows, d)
    else:
        y = y + dsk_ref[...] * u
    y_ref[...] = y.reshape(bsz, ts, d)


def _s5_scan(h, modsa, gains, wd, wr, a_re, a_im, dskip, prev, *, bsz, nct, nlt, reverse):
    _, ls, d = h.shape
    ts = S5_TSTEPS
    ntot = nct + nlt
    nchunk2, lanes = a_re.shape[0] * 2, a_re.shape[2]

    def tmap(s):
        if reverse:
            return (0, jnp.where(s < nct, nct - 1 - s, ntot + nct - 1 - s), 0)
        return (0, s, 0)

    spec = pl.BlockSpec((bsz, ts, d), tmap)

    def full(arr):
        nd = arr.ndim
        return pl.BlockSpec(arr.shape, lambda s: (0,) * nd)

    arrays = [h, modsa, gains, wd, wr, a_re, a_im, dskip]
    specs = [spec, full(modsa), full(gains), full(wd), full(wr), full(a_re), full(a_im), full(dskip)]
    aliases = {}
    if prev is not None:
        arrays.append(prev)
        specs.append(spec)
        aliases = {8: 0}
    return pl.pallas_call(
        functools.partial(_s5_kernel, reverse=reverse, accumulate=prev is not None, nct=nct, bsz=bsz),
        out_shape=jax.ShapeDtypeStruct(h.shape, f32),
        grid=(ntot,),
        in_specs=specs,
        out_specs=spec,
        scratch_shapes=[pltpu.VMEM((nchunk2, bsz * ts, lanes), f32), pltpu.VMEM((nchunk2, bsz, lanes), f32),
                        pltpu.VMEM((bsz * ts, d), f32)],
        input_output_aliases=aliases,
        compiler_params=_cparams(("arbitrary",)),
        name="s5_scan_bwd" if reverse else "s5_scan_fwd",
    )(*arrays)


def _s5_post_kernel(h_ref, y_ref, mods_ref, gains_ref, wg_ref, o_ref):
    d = h_ref.shape[-1]
    z = jax.nn.gelu(y_ref[...]).astype(bf16)
    r = _dot(z, wg_ref[...])
    out = r[:, :d] * jax.nn.sigmoid(r[:, d:])
    o_ref[...] = _residual(h_ref[...], out, mods_ref, gains_ref, 0)


def _s5_post(h, y, mods, gains, wg, *, bsz, nt):
    d = wg.shape[0]
    return pl.pallas_call(
        _s5_post_kernel,
        out_shape=jax.ShapeDtypeStruct(h.shape, f32),
        grid=(bsz, nt),
        in_specs=[_tile_spec(d, nt), _tile_spec(d, nt), _mods_spec(d), _full_spec(gains), _full_spec(wg)],
        out_specs=_tile_spec(d, nt),
        compiler_params=_cparams(("parallel", "arbitrary")),
        name="s5_glu",
    )(h, y, mods, gains, wg)


def _s5_weights(b_re, b_im, cf_re, cf_im, a_re, a_im, lanes, bsz):
    g, n, c = b_re.shape
    gps = S5_SLAB // c
    nslab = g // gps
    eye = jnp.eye(gps, dtype=f32)

    def drive(bm):
        t = bm.astype(f32).reshape(nslab, gps, n, c).transpose(0, 1, 3, 2)
        t = t[:, :, :, None, :] * eye[None, :, None, :, None]
        return t.reshape(nslab, gps * c, gps * n)

    wd = jnp.concatenate([drive(b_re), drive(b_im)], axis=-1).astype(bf16)

    def read(cm):
        t = cm.reshape(nslab, gps, c, n).transpose(0, 1, 3, 2)
        t = t[:, :, :, None, :] * eye[None, :, None, :, None]
        return t.reshape(nslab, gps * n, gps * c)

    wr = jnp.stack([jnp.concatenate([read(cf_re[dd]), -read(cf_im[dd])], axis=1) for dd in range(2)]).astype(bf16)

    def decay(a):
        t = a.reshape(2, (g * n) // lanes, 1, lanes)
        return jnp.broadcast_to(t, (2, (g * n) // lanes, bsz, lanes))

    return wd, wr, decay(a_re), decay(a_im)


def _na_bias_table(rpb):
    nh = rpb.shape[0]
    qc = jnp.arange(GRID_W)[:, None]
    kc = jnp.arange(GRID_W)[None, :]
    dc = jnp.clip(kc - qc, 1 - NA_COLS, NA_COLS - 1) + (NA_COLS - 1)
    q_start = jnp.clip(qc - NA_COLS // 2, 0, GRID_W - NA_COLS)
    in_win = (kc >= q_start) & (kc < q_start + NA_COLS)
    tab = rpb.astype(f32)[:, :, dc]
    tab = jnp.where(in_win[None, None], tab, NEG_BIG)
    offs = jnp.arange(NA_ROWS)[:, None] + jnp.arange(NA_ROWS)[None, :]
    t2 = tab[:, offs]
    t2 = t2.reshape(nh // 4, 4, NA_ROWS, NA_ROWS, GRID_W, GRID_W)
    t2 = t2.transpose(2, 0, 3, 5, 1, 4)
    return t2.reshape(NA_ROWS, nh // 4, NA_ROWS * GRID_W, 4 * GRID_W)


def _na_kernel(*refs, lc, nrows):
    q_ref, k_ref, v_ref = refs[:3]
    bias_refs = refs[3:3 + nrows]
    o_ref, s_scr, p_scr, m_scr = refs[3 + nrows:]
    r0 = pl.program_id(1) * nrows
    rows_total = (k_ref.shape[0] - lc) // GRID_W
    nloc = NA_ROWS * GRID_W
    d = q_ref.shape[-1]
    dh = d // NA_HEADS
    npair = NA_HEADS // 2
    pw = 2 * dh
    qw = 2 * pw
    lane = lax.broadcasted_iota(jnp.int32, (GRID_W, pw), 1)
    lane4 = lax.broadcasted_iota(jnp.int32, (GRID_W, qw), 1) // dh
    k0 = [pl.multiple_of(lc + jnp.clip(r0 + j - NA_ROWS // 2, 0, rows_total - NA_ROWS) * GRID_W, GRID_W)
          for j in range(nrows)]
    for j in range(nrows):
        for pp in range(npair // 2):
            sl = slice(pp * qw, (pp + 1) * qw)
            q4 = q_ref[j * GRID_W:(j + 1) * GRID_W, sl]
            zero = jnp.zeros_like(q4)
            qm = jnp.concatenate([jnp.where(lane4 == i, q4, zero) for i in range(4)], axis=0)
            s_loc = _dot_nt(k_ref[pl.ds(k0[j], nloc), sl], qm) + bias_refs[j][pp]
            s_ctx = _dot_nt(k_ref[0:lc, sl], qm)
            s_scr[j, pp, 0:nloc, :] = s_loc
            s_scr[j, pp, nloc:nloc + lc, :] = s_ctx
            m_scr[j, pp] = jnp.maximum(jnp.max(s_loc, axis=0, keepdims=True), jnp.max(s_ctx, axis=0, keepdims=True))
    for j in range(nrows):
        for pp in range(npair // 2):
            p_scr[j, pp] = jnp.exp(s_scr[j, pp] - m_scr[j, pp]).astype(bf16)
    ones_loc = jnp.ones((nloc, pw), bf16)
    ones_ctx = jnp.ones((lc, pw), bf16)
    for j in range(nrows):
        for p in range(npair):
            sl = slice(p * pw, (p + 1) * pw)
            ps = slice((p % 2) * pw, (p % 2 + 1) * pw)
            v_loc = jnp.concatenate([v_ref[pl.ds(k0[j], nloc), sl], ones_loc], axis=1)
            v_ctx = jnp.concatenate([v_ref[0:lc, sl], ones_ctx], axis=1)
            oa = _dot_tn(p_scr[j, p // 2, 0:nloc, ps], v_loc) + _dot_tn(p_scr[j, p // 2, nloc:nloc + lc, ps], v_ctx)
            o = oa[:, 0:pw] / oa[:, pw:qw]
            o_ref[j * GRID_W:(j + 1) * GRID_W, sl] = jnp.where(lane < dh, o[0:GRID_W], o[GRID_W:pw]).astype(o_ref.dtype)


def _na_core(q, k, v, bias, *, bsz, ls, lc, nrows):
    d = q.shape[1]
    nrow = (ls - lc) // GRID_W
    nq = nrows * GRID_W
    nkeys = NA_ROWS * GRID_W + lc

    def bias_spec(j):
        def bias_map(b, rp):
            r = rp * nrows + j
            return (jnp.clip(r - NA_ROWS // 2, 0, nrow - NA_ROWS) - r + NA_ROWS - 1, 0, 0, 0)
        return pl.BlockSpec((None,) + bias.shape[1:], bias_map, pipeline_mode=pl.Buffered(1))

    kv_spec = pl.BlockSpec((ls, d), lambda b, rp: (b, 0))
    return pl.pallas_call(
        functools.partial(_na_kernel, lc=lc, nrows=nrows),
        out_shape=jax.ShapeDtypeStruct((bsz * nrow * GRID_W, d), bf16),
        grid=(bsz, nrow // nrows),
        in_specs=[pl.BlockSpec((nq, d), lambda b, rp: (b * (ls // nq) + lc // nq + rp, 0)), kv_spec, kv_spec]
        + [bias_spec(j) for j in range(nrows)],
        out_specs=pl.BlockSpec((nq, d), lambda b, rp: (b * (nrow // nrows) + rp, 0)),
        scratch_shapes=[pltpu.VMEM((nrows, NA_HEADS // 4, nkeys, 4 * GRID_W), f32),
                        pltpu.VMEM((nrows, NA_HEADS // 4, nkeys, 4 * GRID_W), bf16),
                        pltpu.VMEM((nrows, NA_HEADS // 4, 1, 4 * GRID_W), f32)],
        compiler_params=_cparams(("parallel", "arbitrary")),
        name="na_core",
    )(q, k, v, *([bias] * nrows))


def _na_post_kernel(h_ref, o_ref_in, mods_ref, gains_ref, wo_ref, out_ref):
    y = _dot(o_ref_in[...], wo_ref[...])
    out_ref[...] = _residual(h_ref[...], y, mods_ref, gains_ref, 0)


def _na_post(h, o, mods, gains, wo, *, bsz, nt):
    d = wo.shape[0]
    tile = TILE_ROWS[0]
    lat_spec = pl.BlockSpec((tile, d), lambda b, t: (b * (nt - 1) + t, 0))
    return pl.pallas_call(
        _na_post_kernel,
        out_shape=jax.ShapeDtypeStruct((bsz * (nt - 1) * tile, d), f32),
        grid=(bsz, nt - 1),
        in_specs=[pl.BlockSpec((tile, d), lambda b, t: (b * nt + t + 1, 0)), lat_spec,
                  pl.BlockSpec((None, N_MOD, d), lambda b, t: (b, 0, 0)), _full_spec(gains), _full_spec(wo)],
        out_specs=lat_spec,
        compiler_params=_cparams(("parallel", "arbitrary")),
        name="na_out",
    )(h, o, mods, gains, wo)


def _mlp_lat(h_lat, mods, gains, w1, w2, *, bsz, ntl):
    d = w1.shape[0]
    tile = TILE_ROWS[0]
    spec = pl.BlockSpec((tile, d), lambda b, t: (b * ntl + t, 0))
    return pl.pallas_call(
        functools.partial(_mlp_kernel, ff_chunk=min(1024, w1.shape[1])),
        out_shape=jax.ShapeDtypeStruct(h_lat.shape, f32),
        grid=(bsz, ntl),
        in_specs=[spec, pl.BlockSpec((None, N_MOD, d), lambda b, t: (b, 0, 0)),
                  _full_spec(gains), _full_spec(w1), _full_spec(w2)],
        out_specs=spec,
        compiler_params=_cparams(("parallel", "arbitrary")),
        name="relu2_mlp_lat",
    )(h_lat, mods, gains, w1, w2)


def kernel(x, c, ctx, c_ctx, ada_w, ada_b, norm_gains, mlp_w_in, mlp_w_out, sc_w_in, sc_conv, sc_w_out, hg_w_in, hg_lower_bound, hg_norm, hg_w_out, s5_lam_re, s5_lam_im, s5_log_dt, s5_b_re, s5_b_im, s5_c_re, s5_c_im, s5_d, s5_w_glu, na_w_qkv, na_rpb, na_w_out):
    bsz, seq, d = x.shape
    lc = ctx.shape[1]
    depth = ada_w.shape[0]
    tile = TILE_ROWS[0]
    assert depth == 4 and lc == tile and seq % tile == 0 and seq % GRID_W == 0 and bsz <= MOD_ROWS - 8
    assert bsz % 8 == 0 and tile % S5_TSTEPS == 0
    ls = lc + seq
    nt = ls // tile
    nsub = 2
    assert (bsz * nt) % nsub == 0 and (nt - 1) % nsub == 0 and SUB == tile

    cc = jnp.zeros((MOD_ROWS, d), f32).at[:bsz].set(c.astype(f32)).at[MOD_ROWS - 8].set(c_ctx.astype(f32))
    mods_b = _mods(cc, ada_w, ada_b)
    mods_a = mods_b.transpose(0, 2, 1, 3)
    lb_all, a_re, a_im, cf_re, cf_im = _prep(hg_lower_bound, s5_lam_re[0], s5_lam_im[0], s5_log_dt[0],
                                             s5_c_re[0], s5_c_im[0])
    gains = norm_gains.astype(f32)
    w1 = mlp_w_in.astype(bf16)
    w2 = mlp_w_out.astype(bf16)
    xl = x.reshape(bsz * seq, d)
    xc = ctx.reshape(bsz * lc, d)

    bg, u = _proj((xc, xl), mods_a[0], gains[0], sc_w_in[0].astype(bf16), (bf16, bf16),
                  bsz=bsz, nt=nt, first=True, conv=True)
    h = _conv_post(xc, xl, bg, u, sc_conv[0].astype(f32), mods_a[0], gains[0], sc_w_out[0].astype(bf16),
                   w1[0], w2[0], bsz=bsz, nt=nt)

    q, v, gate, f_fwd, f_bwd = _proj_flat(h, mods_a[1], gains[1], hg_w_in[0].astype(bf16),
                                          (bf16, bf16, bf16, f32, f32), bsz=bsz, nt=nt, nsub=nsub)
    lb1 = lb_all[1:2]
    o = _gla(q, v, f_fwd, lb1, None, bsz=bsz, nt=nt, reverse=False)
    o = _gla(q, v, f_bwd, lb1, o, bsz=bsz, nt=nt, reverse=True)
    h = _post_mlp("hg", h, [o, gate], [hg_norm[0].astype(f32).reshape(1, d), hg_w_out[0].astype(bf16)],
                  mods_a[1], gains[1], w1[1], w2[1], bsz=bsz, nt=nt, nsub=nsub)

    lanes = 256
    wd, wr, ar, ai = _s5_weights(s5_b_re[0], s5_b_im[0], cf_re, cf_im, a_re, a_im, lanes, bsz)
    h3 = h.reshape(bsz, ls, d)
    nct, nlt = lc // S5_TSTEPS, seq // S5_TSTEPS
    dsk = s5_d[0].astype(f32).reshape(1, d)
    y = _s5_scan(h3, mods_a[2], gains[2], wd, wr[0], ar[0], ai[0], dsk, None, bsz=bsz, nct=nct, nlt=nlt, reverse=False)
    y = _s5_scan(h3, mods_a[2], gains[2], wd, wr[1], ar[1], ai[1], dsk, y, bsz=bsz, nct=nct, nlt=nlt, reverse=True)
    h = _post_mlp("s5", h, [y.reshape(bsz * ls, d)], [s5_w_glu[0].astype(bf16)],
                  mods_a[2], gains[2], w1[2], w2[2], bsz=bsz, nt=nt, nsub=nsub)

    dh = d // NA_HEADS
    qq, kk, vv = _proj_flat(h, mods_a[3], gains[3], na_w_qkv[0].astype(bf16), (bf16, bf16, bf16), bsz=bsz, nt=nt,
                            nsub=nsub, scales=(dh ** -0.5, 1.0, 1.0))
    bias = _na_bias_table(na_rpb[0])
    o = _na_core(qq, kk, vv, bias, bsz=bsz, ls=ls, lc=lc, nrows=2)
    hl = _post_mlp("na", h, [o], [na_w_out[0].astype(bf16)], mods_a[3], gains[3], w1[3], w2[3],
                   bsz=bsz, nt=nt, nsub=nsub, lat_only=True)
    return hl.reshape(bsz, seq, d)
```

```python
import functools
import math

import jax
import jax.numpy as jnp
from jax import lax
from jax.experimental import pallas as pl
from jax.experimental.pallas import tpu as pltpu

EPS = 1e-6
N_MOD = 6
MOD_ROWS = 24
HG_HEAD_DIM = 128
HG_CHUNK = 32
S5_GROUP = 16
S5_SLAB = 256
S5_TSTEPS = 16
GRID_W = 64
NA_HEADS = 16
NA_ROWS = 8
NA_COLS = 16
NEG_BIG = -1e30
VMEM_LIMIT = 56 * 1024 * 1024

bf16 = jnp.bfloat16
f32 = jnp.float32


def _cparams(sem):
    return pltpu.CompilerParams(dimension_semantics=sem, vmem_limit_bytes=VMEM_LIMIT)


def _rms(x, g):
    return x * lax.rsqrt(jnp.mean(x * x, axis=-1, keepdims=True) + EPS) * g


def _dot(a, b):
    return jnp.dot(a, b, preferred_element_type=f32)


def _dot_nt(a, b):
    return lax.dot_general(a, b, (((1,), (1,)), ((), ())), preferred_element_type=f32)


def _dot_tn(a, b):
    return lax.dot_general(a, b, (((0,), (0,)), ((), ())), preferred_element_type=f32)


def _split3(x):
    hi = x.astype(bf16)
    r = x - hi.astype(f32)
    mid = r.astype(bf16)
    lo = (r - mid.astype(f32)).astype(bf16)
    return hi, mid, lo


def _mods_kernel(cc_ref, w_ref, b_ref, o_ref):
    x = cc_ref[...]
    a = (x * jax.nn.sigmoid(x)).astype(bf16)
    o_ref[...] = _dot(a, w_ref[...].astype(bf16)) + b_ref[...]


def _mods(cc, ada_w, ada_b):
    depth, d, _ = ada_w.shape
    out = pl.pallas_call(
        _mods_kernel,
        out_shape=jax.ShapeDtypeStruct((depth, N_MOD, MOD_ROWS, d), f32),
        grid=(depth, N_MOD),
        in_specs=[pl.BlockSpec((MOD_ROWS, d), lambda i, j: (0, 0)),
                  pl.BlockSpec((None, d, d), lambda i, j: (i, 0, j)),
                  pl.BlockSpec((None, 1, d), lambda i, j: (i * N_MOD + j, 0, 0))],
        out_specs=pl.BlockSpec((None, None, MOD_ROWS, d), lambda i, j: (i, j, 0, 0)),
        compiler_params=_cparams(("arbitrary", "arbitrary")),
        name="ada_mods",
    )(cc, ada_w, ada_b.reshape(depth * N_MOD, 1, d))
    return out


def _prep_kernel(lbp_ref, lre_ref, lim_ref, ldt_ref, cre_ref, cim_ref,
                 lb_ref, are_ref, aim_ref, cfre_ref, cfim_ref):
    x = lbp_ref[...]
    rows = [x[i:i + 1, :] for i in range(x.shape[0])]
    m = functools.reduce(jnp.maximum, rows)
    es = [jnp.exp(r - m) for r in rows]
    tot = functools.reduce(lambda a, b: a + b, es)
    acc = None
    first = None
    for i, e in enumerate(es):
        sm = e / tot
        acc = sm if acc is None else acc + sm
        if first is None:
            first = acc
        lb_ref[i:i + 1, :] = acc - first
    lam_re = jnp.minimum(lre_ref[...], -1e-4)
    lam_im = lim_ref[...]
    dt = jnp.exp(ldt_ref[...])
    mag = jnp.exp(lam_re * dt)
    a_re = mag * jnp.cos(lam_im * dt)
    a_im = mag * jnp.sin(lam_im * dt)
    den = lam_re * lam_re + lam_im * lam_im
    f_re = ((a_re - 1) * lam_re + a_im * lam_im) / den
    f_im = (a_im * lam_re - (a_re - 1) * lam_im) / den
    are_ref[...] = a_re
    aim_ref[...] = a_im
    c_re, c_im = cre_ref[...], cim_ref[...]
    cfre_ref[...] = c_re * f_re - c_im * f_im
    cfim_ref[...] = c_re * f_im + c_im * f_re


def _prep(hg_lower_bound, lam_re, lam_im, log_dt, c_re, c_im):
    shape = c_re.shape
    flat = (shape[0] * shape[1] * shape[2], shape[3])

    def expand(t):
        return jnp.broadcast_to(t[:, :, None, :], shape).reshape(flat)

    ldt = jnp.broadcast_to(log_dt[:, :, None, None], shape).reshape(flat)
    outs = pl.pallas_call(
        _prep_kernel,
        out_shape=[jax.ShapeDtypeStruct(hg_lower_bound.shape, f32)] + [jax.ShapeDtypeStruct(flat, f32)] * 4,
        name="param_prep",
    )(hg_lower_bound.astype(f32), expand(lam_re.astype(f32)), expand(lam_im.astype(f32)), ldt.astype(f32),
      c_re.astype(f32).reshape(flat), c_im.astype(f32).reshape(flat))
    lb, a_re, a_im, cf_re, cf_im = outs
    a_re = a_re.reshape(shape)[:, :, 0, :]
    a_im = a_im.reshape(shape)[:, :, 0, :]
    return lb, a_re, a_im, cf_re.reshape(shape), cf_im.reshape(shape)


def _mod_row(b, t):
    return jnp.where(t == 0, MOD_ROWS - 8, b)


def _mods_spec(d):
    return pl.BlockSpec((None, N_MOD, d), lambda b, t: (_mod_row(b, t), 0, 0))


def _full_spec(arr):
    nd = arr.ndim
    return pl.BlockSpec(arr.shape, lambda *_: (0,) * nd, pipeline_mode=pl.Buffered(1))


SUB = 256


def _sub(j):
    return slice(j * SUB, (j + 1) * SUB)


def _flat_specs(nsub, d, nt, lat_only):
    def where(i, j):
        g = i * nsub + j
        if lat_only:
            b, t = g // (nt - 1), g % (nt - 1) + 1
        else:
            b, t = g // nt, g % nt
        return b, t

    h_specs = [pl.BlockSpec((SUB, d), lambda i, j=j: (where(i, j)[0] * nt + where(i, j)[1], 0)) for j in range(nsub)]
    m_specs = [pl.BlockSpec((None, N_MOD, d), lambda i, j=j: (_mod_row(*where(i, j)), 0, 0)) for j in range(nsub)]
    return h_specs, m_specs


def _rows_spec(nsub, d):
    return pl.BlockSpec((nsub * SUB, d), lambda i: (i, 0))


def _tile_spec(d, nt, off=0):
    return pl.BlockSpec((TILE_ROWS[0], d), lambda b, t: (b * nt + t + off, 0))


TILE_ROWS = [256]


def _prenorm(h, mods_ref, gains_ref, which):
    g = gains_ref[2 * which:2 * which + 1, :]
    shift = mods_ref[3 * which:3 * which + 1, :]
    scale = mods_ref[3 * which + 1:3 * which + 2, :]
    return _rms(h, g) * (1 + scale) + shift


def _residual(h, y, mods_ref, gains_ref, which):
    g = gains_ref[2 * which + 1:2 * which + 2, :]
    gate = mods_ref[3 * which + 2:3 * which + 3, :]
    return h + gate * _rms(y, g)


def _first_layer_h(t, hc_ref, hl_ref):
    return jnp.where(t == 0, hc_ref[...], hl_ref[...])


def _proj_kernel(*refs, split, first, conv):
    if first:
        hc_ref, hl_ref, mods_ref, gains_ref, w_ref = refs[:5]
        outs = refs[5:]
        h = _first_layer_h(pl.program_id(1), hc_ref, hl_ref)
    else:
        h_ref, mods_ref, gains_ref, w_ref = refs[:4]
        outs = refs[4:]
        h = h_ref[...]
    d = h.shape[-1]
    a = _prenorm(h, mods_ref, gains_ref, 0).astype(bf16)
    res = _dot(a, w_ref[...])
    if conv:
        bg_ref, u_ref = outs
        bg_ref[...] = res[:, :d].astype(bg_ref.dtype)
        u_ref[...] = (res[:, d:2 * d] * res[:, 2 * d:3 * d]).astype(u_ref.dtype)
    else:
        for k, (o_ref, sc) in enumerate(zip(outs, split)):
            piece = res[:, k * d:(k + 1) * d]
            if sc != 1.0:
                piece = piece * sc
            o_ref[...] = piece.astype(o_ref.dtype)


def _proj(h_in, mods, gains, w, out_dtypes, *, bsz, nt, first=False, conv=False, scales=None):
    d = w.shape[0]
    tile = TILE_ROWS[0]
    rows = bsz * nt * tile
    if first:
        hc, hl = h_in
        in_arrays = [hc, hl]
        in_specs = [pl.BlockSpec((tile, d), lambda b, t: (b, 0)),
                    pl.BlockSpec((tile, d), lambda b, t: (b * (nt - 1) + jnp.maximum(t - 1, 0), 0))]
    else:
        in_arrays = [h_in]
        in_specs = [_tile_spec(d, nt)]
    in_arrays += [mods, gains, w]
    in_specs += [_mods_spec(d), _full_spec(gains), _full_spec(w)]
    scales = scales or (1.0,) * len(out_dtypes)
    return pl.pallas_call(
        functools.partial(_proj_kernel, split=tuple(scales), first=first, conv=conv),
        out_shape=[jax.ShapeDtypeStruct((rows, d), dt) for dt in out_dtypes],
        grid=(bsz, nt),
        in_specs=in_specs,
        out_specs=[_tile_spec(d, nt) for _ in out_dtypes],
        compiler_params=_cparams(("parallel", "arbitrary")),
        name="prenorm_proj",
    )(*in_arrays)


def _proj_flat_kernel(*refs, nsub, scales):
    h_refs, mods_refs = refs[:nsub], refs[nsub:2 * nsub]
    gains_ref, w_ref = refs[2 * nsub:2 * nsub + 2]
    outs, a_s = refs[2 * nsub + 2:-1], refs[-1]
    d = a_s.shape[-1]
    for j in range(nsub):
        a_s[_sub(j), :] = _prenorm(h_refs[j][...], mods_refs[j], gains_ref, 0).astype(bf16)
    for k, (o_ref, sc) in enumerate(zip(outs, scales)):
        piece = _dot(a_s[...], w_ref[:, k * d:(k + 1) * d])
        if sc != 1.0:
            piece = piece * sc
        o_ref[...] = piece.astype(o_ref.dtype)


def _proj_flat(h, mods, gains, w, out_dtypes, *, bsz, nt, nsub, scales=None):
    d = w.shape[0]
    rows = bsz * nt * SUB
    h_specs, m_specs = _flat_specs(nsub, d, nt, False)
    scales = tuple(scales or (1.0,) * len(out_dtypes))
    return pl.pallas_call(
        functools.partial(_proj_flat_kernel, nsub=nsub, scales=scales),
        out_shape=[jax.ShapeDtypeStruct((rows, d), dt) for dt in out_dtypes],
        grid=(rows // (nsub * SUB),),
        in_specs=h_specs + m_specs + [_full_spec(gains), _full_spec(w)],
        out_specs=[_rows_spec(nsub, d) for _ in out_dtypes],
        scratch_shapes=[pltpu.VMEM((nsub * SUB, d), bf16)],
        compiler_params=_cparams(("arbitrary",)),
        name="prenorm_proj",
    )(*([h] * nsub + [mods] * nsub + [gains, w]))


def _mlp_chunks(a, w1_ref, w2_ref, ff_chunk):
    nchunk = w1_ref.shape[1] // ff_chunk

    def hidden(c):
        hid = jnp.maximum(_dot(a, w1_ref[:, c * ff_chunk:(c + 1) * ff_chunk]), 0.0)
        return (hid * hid).astype(bf16)

    acc = None
    nxt = hidden(0)
    for c in range(nchunk):
        cur = nxt
        if c + 1 < nchunk:
            nxt = hidden(c + 1)
        part = _dot(cur, w2_ref[c * ff_chunk:(c + 1) * ff_chunk, :])
        acc = part if acc is None else acc + part
    return acc


def _post_mlp_kernel(*refs, kind, nsub, ff_chunk):
    h_refs, mods_refs, gains_ref = refs[:nsub], refs[nsub:2 * nsub], refs[2 * nsub]
    ins, (w1_ref, w2_ref, out_ref, h1_s, a_s) = refs[2 * nsub + 1:-5], refs[-5:]
    d = out_ref.shape[-1]
    if kind == "hg":
        o_in, gate_ref, gn_ref, wo_ref = ins
        o, gn = o_in[...], gn_ref[...]
        on = jnp.concatenate([_rms(o[:, k * HG_HEAD_DIM:(k + 1) * HG_HEAD_DIM], gn[:, k * HG_HEAD_DIM:(k + 1) * HG_HEAD_DIM])
                              for k in range(d // HG_HEAD_DIM)], axis=-1)
        g = gate_ref[...].astype(f32)
        y = _dot((on * (g * jax.nn.sigmoid(g))).astype(bf16), wo_ref[...])
    elif kind == "s5":
        y_in, wg_ref = ins
        r = _dot(jax.nn.gelu(y_in[...]).astype(bf16), wg_ref[...])
        y = r[:, :d] * jax.nn.sigmoid(r[:, d:])
    else:
        o_in, wo_ref = ins
        y = _dot(o_in[...], wo_ref[...])
    for j in range(nsub):
        h1 = _residual(h_refs[j][...], y[_sub(j)], mods_refs[j], gains_ref, 0)
        h1_s[_sub(j), :] = h1
        a_s[_sub(j), :] = _prenorm(h1, mods_refs[j], gains_ref, 1).astype(bf16)
    acc = _mlp_chunks(a_s[...], w1_ref, w2_ref, ff_chunk)
    for j in range(nsub):
        out_ref[_sub(j), :] = _residual(h1_s[_sub(j), :], acc[_sub(j)], mods_refs[j], gains_ref, 1)


def _post_mlp(kind, h, ins, consts, mods, gains, w1, w2, *, bsz, nt, nsub, lat_only=False):
    d = w1.shape[0]
    rows = bsz * (nt - 1 if lat_only else nt) * SUB
    h_specs, m_specs = _flat_specs(nsub, d, nt, lat_only)
    return pl.pallas_call(
        functools.partial(_post_mlp_kernel, kind=kind, nsub=nsub, ff_chunk=min(1024, w1.shape[1])),
        out_shape=jax.ShapeDtypeStruct((rows, d), f32),
        grid=(rows // (nsub * SUB),),
        in_specs=(h_specs + m_specs + [_full_spec(gains)] + [_rows_spec(nsub, d) for _ in ins]
                  + [_full_spec(c_) for c_ in consts] + [_full_spec(w1), _full_spec(w2)]),
        out_specs=_rows_spec(nsub, d),
        scratch_shapes=[pltpu.VMEM((nsub * SUB, d), f32), pltpu.VMEM((nsub * SUB, d), bf16)],
        compiler_params=_cparams(("arbitrary",)),
        name=kind + "_out_mlp",
    )(*([h] * nsub + [mods] * nsub + [gains] + list(ins) + list(consts) + [w1, w2]))


def _mlp_body(h, mods_ref, gains_ref, w1_ref, w2_ref, ff_chunk):
    a = _prenorm(h, mods_ref, gains_ref, 1).astype(bf16)
    return _residual(h, _mlp_chunks(a, w1_ref, w2_ref, ff_chunk), mods_ref, gains_ref, 1)


def _conv_post_kernel(hc_ref, hl_ref, bg_ref, u_ref, up_ref, un_ref, cw_ref, mods_ref, gains_ref, wo_ref,
                      w1_ref, w2_ref, o_ref, *, nt, ff_chunk):
    t = pl.program_id(1)
    h = _first_layer_h(t, hc_ref, hl_ref)
    u = u_ref[...].astype(f32)
    rows = u.shape[0]
    hal = up_ref.shape[0]
    prev_row = jnp.where(t <= 1, 0.0, up_ref[hal - 1:hal, :].astype(f32))
    next_row = jnp.where((t == 0) | (t == nt - 1), 0.0, un_ref[0:1, :].astype(f32))
    ridx = lax.broadcasted_iota(jnp.int32, u.shape, 0)
    u_prev = jnp.where(ridx == 0, prev_row, pltpu.roll(u, 1, 0))
    u_next = jnp.where(ridx == rows - 1, next_row, pltpu.roll(u, rows - 1, 0))
    cw = cw_ref[...]
    conv = cw[0:1, :] * u_prev + cw[1:2, :] * u + cw[2:3, :] * u_next
    y = _dot((bg_ref[...].astype(f32) * conv).astype(bf16), wo_ref[...])
    h1 = _residual(h, y, mods_ref, gains_ref, 0)
    o_ref[...] = _mlp_body(h1, mods_ref, gains_ref, w1_ref, w2_ref, ff_chunk)


def _conv_post(hc, hl, bg, u, conv_w, mods, gains, wo, w1, w2, *, bsz, nt):
    d = wo.shape[0]
    tile = TILE_ROWS[0]
    hal = 16
    per = tile // hal
    nblk = bsz * nt * per
    return pl.pallas_call(
        functools.partial(_conv_post_kernel, nt=nt, ff_chunk=min(1024, w1.shape[1])),
        out_shape=jax.ShapeDtypeStruct((bsz * nt * tile, d), f32),
        grid=(bsz, nt),
        in_specs=[pl.BlockSpec((tile, d), lambda b, t: (b, 0)),
                  pl.BlockSpec((tile, d), lambda b, t: (b * (nt - 1) + jnp.maximum(t - 1, 0), 0)),
                  _tile_spec(d, nt), _tile_spec(d, nt),
                  pl.BlockSpec((hal, d), lambda b, t: (jnp.maximum((b * nt + t) * per - 1, 0), 0)),
                  pl.BlockSpec((hal, d), lambda b, t: (jnp.minimum((b * nt + t + 1) * per, nblk - 1), 0)),
                  _full_spec(conv_w), _mods_spec(d), _full_spec(gains), _full_spec(wo), _full_spec(w1), _full_spec(w2)],
        out_specs=_tile_spec(d, nt),
        compiler_params=_cparams(("parallel", "arbitrary")),
        name="conv_out_mlp",
    )(hc, hl, bg, u, u, u, conv_w, mods, gains, wo, w1, w2)


def _gla_kernel(*refs, reverse, accumulate):
    if accumulate:
        (q_ref, v_ref, f_ref, lb_ref, prev_ref, o_ref, st_ref,
         qin_s, kin_s, qout_s, kst_s, q2_s, k2_s, dec_s, kv_s, sc_s, sx_s) = refs
    else:
        (q_ref, v_ref, f_ref, lb_ref, o_ref, st_ref,
         qin_s, kin_s, qout_s, kst_s, q2_s, k2_s, dec_s, kv_s, sc_s, sx_s) = refs
        prev_ref = None
    tile, d = q_ref.shape
    nh = d // HG_HEAD_DIM
    c = HG_CHUNK
    nchunk = tile // c

    @pl.when(pl.program_id(1) == 0)
    def _():
        st_ref[...] = jnp.zeros_like(st_ref)

    ri = lax.broadcasted_iota(jnp.int32, (c, c), 0)
    ci = lax.broadcasted_iota(jnp.int32, (c, c), 1)
    causal = (ci >= ri) if reverse else (ci <= ri)
    tri = jnp.where(causal, 1.0, 0.0).astype(bf16)
    mid = c // 2 if reverse else c // 2 - 1
    last = 0 if reverse else c - 1
    lb = lb_ref[...]

    ngroup = nchunk // 2
    for g in range(ngroup):
        vals = {}
        for k in (2 * g, 2 * g + 1):
            rows = slice(k * c, (k + 1) * c)
            fg = lb + (1 - lb) * jax.nn.sigmoid(f_ref[rows, :])
            kk = 1 - fg
            lf = jnp.log(fg)
            hi = lf.astype(bf16)
            md = (lf - hi.astype(f32)).astype(bf16)
            bcum = _dot(tri, hi) + _dot(tri, md)
            b_mid = bcum[mid:mid + 1, :]
            b_last = bcum[last:last + 1, :]
            qq = q_ref[rows, :].astype(f32)
            qin_s[rows, :] = (qq * jnp.exp(bcum - b_mid)).astype(bf16)
            kin_s[rows, :] = (kk * jnp.exp(b_mid - bcum)).astype(bf16)
            vals[k] = (rows, qq * jnp.exp(bcum), kk * jnp.exp(b_last - bcum), jnp.exp(b_last))
        ka, kb = (2 * g + 1, 2 * g) if reverse else (2 * g, 2 * g + 1)
        rows_a, qo_a, ks_a, dec_a = vals[ka]
        rows_b, qo_b, ks_b, dec_b = vals[kb]
        qout_s[rows_b, :] = qo_b.astype(bf16)
        kst_s[rows_a, :] = ks_a.astype(bf16)
        q2_s[rows_a, :] = qo_a.astype(bf16)
        q2_s[rows_b, :] = (qo_b * dec_a).astype(bf16)
        k2_s[rows_a, :] = (ks_a * dec_b).astype(bf16)
        k2_s[rows_b, :] = ks_b.astype(bf16)
        dec_s[g:g + 1, :] = dec_a * dec_b

    def hs(h):
        return slice(h * HG_HEAD_DIM, (h + 1) * HG_HEAD_DIM)

    def cr(k):
        return slice(k * c, (k + 1) * c)

    def ab(g):
        return (2 * g + 1, 2 * g) if reverse else (2 * g, 2 * g + 1)

    units = [(k, h) for k in range(nchunk) for h in range(nh)]
    gunits = [(g, h) for g in range(ngroup) for h in range(nh)]
    for k, h in units:
        sc = _dot_nt(qin_s[cr(k), hs(h)], kin_s[cr(k), hs(h)])
        sc_s[k, h] = jnp.where(causal, sc, 0.0).astype(bf16)
    for g, h in gunits:
        ka, kb = ab(g)
        sx_s[g, h] = _dot_nt(qout_s[cr(kb), hs(h)], kst_s[cr(ka), hs(h)]).astype(bf16)
    for k, h in units:
        o_h = _dot(sc_s[k, h], v_ref[cr(k), hs(h)])
        ka, kb = ab(k // 2)
        if k == kb:
            o_h = o_h + _dot(sx_s[k // 2, h], v_ref[cr(ka), hs(h)])
        if prev_ref is not None:
            o_h = o_h + prev_ref[cr(k), hs(h)]
        o_ref[cr(k), hs(h)] = o_h
    for g, h in gunits:
        rows = slice(2 * g * c, (2 * g + 2) * c)
        kv_s[g, h] = _dot_tn(v_ref[rows, hs(h)], k2_s[rows, hs(h)])

    for h in range(nh):
        st = st_ref[h]
        for i in range(ngroup):
            g = (ngroup - 1 - i) if reverse else i
            rows = slice(2 * g * c, (2 * g + 2) * c)
            o_ref[rows, hs(h)] += _dot_nt(q2_s[rows, hs(h)], st.astype(bf16))
            st = st * dec_s[g:g + 1, hs(h)] + kv_s[g, h]
        st_ref[h] = st


def _gla(q, v, fraw, lb, prev, *, bsz, nt, reverse):
    d = q.shape[1]
    tile = TILE_ROWS[0]
    nh = d // HG_HEAD_DIM
    npair = tile // (2 * HG_CHUNK)

    def tmap(b, s):
        t = jnp.where(s == 0, 0, nt - s) if reverse else s
        return (b * nt + t, 0)

    spec = pl.BlockSpec((tile, d), tmap)
    arrays = [q, v, fraw, lb]
    specs = [spec, spec, spec, _full_spec(lb)]
    aliases = {}
    if prev is not None:
        arrays.append(prev)
        specs.append(spec)
        aliases = {4: 0}
    return pl.pallas_call(
        functools.partial(_gla_kernel, reverse=reverse, accumulate=prev is not None),
        out_shape=jax.ShapeDtypeStruct(q.shape, f32),
        grid=(bsz, nt),
        in_specs=specs,
        out_specs=spec,
        scratch_shapes=[pltpu.VMEM((nh, HG_HEAD_DIM, HG_HEAD_DIM), f32)]
        + [pltpu.VMEM((tile, d), bf16)] * 6
        + [pltpu.VMEM((npair, d), f32),
           pltpu.VMEM((npair, nh, HG_HEAD_DIM, HG_HEAD_DIM), f32),
           pltpu.VMEM((2 * npair, nh, HG_CHUNK, HG_CHUNK), bf16),
           pltpu.VMEM((npair, nh, HG_CHUNK, HG_CHUNK), bf16)],
        input_output_aliases=aliases,
        compiler_params=_cparams(("parallel", "arbitrary")),
        name="hgrn2_gla_bwd" if reverse else "hgrn2_gla_fwd",
    )(*arrays)


def _hg_post_kernel(h_ref, o_ref_in, gate_ref, gn_ref, mods_ref, gains_ref, wo_ref, out_ref):
    o = o_ref_in[...]
    d = o.shape[-1]
    gn = gn_ref[...]
    pieces = []
    for hh in range(d // HG_HEAD_DIM):
        sl = slice(hh * HG_HEAD_DIM, (hh + 1) * HG_HEAD_DIM)
        pieces.append(_rms(o[:, sl], gn[:, sl]))
    on = jnp.concatenate(pieces, axis=-1)
    g = gate_ref[...].astype(f32)
    y = _dot((on * (g * jax.nn.sigmoid(g))).astype(bf16), wo_ref[...])
    out_ref[...] = _residual(h_ref[...], y, mods_ref, gains_ref, 0)


def _hg_post(h, o, gate, gnorm, mods, gains, wo, *, bsz, nt):
    d = wo.shape[0]
    return pl.pallas_call(
        _hg_post_kernel,
        out_shape=jax.ShapeDtypeStruct(h.shape, f32),
        grid=(bsz, nt),
        in_specs=[_tile_spec(d, nt), _tile_spec(d, nt), _tile_spec(d, nt), _full_spec(gnorm),
                  _mods_spec(d), _full_spec(gains), _full_spec(wo)],
        out_specs=_tile_spec(d, nt),
        compiler_params=_cparams(("parallel", "arbitrary")),
        name="hgrn2_readout",
    )(h, o, gate, gnorm, mods, gains, wo)


def _s5_kernel(*refs, reverse, accumulate, nct, bsz):
    if accumulate:
        (h_ref, modsa_ref, gains_ref, wd_ref, wr_ref, are_ref, aim_ref, dsk_ref, prev_ref,
         y_ref, z_ref, st_ref, ytb_ref, u_s, ub_s) = refs
    else:
        (h_ref, modsa_ref, gains_ref, wd_ref, wr_ref, are_ref, aim_ref, dsk_ref,
         y_ref, z_ref, st_ref, ytb_ref, u_s, ub_s) = refs
        prev_ref = None
    s = pl.program_id(0)
    _, ts, d = h_ref.shape
    rows = bsz * ts
    nslab = d // S5_SLAB
    lanes = z_ref.shape[2]
    nchunk = z_ref.shape[0] // 2
    per_slab = nchunk // nslab

    @pl.when(s == 0)
    def _():
        st_ref[...] = jnp.zeros_like(st_ref)

    is_ctx = s < nct
    h3 = h_ref[...]
    g = gains_ref[0:1, :]
    crow = slice(MOD_ROWS - 8, MOD_ROWS - 7)
    shift = jnp.where(is_ctx, modsa_ref[crow, 0:1, :], modsa_ref[0:bsz, 0:1, :])
    scale = jnp.where(is_ctx, modsa_ref[crow, 1:2, :], modsa_ref[0:bsz, 1:2, :])
    u3 = h3 * lax.rsqrt(jnp.mean(h3 * h3, axis=-1, keepdims=True) + EPS) * g * (1 + scale) + shift
    u = u3.reshape(rows, d)
    ro = lax.broadcasted_iota(jnp.int32, (rows, rows), 0)
    co = lax.broadcasted_iota(jnp.int32, (rows, rows), 1)
    to_tb = jnp.where((ro // bsz == co % ts) & (ro % bsz == co // ts), 1.0, 0.0).astype(bf16)
    to_bt = jnp.where((co // bsz == ro % ts) & (co % bsz == ro // ts), 1.0, 0.0).astype(bf16)
    u_s[...] = u
    ub_s[...] = _dot(to_tb, u.astype(bf16)).astype(bf16)

    def drive(ck):
        kb, cc = divmod(ck, per_slab)
        ubk = ub_s[:, kb * S5_SLAB:(kb + 1) * S5_SLAB]
        z_ref[ck] = _dot(ubk, wd_ref[kb, :, cc * lanes:(cc + 1) * lanes])
        z_ref[nchunk + ck] = _dot(ubk, wd_ref[kb, :, (per_slab + cc) * lanes:(per_slab + cc + 1) * lanes])

    def scan(ck):
        ar = are_ref[ck]
        ai = aim_ref[ck]
        zr = st_ref[ck]
        zi = st_ref[nchunk + ck]
        for t in (range(ts - 1, -1, -1) if reverse else range(ts)):
            rs = slice(t * bsz, (t + 1) * bsz)
            nzr = ar * zr - ai * zi + z_ref[ck, rs, :]
            nzi = ar * zi + ai * zr + z_ref[nchunk + ck, rs, :]
            zr, zi = nzr, nzi
            z_ref[ck, rs, :] = zr
            z_ref[nchunk + ck, rs, :] = zi
        st_ref[ck] = zr
        st_ref[nchunk + ck] = zi

    def readout(ck):
        kb, cc = divmod(ck, per_slab)
        p = (_dot(z_ref[ck].astype(bf16), wr_ref[kb, cc * lanes:(cc + 1) * lanes, :])
             + _dot(z_ref[nchunk + ck].astype(bf16), wr_ref[kb, (per_slab + cc) * lanes:(per_slab + cc + 1) * lanes, :]))
        sl = slice(kb * S5_SLAB, (kb + 1) * S5_SLAB)
        if cc == 0:
            ytb_ref[:, sl] = p
        elif cc < per_slab - 1:
            ytb_ref[:, sl] += p
        else:
            ytb = ytb_ref[:, sl] + p
            hi = ytb.astype(bf16)
            lo = (ytb - hi.astype(f32)).astype(bf16)
            y = _dot(to_bt, hi) + _dot(to_bt, lo)
            if prev_ref is not None:
                y = y + prev_ref[:, :, sl].reshape(rows, S5_SLAB)
            else:
                y = y + dsk_ref[:, sl] * u_s[:, sl]
            y_ref[:, :, sl] = y.reshape(bsz, ts, S5_SLAB)

    lead = 2
    for ck in range(lead):
        drive(ck)
    for ck in range(nchunk):
        if ck + lead < nchunk:
            drive(ck + lead)
        scan(ck)
        if ck >= 1:
            readout(ck - 1)
    readout(nchunk - 1)


def _s5_scan(h, modsa, gains, wd, wr, a_re, a_im, dskip, prev, *, bsz, nct, nlt, reverse):
    _, ls, d = h.shape
    ts = S5_TSTEPS
    ntot = nct + nlt
    nchunk2, lanes = a_re.shape[0] * 2, a_re.shape[2]

    def tmap(s):
        if reverse:
            return (0, jnp.where(s < nct, nct - 1 - s, ntot + nct - 1 - s), 0)
        return (0, s, 0)

    spec = pl.BlockSpec((bsz, ts, d), tmap)

    def full(arr):
        nd = arr.ndim
        return pl.BlockSpec(arr.shape, lambda s: (0,) * nd)

    arrays = [h, modsa, gains, wd, wr, a_re, a_im, dskip]
    specs = [spec, full(modsa), full(gains), full(wd), full(wr), full(a_re), full(a_im), full(dskip)]
    aliases = {}
    if prev is not None:
        arrays.append(prev)
        specs.append(spec)
        aliases = {8: 0}
    return pl.pallas_call(
        functools.partial(_s5_kernel, reverse=reverse, accumulate=prev is not None, nct=nct, bsz=bsz),
        out_shape=jax.ShapeDtypeStruct(h.shape, f32),
        grid=(ntot,),
        in_specs=specs,
        out_specs=spec,
        scratch_shapes=[pltpu.VMEM((nchunk2, bsz * ts, lanes), f32), pltpu.VMEM((nchunk2, bsz, lanes), f32),
                        pltpu.VMEM((bsz * ts, d), f32), pltpu.VMEM((bsz * ts, d), f32),
                        pltpu.VMEM((bsz * ts, d), bf16)],
        input_output_aliases=aliases,
        compiler_params=_cparams(("arbitrary",)),
        name="s5_scan_bwd" if reverse else "s5_scan_fwd",
    )(*arrays)


def _s5_post_kernel(h_ref, y_ref, mods_ref, gains_ref, wg_ref, o_ref):
    d = h_ref.shape[-1]
    z = jax.nn.gelu(y_ref[...]).astype(bf16)
    r = _dot(z, wg_ref[...])
    out = r[:, :d] * jax.nn.sigmoid(r[:, d:])
    o_ref[...] = _residual(h_ref[...], out, mods_ref, gains_ref, 0)


def _s5_post(h, y, mods, gains, wg, *, bsz, nt):
    d = wg.shape[0]
    return pl.pallas_call(
        _s5_post_kernel,
        out_shape=jax.ShapeDtypeStruct(h.shape, f32),
        grid=(bsz, nt),
        in_specs=[_tile_spec(d, nt), _tile_spec(d, nt), _mods_spec(d), _full_spec(gains), _full_spec(wg)],
        out_specs=_tile_spec(d, nt),
        compiler_params=_cparams(("parallel", "arbitrary")),
        name="s5_glu",
    )(h, y, mods, gains, wg)


def _s5_weights(b_re, b_im, cf_re, cf_im, a_re, a_im, lanes, bsz):
    g, n, c = b_re.shape
    gps = S5_SLAB // c
    nslab = g // gps
    eye = jnp.eye(gps, dtype=f32)

    def drive(bm):
        t = bm.astype(f32).reshape(nslab, gps, n, c).transpose(0, 1, 3, 2)
        t = t[:, :, :, None, :] * eye[None, :, None, :, None]
        return t.reshape(nslab, gps * c, gps * n)

    wd = jnp.concatenate([drive(b_re), drive(b_im)], axis=-1).astype(bf16)

    def read(cm):
        t = cm.reshape(nslab, gps, c, n).transpose(0, 1, 3, 2)
        t = t[:, :, :, None, :] * eye[None, :, None, :, None]
        return t.reshape(nslab, gps * n, gps * c)

    wr = jnp.stack([jnp.concatenate([read(cf_re[dd]), -read(cf_im[dd])], axis=1) for dd in range(2)]).astype(bf16)

    def decay(a):
        t = a.reshape(2, (g * n) // lanes, 1, lanes)
        return jnp.broadcast_to(t, (2, (g * n) // lanes, bsz, lanes))

    return wd, wr, decay(a_re), decay(a_im)


def _na_bias_table(rpb):
    nh = rpb.shape[0]
    qc = jnp.arange(GRID_W)[:, None]
    kc = jnp.arange(GRID_W)[None, :]
    dc = jnp.clip(kc - qc, 1 - NA_COLS, NA_COLS - 1) + (NA_COLS - 1)
    q_start = jnp.clip(qc - NA_COLS // 2, 0, GRID_W - NA_COLS)
    in_win = (kc >= q_start) & (kc < q_start + NA_COLS)
    tab = rpb.astype(f32)[:, :, dc]
    tab = jnp.where(in_win[None, None], tab, NEG_BIG)
    offs = jnp.arange(NA_ROWS)[:, None] + jnp.arange(NA_ROWS)[None, :]
    t2 = tab[:, offs]
    t2 = t2.reshape(nh // 4, 4, NA_ROWS, NA_ROWS, GRID_W, GRID_W)
    t2 = t2.transpose(2, 0, 3, 5, 1, 4)
    return t2.reshape(NA_ROWS, nh // 4, NA_ROWS * GRID_W, 4 * GRID_W)


def _na_kernel(*refs, lc, nrows):
    q_ref, k_ref, v_ref = refs[:3]
    bias_refs = refs[3:3 + nrows]
    o_ref, s_scr, p_scr, m_scr = refs[3 + nrows:]
    r0 = pl.program_id(1) * nrows
    rows_total = (k_ref.shape[0] - lc) // GRID_W
    nloc = NA_ROWS * GRID_W
    d = q_ref.shape[-1]
    dh = d // NA_HEADS
    npair = NA_HEADS // 2
    pw = 2 * dh
    qw = 2 * pw
    lane = lax.broadcasted_iota(jnp.int32, (GRID_W, pw), 1)
    lane4 = lax.broadcasted_iota(jnp.int32, (GRID_W, qw), 1) // dh
    k0 = [pl.multiple_of(lc + jnp.clip(r0 + j - NA_ROWS // 2, 0, rows_total - NA_ROWS) * GRID_W, GRID_W)
          for j in range(nrows)]
    for j in range(nrows):
        for pp in range(npair // 2):
            sl = slice(pp * qw, (pp + 1) * qw)
            q4 = q_ref[j * GRID_W:(j + 1) * GRID_W, sl]
            zero = jnp.zeros_like(q4)
            qm = jnp.concatenate([jnp.where(lane4 == i, q4, zero) for i in range(4)], axis=0)
            s_loc = _dot_nt(k_ref[pl.ds(k0[j], nloc), sl], qm) + bias_refs[j][pp]
            s_ctx = _dot_nt(k_ref[0:lc, sl], qm)
            s_scr[j, pp, 0:nloc, :] = s_loc
            s_scr[j, pp, nloc:nloc + lc, :] = s_ctx
            m_scr[j, pp] = jnp.maximum(jnp.max(s_loc, axis=0, keepdims=True), jnp.max(s_ctx, axis=0, keepdims=True))
    for j in range(nrows):
        for pp in range(npair // 2):
            p_scr[j, pp] = jnp.exp(s_scr[j, pp] - m_scr[j, pp]).astype(bf16)
    ones_loc = jnp.ones((nloc, pw), bf16)
    ones_ctx = jnp.ones((lc, pw), bf16)
    for j in range(nrows):
        for p in range(npair):
            sl = slice(p * pw, (p + 1) * pw)
            ps = slice((p % 2) * pw, (p % 2 + 1) * pw)
            v_loc = jnp.concatenate([v_ref[pl.ds(k0[j], nloc), sl], ones_loc], axis=1)
            v_ctx = jnp.concatenate([v_ref[0:lc, sl], ones_ctx], axis=1)
            oa = _dot_tn(p_scr[j, p // 2, 0:nloc, ps], v_loc) + _dot_tn(p_scr[j, p // 2, nloc:nloc + lc, ps], v_ctx)
            o = oa[:, 0:pw] / oa[:, pw:qw]
            o_ref[j * GRID_W:(j + 1) * GRID_W, sl] = jnp.where(lane < dh, o[0:GRID_W], o[GRID_W:pw]).astype(o_ref.dtype)


def _na_core(q, k, v, bias, *, bsz, ls, lc, nrows):
    d = q.shape[1]
    nrow = (ls - lc) // GRID_W
    nq = nrows * GRID_W
    nkeys = NA_ROWS * GRID_W + lc

    def bias_spec(j):
        def bias_map(b, rp):
            r = rp * nrows + j
            return (jnp.clip(r - NA_ROWS // 2, 0, nrow - NA_ROWS) - r + NA_ROWS - 1, 0, 0, 0)
        return pl.BlockSpec((None,) + bias.shape[1:], bias_map, pipeline_mode=pl.Buffered(1))

    kv_spec = pl.BlockSpec((ls, d), lambda b, rp: (b, 0))
    return pl.pallas_call(
        functools.partial(_na_kernel, lc=lc, nrows=nrows),
        out_shape=jax.ShapeDtypeStruct((bsz * nrow * GRID_W, d), bf16),
        grid=(bsz, nrow // nrows),
        in_specs=[pl.BlockSpec((nq, d), lambda b, rp: (b * (ls // nq) + lc // nq + rp, 0)), kv_spec, kv_spec]
        + [bias_spec(j) for j in range(nrows)],
        out_specs=pl.BlockSpec((nq, d), lambda b, rp: (b * (nrow // nrows) + rp, 0)),
        scratch_shapes=[pltpu.VMEM((nrows, NA_HEADS // 4, nkeys, 4 * GRID_W), f32),
                        pltpu.VMEM((nrows, NA_HEADS // 4, nkeys, 4 * GRID_W), bf16),
                        pltpu.VMEM((nrows, NA_HEADS // 4, 1, 4 * GRID_W), f32)],
        compiler_params=_cparams(("parallel", "arbitrary")),
        name="na_core",
    )(q, k, v, *([bias] * nrows))


def _na_post_kernel(h_ref, o_ref_in, mods_ref, gains_ref, wo_ref, out_ref):
    y = _dot(o_ref_in[...], wo_ref[...])
    out_ref[...] = _residual(h_ref[...], y, mods_ref, gains_ref, 0)


def _na_post(h, o, mods, gains, wo, *, bsz, nt):
    d = wo.shape[0]
    tile = TILE_ROWS[0]
    lat_spec = pl.BlockSpec((tile, d), lambda b, t: (b * (nt - 1) + t, 0))
    return pl.pallas_call(
        _na_post_kernel,
        out_shape=jax.ShapeDtypeStruct((bsz * (nt - 1) * tile, d), f32),
        grid=(bsz, nt - 1),
        in_specs=[pl.BlockSpec((tile, d), lambda b, t: (b * nt + t + 1, 0)), lat_spec,
                  pl.BlockSpec((None, N_MOD, d), lambda b, t: (b, 0, 0)), _full_spec(gains), _full_spec(wo)],
        out_specs=lat_spec,
        compiler_params=_cparams(("parallel", "arbitrary")),
        name="na_out",
    )(h, o, mods, gains, wo)


def _mlp_lat(h_lat, mods, gains, w1, w2, *, bsz, ntl):
    d = w1.shape[0]
    tile = TILE_ROWS[0]
    spec = pl.BlockSpec((tile, d), lambda b, t: (b * ntl + t, 0))
    return pl.pallas_call(
        functools.partial(_mlp_kernel, ff_chunk=min(1024, w1.shape[1])),
        out_shape=jax.ShapeDtypeStruct(h_lat.shape, f32),
        grid=(bsz, ntl),
        in_specs=[spec, pl.BlockSpec((None, N_MOD, d), lambda b, t: (b, 0, 0)),
                  _full_spec(gains), _full_spec(w1), _full_spec(w2)],
        out_specs=spec,
        compiler_params=_cparams(("parallel", "arbitrary")),
        name="relu2_mlp_lat",
    )(h_lat, mods, gains, w1, w2)


def kernel(x, c, ctx, c_ctx, ada_w, ada_b, norm_gains, mlp_w_in, mlp_w_out, sc_w_in, sc_conv, sc_w_out, hg_w_in, hg_lower_bound, hg_norm, hg_w_out, s5_lam_re, s5_lam_im, s5_log_dt, s5_b_re, s5_b_im, s5_c_re, s5_c_im, s5_d, s5_w_glu, na_w_qkv, na_rpb, na_w_out):
    bsz, seq, d = x.shape
    lc = ctx.shape[1]
    depth = ada_w.shape[0]
    tile = TILE_ROWS[0]
    assert depth == 4 and lc == tile and seq % tile == 0 and seq % GRID_W == 0 and bsz <= MOD_ROWS - 8
    assert bsz % 8 == 0 and tile % S5_TSTEPS == 0
    ls = lc + seq
    nt = ls // tile
    nsub = 2
    assert (bsz * nt) % nsub == 0 and (nt - 1) % nsub == 0 and SUB == tile

    cc = jnp.zeros((MOD_ROWS, d), f32).at[:bsz].set(c.astype(f32)).at[MOD_ROWS - 8].set(c_ctx.astype(f32))
    mods_b = _mods(cc, ada_w, ada_b)
    mods_a = mods_b.transpose(0, 2, 1, 3)
    lb_all, a_re, a_im, cf_re, cf_im = _prep(hg_lower_bound, s5_lam_re[0], s5_lam_im[0], s5_log_dt[0],
                                             s5_c_re[0], s5_c_im[0])
    gains = norm_gains.astype(f32)
    w1 = mlp_w_in.astype(bf16)
    w2 = mlp_w_out.astype(bf16)
    xl = x.reshape(bsz * seq, d)
    xc = ctx.reshape(bsz * lc, d)

    bg, u = _proj((xc, xl), mods_a[0], gains[0], sc_w_in[0].astype(bf16), (bf16, bf16),
                  bsz=bsz, nt=nt, first=True, conv=True)
    h = _conv_post(xc, xl, bg, u, sc_conv[0].astype(f32), mods_a[0], gains[0], sc_w_out[0].astype(bf16),
                   w1[0], w2[0], bsz=bsz, nt=nt)

    q, v, gate, f_fwd, f_bwd = _proj_flat(h, mods_a[1], gains[1], hg_w_in[0].astype(bf16),
                                          (bf16, bf16, bf16, f32, f32), bsz=bsz, nt=nt, nsub=nsub)
    lb1 = lb_all[1:2]
    o = _gla(q, v, f_fwd, lb1, None, bsz=bsz, nt=nt, reverse=False)
    o = _gla(q, v, f_bwd, lb1, o, bsz=bsz, nt=nt, reverse=True)
    h = _post_mlp("hg", h, [o, gate], [hg_norm[0].astype(f32).reshape(1, d), hg_w_out[0].astype(bf16)],
                  mods_a[1], gains[1], w1[1], w2[1], bsz=bsz, nt=nt, nsub=nsub)

    lanes = 256
    wd, wr, ar, ai = _s5_weights(s5_b_re[0], s5_b_im[0], cf_re, cf_im, a_re, a_im, lanes, bsz)
    h3 = h.reshape(bsz, ls, d)
    nct, nlt = lc // S5_TSTEPS, seq // S5_TSTEPS
    dsk = s5_d[0].astype(f32).reshape(1, d)
    y = _s5_scan(h3, mods_a[2], gains[2], wd, wr[0], ar[0], ai[0], dsk, None, bsz=bsz, nct=nct, nlt=nlt, reverse=False)
    y = _s5_scan(h3, mods_a[2], gains[2], wd, wr[1], ar[1], ai[1], dsk, y, bsz=bsz, nct=nct, nlt=nlt, reverse=True)
    h = _post_mlp("s5", h, [y.reshape(bsz * ls, d)], [s5_w_glu[0].astype(bf16)],
                  mods_a[2], gains[2], w1[2], w2[2], bsz=bsz, nt=nt, nsub=nsub)

    dh = d // NA_HEADS
    qq, kk, vv = _proj_flat(h, mods_a[3], gains[3], na_w_qkv[0].astype(bf16), (bf16, bf16, bf16), bsz=bsz, nt=nt,
                            nsub=nsub, scales=(dh ** -0.5, 1.0, 1.0))
    bias = _na_bias_table(na_rpb[0])
    o = _na_core(qq, kk, vv, bias, bsz=bsz, ls=ls, lc=lc, nrows=2)
    hl = _post_mlp("na", h, [o], [na_w_out[0].astype(bf16)], mods_a[3], gains[3], w1[3], w2[3],
                   bsz=bsz, nt=nt, nsub=nsub, lat_only=True)
    return hl.reshape(bsz, seq, d)
```

```python
import functools

import jax
import jax.numpy as jnp
from jax import lax
from jax.experimental import pallas as pl
from jax.experimental.pallas import tpu as pltpu

EPS = 1e-6
N_MOD = 6
MOD_ROWS = 24
HG_HEAD_DIM = 128
HG_CHUNK = 32
S5_SLAB = 256
S5_TSTEPS = 16
S5_SUBTILES = 2
GRID_W = 64
NA_HEADS = 16
NA_ROWS = 8
NA_COLS = 16
NEG_BIG = -1e30
VMEM_LIMIT = 56 * 1024 * 1024

bf16 = jnp.bfloat16
f32 = jnp.float32


def _cparams(sem):
    return pltpu.CompilerParams(dimension_semantics=sem, vmem_limit_bytes=VMEM_LIMIT)


def _rms(x, g):
    return x * lax.rsqrt(jnp.mean(x * x, axis=-1, keepdims=True) + EPS) * g


def _dot(a, b):
    return jnp.dot(a, b, preferred_element_type=f32)


def _dot_nt(a, b):
    return lax.dot_general(a, b, (((1,), (1,)), ((), ())), preferred_element_type=f32)


def _dot_tn(a, b):
    return lax.dot_general(a, b, (((0,), (0,)), ((), ())), preferred_element_type=f32)


def _mods_kernel(cc_ref, w_ref, b_ref, o_ref):
    x = cc_ref[...]
    a = (x * jax.nn.sigmoid(x)).astype(bf16)
    o_ref[...] = _dot(a, w_ref[...].astype(bf16)) + b_ref[...]


def _mods(cc, ada_w, ada_b):
    depth, d, _ = ada_w.shape
    out = pl.pallas_call(
        _mods_kernel,
        out_shape=jax.ShapeDtypeStruct((depth, N_MOD, MOD_ROWS, d), f32),
        grid=(depth, N_MOD),
        in_specs=[pl.BlockSpec((MOD_ROWS, d), lambda i, j: (0, 0)),
                  pl.BlockSpec((None, d, d), lambda i, j: (i, 0, j)),
                  pl.BlockSpec((None, 1, d), lambda i, j: (i * N_MOD + j, 0, 0))],
        out_specs=pl.BlockSpec((None, None, MOD_ROWS, d), lambda i, j: (i, j, 0, 0)),
        compiler_params=_cparams(("arbitrary", "arbitrary")),
        name="ada_mods",
    )(cc, ada_w, ada_b.reshape(depth * N_MOD, 1, d))
    return out


def _prep_kernel(lbp_ref, lre_ref, lim_ref, ldt_ref, cre_ref, cim_ref,
                 lb_ref, are_ref, aim_ref, cfre_ref, cfim_ref):
    x = lbp_ref[...]
    rows = [x[i:i + 1, :] for i in range(x.shape[0])]
    m = functools.reduce(jnp.maximum, rows)
    es = [jnp.exp(r - m) for r in rows]
    tot = functools.reduce(lambda a, b: a + b, es)
    acc = None
    first = None
    for i, e in enumerate(es):
        sm = e / tot
        acc = sm if acc is None else acc + sm
        if first is None:
            first = acc
        lb_ref[i:i + 1, :] = acc - first
    lam_re = jnp.minimum(lre_ref[...], -1e-4)
    lam_im = lim_ref[...]
    dt = jnp.exp(ldt_ref[...])
    mag = jnp.exp(lam_re * dt)
    a_re = mag * jnp.cos(lam_im * dt)
    a_im = mag * jnp.sin(lam_im * dt)
    den = lam_re * lam_re + lam_im * lam_im
    f_re = ((a_re - 1) * lam_re + a_im * lam_im) / den
    f_im = (a_im * lam_re - (a_re - 1) * lam_im) / den
    are_ref[...] = a_re
    aim_ref[...] = a_im
    c_re, c_im = cre_ref[...], cim_ref[...]
    cfre_ref[...] = c_re * f_re - c_im * f_im
    cfim_ref[...] = c_re * f_im + c_im * f_re


def _prep(hg_lower_bound, lam_re, lam_im, log_dt, c_re, c_im):
    shape = c_re.shape
    flat = (shape[0] * shape[1] * shape[2], shape[3])

    def expand(t):
        return jnp.broadcast_to(t[:, :, None, :], shape).reshape(flat)

    ldt = jnp.broadcast_to(log_dt[:, :, None, None], shape).reshape(flat)
    outs = pl.pallas_call(
        _prep_kernel,
        out_shape=[jax.ShapeDtypeStruct(hg_lower_bound.shape, f32)] + [jax.ShapeDtypeStruct(flat, f32)] * 4,
        name="param_prep",
    )(hg_lower_bound.astype(f32), expand(lam_re.astype(f32)), expand(lam_im.astype(f32)), ldt.astype(f32),
      c_re.astype(f32).reshape(flat), c_im.astype(f32).reshape(flat))
    lb, a_re, a_im, cf_re, cf_im = outs
    a_re = a_re.reshape(shape)[:, :, 0, :]
    a_im = a_im.reshape(shape)[:, :, 0, :]
    return lb, a_re, a_im, cf_re.reshape(shape), cf_im.reshape(shape)


def _mod_row(b, t):
    return jnp.where(t == 0, MOD_ROWS - 8, b)


def _full_spec(arr):
    nd = arr.ndim
    return pl.BlockSpec(arr.shape, lambda *_: (0,) * nd, pipeline_mode=pl.Buffered(1))


SUB = 256


def _sub(j):
    return slice(j * SUB, (j + 1) * SUB)


def _flat_specs(nsub, d, nt, lat_only):
    def where(i, j):
        g = i * nsub + j
        if lat_only:
            b, t = g // (nt - 1), g % (nt - 1) + 1
        else:
            b, t = g // nt, g % nt
        return b, t

    h_specs = [pl.BlockSpec((SUB, d), lambda i, j=j: (where(i, j)[0] * nt + where(i, j)[1], 0)) for j in range(nsub)]
    m_specs = [pl.BlockSpec((None, N_MOD, d), lambda i, j=j: (_mod_row(*where(i, j)), 0, 0)) for j in range(nsub)]
    return h_specs, m_specs


def _rows_spec(nsub, d):
    return pl.BlockSpec((nsub * SUB, d), lambda i: (i, 0))


def _prenorm(h, mods_ref, gains_ref, which):
    g = gains_ref[2 * which:2 * which + 1, :]
    shift = mods_ref[3 * which:3 * which + 1, :]
    scale = mods_ref[3 * which + 1:3 * which + 2, :]
    return _rms(h, g) * (1 + scale) + shift


def _residual(h, y, mods_ref, gains_ref, which):
    g = gains_ref[2 * which + 1:2 * which + 2, :]
    gate = mods_ref[3 * which + 2:3 * which + 3, :]
    return h + gate * _rms(y, g)


def _first_layer_h(t, hc_ref, hl_ref):
    return jnp.where(t == 0, hc_ref[...], hl_ref[...])


def _proj_flat_kernel(*refs, nsub, scales):
    h_refs, mods_refs = refs[:nsub], refs[nsub:2 * nsub]
    gains_ref, w_ref = refs[2 * nsub:2 * nsub + 2]
    outs, a_s = refs[2 * nsub + 2:-1], refs[-1]
    d = a_s.shape[-1]
    for j in range(nsub):
        a_s[_sub(j), :] = _prenorm(h_refs[j][...], mods_refs[j], gains_ref, 0).astype(bf16)
    for k, (o_ref, sc) in enumerate(zip(outs, scales)):
        piece = _dot(a_s[...], w_ref[:, k * d:(k + 1) * d])
        if sc != 1.0:
            piece = piece * sc
        o_ref[...] = piece.astype(o_ref.dtype)


def _proj_flat(h, mods, gains, w, out_dtypes, *, bsz, nt, nsub, scales=None):
    d = w.shape[0]
    rows = bsz * nt * SUB
    h_specs, m_specs = _flat_specs(nsub, d, nt, False)
    scales = tuple(scales or (1.0,) * len(out_dtypes))
    return pl.pallas_call(
        functools.partial(_proj_flat_kernel, nsub=nsub, scales=scales),
        out_shape=[jax.ShapeDtypeStruct((rows, d), dt) for dt in out_dtypes],
        grid=(rows // (nsub * SUB),),
        in_specs=h_specs + m_specs + [_full_spec(gains), _full_spec(w)],
        out_specs=[_rows_spec(nsub, d) for _ in out_dtypes],
        scratch_shapes=[pltpu.VMEM((nsub * SUB, d), bf16)],
        compiler_params=_cparams(("arbitrary",)),
        name="prenorm_proj",
    )(*([h] * nsub + [mods] * nsub + [gains, w]))


def _mlp_chunks(a, w1_ref, w2_ref, ff_chunk):
    nchunk = w1_ref.shape[1] // ff_chunk

    def hidden(c):
        hid = jnp.maximum(_dot(a, w1_ref[:, c * ff_chunk:(c + 1) * ff_chunk]), 0.0)
        return (hid * hid).astype(bf16)

    acc = None
    nxt = hidden(0)
    for c in range(nchunk):
        cur = nxt
        if c + 1 < nchunk:
            nxt = hidden(c + 1)
        part = _dot(cur, w2_ref[c * ff_chunk:(c + 1) * ff_chunk, :])
        acc = part if acc is None else acc + part
    return acc


def _post_mlp_kernel(*refs, kind, nsub, ff_chunk):
    h_refs, mods_refs, gains_ref = refs[:nsub], refs[nsub:2 * nsub], refs[2 * nsub]
    ins, (w1_ref, w2_ref, out_ref, h1_s, a_s) = refs[2 * nsub + 1:-5], refs[-5:]
    d = out_ref.shape[-1]
    if kind == "hg":
        o_in, gate_ref, gn_ref, wo_ref = ins
        o, gn = o_in[...], gn_ref[...]
        on = jnp.concatenate([_rms(o[:, k * HG_HEAD_DIM:(k + 1) * HG_HEAD_DIM], gn[:, k * HG_HEAD_DIM:(k + 1) * HG_HEAD_DIM])
                              for k in range(d // HG_HEAD_DIM)], axis=-1)
        g = gate_ref[...].astype(f32)
        y = _dot((on * (g * jax.nn.sigmoid(g))).astype(bf16), wo_ref[...])
    elif kind == "s5":
        y_in, wg_ref = ins
        r = _dot(jax.nn.gelu(y_in[...]).astype(bf16), wg_ref[...])
        y = r[:, :d] * jax.nn.sigmoid(r[:, d:])
    else:
        o_in, wo_ref = ins
        y = _dot(o_in[...], wo_ref[...])
    for j in range(nsub):
        h1 = _residual(h_refs[j][...], y[_sub(j)], mods_refs[j], gains_ref, 0)
        h1_s[_sub(j), :] = h1
        a_s[_sub(j), :] = _prenorm(h1, mods_refs[j], gains_ref, 1).astype(bf16)
    acc = _mlp_chunks(a_s[...], w1_ref, w2_ref, ff_chunk)
    for j in range(nsub):
        out_ref[_sub(j), :] = _residual(h1_s[_sub(j), :], acc[_sub(j)], mods_refs[j], gains_ref, 1)


def _post_mlp(kind, h, ins, consts, mods, gains, w1, w2, *, bsz, nt, nsub, lat_only=False):
    d = w1.shape[0]
    rows = bsz * (nt - 1 if lat_only else nt) * SUB
    h_specs, m_specs = _flat_specs(nsub, d, nt, lat_only)
    return pl.pallas_call(
        functools.partial(_post_mlp_kernel, kind=kind, nsub=nsub, ff_chunk=min(1024, w1.shape[1])),
        out_shape=jax.ShapeDtypeStruct((rows, d), f32),
        grid=(rows // (nsub * SUB),),
        in_specs=(h_specs + m_specs + [_full_spec(gains)] + [_rows_spec(nsub, d) for _ in ins]
                  + [_full_spec(c_) for c_ in consts] + [_full_spec(w1), _full_spec(w2)]),
        out_specs=_rows_spec(nsub, d),
        scratch_shapes=[pltpu.VMEM((nsub * SUB, d), f32), pltpu.VMEM((nsub * SUB, d), bf16)],
        compiler_params=_cparams(("arbitrary",)),
        name=kind + "_out_mlp",
    )(*([h] * nsub + [mods] * nsub + [gains] + list(ins) + list(consts) + [w1, w2]))


def _first_specs(nsub, d, nt):
    def where(i, j):
        g = i * nsub + j
        return g // nt, g % nt

    c_specs = [pl.BlockSpec((SUB, d), lambda i, j=j: (where(i, j)[0], 0)) for j in range(nsub)]
    l_specs = [pl.BlockSpec((SUB, d), lambda i, j=j: (where(i, j)[0] * (nt - 1) + jnp.maximum(where(i, j)[1] - 1, 0), 0))
               for j in range(nsub)]
    m_specs = [pl.BlockSpec((None, N_MOD, d), lambda i, j=j: (_mod_row(*where(i, j)), 0, 0)) for j in range(nsub)]
    return c_specs, l_specs, m_specs


def _sub_t(nsub, nt, j):
    return (pl.program_id(0) * nsub + j) % nt


def _conv_proj_kernel(*refs, nsub, nt):
    hc_refs, hl_refs, mods_refs = refs[:nsub], refs[nsub:2 * nsub], refs[2 * nsub:3 * nsub]
    gains_ref, w_ref, bg_ref, u_ref, a_s = refs[3 * nsub:]
    d = a_s.shape[-1]
    for j in range(nsub):
        h = _first_layer_h(_sub_t(nsub, nt, j), hc_refs[j], hl_refs[j])
        a_s[_sub(j), :] = _prenorm(h, mods_refs[j], gains_ref, 0).astype(bf16)
    a = a_s[...]
    bg_ref[...] = _dot(a, w_ref[:, 0:d]).astype(bg_ref.dtype)
    u_ref[...] = (_dot(a, w_ref[:, d:2 * d]) * _dot(a, w_ref[:, 2 * d:3 * d])).astype(u_ref.dtype)


def _conv_proj(hc, hl, mods, gains, w, *, bsz, nt, nsub):
    d = w.shape[0]
    rows = bsz * nt * SUB
    c_specs, l_specs, m_specs = _first_specs(nsub, d, nt)
    return pl.pallas_call(
        functools.partial(_conv_proj_kernel, nsub=nsub, nt=nt),
        out_shape=[jax.ShapeDtypeStruct((rows, d), bf16)] * 2,
        grid=(rows // (nsub * SUB),),
        in_specs=c_specs + l_specs + m_specs + [_full_spec(gains), _full_spec(w)],
        out_specs=[_rows_spec(nsub, d)] * 2,
        scratch_shapes=[pltpu.VMEM((nsub * SUB, d), bf16)],
        compiler_params=_cparams(("arbitrary",)),
        name="conv_proj",
    )(*([hc] * nsub + [hl] * nsub + [mods] * nsub + [gains, w]))


def _conv_post_kernel(*refs, nsub, nt, ff_chunk):
    hc_refs, hl_refs, mods_refs = refs[:nsub], refs[nsub:2 * nsub], refs[2 * nsub:3 * nsub]
    (gains_ref, bg_ref, u_ref, up_ref, un_ref, cw_ref, wo_ref, w1_ref, w2_ref, o_ref, h1_s, a_s) = refs[3 * nsub:]
    u = u_ref[...].astype(f32)
    rows = u.shape[0]
    hal = up_ref.shape[0]
    ts = [_sub_t(nsub, nt, j) for j in range(nsub)]
    starts = [t <= 1 for t in ts]
    ends = [(t == 0) | (t == nt - 1) for t in ts]
    ridx = lax.broadcasted_iota(jnp.int32, u.shape, 0)
    u_prev = jnp.where(ridx == 0, up_ref[hal - 1:hal, :].astype(f32), pltpu.roll(u, 1, 0))
    u_next = jnp.where(ridx == rows - 1, un_ref[0:1, :].astype(f32), pltpu.roll(u, rows - 1, 0))
    for j in range(nsub):
        u_prev = jnp.where((ridx == j * SUB) & starts[j], 0.0, u_prev)
        u_next = jnp.where((ridx == (j + 1) * SUB - 1) & ends[j], 0.0, u_next)
    cw = cw_ref[...]
    conv = cw[0:1, :] * u_prev + cw[1:2, :] * u + cw[2:3, :] * u_next
    y = _dot((bg_ref[...].astype(f32) * conv).astype(bf16), wo_ref[...])
    for j in range(nsub):
        h = _first_layer_h(ts[j], hc_refs[j], hl_refs[j])
        h1 = _residual(h, y[_sub(j)], mods_refs[j], gains_ref, 0)
        h1_s[_sub(j), :] = h1
        a_s[_sub(j), :] = _prenorm(h1, mods_refs[j], gains_ref, 1).astype(bf16)
    acc = _mlp_chunks(a_s[...], w1_ref, w2_ref, ff_chunk)
    for j in range(nsub):
        o_ref[_sub(j), :] = _residual(h1_s[_sub(j), :], acc[_sub(j)], mods_refs[j], gains_ref, 1)


def _conv_post(hc, hl, bg, u, conv_w, mods, gains, wo, w1, w2, *, bsz, nt, nsub):
    d = wo.shape[0]
    tile = nsub * SUB
    hal = 16
    per = tile // hal
    rows = bsz * nt * SUB
    nblk = rows // hal
    c_specs, l_specs, m_specs = _first_specs(nsub, d, nt)
    return pl.pallas_call(
        functools.partial(_conv_post_kernel, nsub=nsub, nt=nt, ff_chunk=min(1024, w1.shape[1])),
        out_shape=jax.ShapeDtypeStruct((rows, d), f32),
        grid=(rows // tile,),
        in_specs=c_specs + l_specs + m_specs + [
            _full_spec(gains), _rows_spec(nsub, d), _rows_spec(nsub, d),
            pl.BlockSpec((hal, d), lambda i: (jnp.maximum(i * per - 1, 0), 0)),
            pl.BlockSpec((hal, d), lambda i: (jnp.minimum((i + 1) * per, nblk - 1), 0)),
            _full_spec(conv_w), _full_spec(wo), _full_spec(w1), _full_spec(w2)],
        out_specs=_rows_spec(nsub, d),
        scratch_shapes=[pltpu.VMEM((tile, d), f32), pltpu.VMEM((tile, d), bf16)],
        compiler_params=_cparams(("arbitrary",)),
        name="conv_out_mlp",
    )(*([hc] * nsub + [hl] * nsub + [mods] * nsub + [gains, bg, u, u, u, conv_w, wo, w1, w2]))


def _gla_kernel(*refs, reverse, accumulate):
    if accumulate:
        (q_ref, v_ref, f_ref, lb_ref, prev_ref, o_ref, st_ref,
         qin_s, kin_s, qout_s, kst_s, q2_s, k2_s, dec_s, kv_s, sc_s, sx_s) = refs
    else:
        (q_ref, v_ref, f_ref, lb_ref, o_ref, st_ref,
         qin_s, kin_s, qout_s, kst_s, q2_s, k2_s, dec_s, kv_s, sc_s, sx_s) = refs
        prev_ref = None
    tile, d = q_ref.shape
    nh = d // HG_HEAD_DIM
    c = HG_CHUNK
    nchunk = tile // c

    @pl.when(pl.program_id(1) == 0)
    def _():
        st_ref[...] = jnp.zeros_like(st_ref)

    ri = lax.broadcasted_iota(jnp.int32, (c, c), 0)
    ci = lax.broadcasted_iota(jnp.int32, (c, c), 1)
    causal = (ci >= ri) if reverse else (ci <= ri)
    tri = jnp.where(causal, 1.0, 0.0).astype(bf16)
    mid = c // 2 if reverse else c // 2 - 1
    last = 0 if reverse else c - 1
    lb = lb_ref[...]

    ngroup = nchunk // 2
    for g in range(ngroup):
        vals = {}
        for k in (2 * g, 2 * g + 1):
            rows = slice(k * c, (k + 1) * c)
            fg = lb + (1 - lb) * jax.nn.sigmoid(f_ref[rows, :])
            kk = 1 - fg
            lf = jnp.log(fg)
            hi = lf.astype(bf16)
            md = (lf - hi.astype(f32)).astype(bf16)
            bcum = _dot(tri, hi) + _dot(tri, md)
            b_mid = bcum[mid:mid + 1, :]
            b_last = bcum[last:last + 1, :]
            qq = q_ref[rows, :].astype(f32)
            qin_s[rows, :] = (qq * jnp.exp(bcum - b_mid)).astype(bf16)
            kin_s[rows, :] = (kk * jnp.exp(b_mid - bcum)).astype(bf16)
            vals[k] = (rows, qq * jnp.exp(bcum), kk * jnp.exp(b_last - bcum), jnp.exp(b_last))
        ka, kb = (2 * g + 1, 2 * g) if reverse else (2 * g, 2 * g + 1)
        rows_a, qo_a, ks_a, dec_a = vals[ka]
        rows_b, qo_b, ks_b, dec_b = vals[kb]
        qout_s[rows_b, :] = qo_b.astype(bf16)
        kst_s[rows_a, :] = ks_a.astype(bf16)
        q2_s[rows_a, :] = qo_a.astype(bf16)
        q2_s[rows_b, :] = (qo_b * dec_a).astype(bf16)
        k2_s[rows_a, :] = (ks_a * dec_b).astype(bf16)
        k2_s[rows_b, :] = ks_b.astype(bf16)
        dec_s[g:g + 1, :] = dec_a * dec_b

    def hs(h):
        return slice(h * HG_HEAD_DIM, (h + 1) * HG_HEAD_DIM)

    def cr(k):
        return slice(k * c, (k + 1) * c)

    def ab(g):
        return (2 * g + 1, 2 * g) if reverse else (2 * g, 2 * g + 1)

    units = [(k, h) for k in range(nchunk) for h in range(nh)]
    gunits = [(g, h) for g in range(ngroup) for h in range(nh)]
    for k, h in units:
        sc = _dot_nt(qin_s[cr(k), hs(h)], kin_s[cr(k), hs(h)])
        sc_s[k, h] = jnp.where(causal, sc, 0.0).astype(bf16)
    for g, h in gunits:
        ka, kb = ab(g)
        sx_s[g, h] = _dot_nt(qout_s[cr(kb), hs(h)], kst_s[cr(ka), hs(h)]).astype(bf16)
    for k, h in units:
        o_h = _dot(sc_s[k, h], v_ref[cr(k), hs(h)])
        ka, kb = ab(k // 2)
        if k == kb:
            o_h = o_h + _dot(sx_s[k // 2, h], v_ref[cr(ka), hs(h)])
        if prev_ref is not None:
            o_h = o_h + prev_ref[cr(k), hs(h)]
        o_ref[cr(k), hs(h)] = o_h
    for g, h in gunits:
        rows = slice(2 * g * c, (2 * g + 2) * c)
        kv_s[g, h] = _dot_tn(v_ref[rows, hs(h)], k2_s[rows, hs(h)])

    for h in range(nh):
        st = st_ref[h]
        for i in range(ngroup):
            g = (ngroup - 1 - i) if reverse else i
            rows = slice(2 * g * c, (2 * g + 2) * c)
            o_ref[rows, hs(h)] += _dot_nt(q2_s[rows, hs(h)], st.astype(bf16))
            st = st * dec_s[g:g + 1, hs(h)] + kv_s[g, h]
        st_ref[h] = st


def _gla(q, v, fraw, lb, prev, *, bsz, nt, reverse):
    d = q.shape[1]
    tile = SUB
    nh = d // HG_HEAD_DIM
    npair = tile // (2 * HG_CHUNK)

    def tmap(b, s):
        t = jnp.where(s == 0, 0, nt - s) if reverse else s
        return (b * nt + t, 0)

    spec = pl.BlockSpec((tile, d), tmap)
    arrays = [q, v, fraw, lb]
    specs = [spec, spec, spec, _full_spec(lb)]
    aliases = {}
    if prev is not None:
        arrays.append(prev)
        specs.append(spec)
        aliases = {4: 0}
    return pl.pallas_call(
        functools.partial(_gla_kernel, reverse=reverse, accumulate=prev is not None),
        out_shape=jax.ShapeDtypeStruct(q.shape, f32),
        grid=(bsz, nt),
        in_specs=specs,
        out_specs=spec,
        scratch_shapes=[pltpu.VMEM((nh, HG_HEAD_DIM, HG_HEAD_DIM), f32)]
        + [pltpu.VMEM((tile, d), bf16)] * 6
        + [pltpu.VMEM((npair, d), f32),
           pltpu.VMEM((npair, nh, HG_HEAD_DIM, HG_HEAD_DIM), f32),
           pltpu.VMEM((2 * npair, nh, HG_CHUNK, HG_CHUNK), bf16),
           pltpu.VMEM((npair, nh, HG_CHUNK, HG_CHUNK), bf16)],
        input_output_aliases=aliases,
        compiler_params=_cparams(("parallel", "arbitrary")),
        name="hgrn2_gla_bwd" if reverse else "hgrn2_gla_fwd",
    )(*arrays)


def _s5_kernel(*refs, reverse, accumulate, nct, bsz):
    if accumulate:
        (h_ref, modsa_ref, gains_ref, wd_ref, wr_ref, are_ref, aim_ref, dsk_ref, prev_ref,
         y_ref, z_ref, st_ref, ytb_ref, u_s, ub_s) = refs
    else:
        (h_ref, modsa_ref, gains_ref, wd_ref, wr_ref, are_ref, aim_ref, dsk_ref,
         y_ref, z_ref, st_ref, ytb_ref, u_s, ub_s) = refs
        prev_ref = None
    s = pl.program_id(0)
    d = h_ref.shape[-1]
    ts = S5_TSTEPS
    nhalf = h_ref.shape[1] // ts
    rows = bsz * ts
    nslab = d // S5_SLAB
    lanes = z_ref.shape[3]
    nchunk = z_ref.shape[1] // 2
    per_slab = nchunk // nslab

    @pl.when(s == 0)
    def _():
        st_ref[...] = jnp.zeros_like(st_ref)

    is_ctx = s < nct
    g = gains_ref[0:1, :]
    crow = slice(MOD_ROWS - 8, MOD_ROWS - 7)
    shift = jnp.where(is_ctx, modsa_ref[crow, 0:1, :], modsa_ref[0:bsz, 0:1, :])
    scale = jnp.where(is_ctx, modsa_ref[crow, 1:2, :], modsa_ref[0:bsz, 1:2, :])
    ro = lax.broadcasted_iota(jnp.int32, (rows, rows), 0)
    co = lax.broadcasted_iota(jnp.int32, (rows, rows), 1)
    to_tb = jnp.where((ro // bsz == co % ts) & (ro % bsz == co // ts), 1.0, 0.0).astype(bf16)
    to_bt = jnp.where((co // bsz == ro % ts) & (co % bsz == ro // ts), 1.0, 0.0).astype(bf16)

    def head(hf):
        h3 = h_ref[:, hf * ts:(hf + 1) * ts, :]
        u3 = h3 * lax.rsqrt(jnp.mean(h3 * h3, axis=-1, keepdims=True) + EPS) * g * (1 + scale) + shift
        u = u3.reshape(rows, d)
        u_s[hf] = u
        ub_s[hf] = _dot(to_tb, u.astype(bf16)).astype(bf16)

    def drive(hf, ck):
        kb, cc = divmod(ck, per_slab)
        ubk = ub_s[hf, :, kb * S5_SLAB:(kb + 1) * S5_SLAB]
        z_ref[hf, ck] = _dot(ubk, wd_ref[kb, :, cc * lanes:(cc + 1) * lanes])
        z_ref[hf, nchunk + ck] = _dot(ubk, wd_ref[kb, :, (per_slab + cc) * lanes:(per_slab + cc + 1) * lanes])

    def scan(hf, ck):
        ar = are_ref[ck]
        ai = aim_ref[ck]
        zr = st_ref[ck]
        zi = st_ref[nchunk + ck]
        for t in (range(ts - 1, -1, -1) if reverse else range(ts)):
            rs = slice(t * bsz, (t + 1) * bsz)
            nzr = ar * zr - ai * zi + z_ref[hf, ck, rs, :]
            nzi = ar * zi + ai * zr + z_ref[hf, nchunk + ck, rs, :]
            zr, zi = nzr, nzi
            z_ref[hf, ck, rs, :] = zr
            z_ref[hf, nchunk + ck, rs, :] = zi
        st_ref[ck] = zr
        st_ref[nchunk + ck] = zi

    def readout(hf, ck):
        kb, cc = divmod(ck, per_slab)
        p = (_dot(z_ref[hf, ck].astype(bf16), wr_ref[kb, cc * lanes:(cc + 1) * lanes, :])
             + _dot(z_ref[hf, nchunk + ck].astype(bf16), wr_ref[kb, (per_slab + cc) * lanes:(per_slab + cc + 1) * lanes, :]))
        sl = slice(kb * S5_SLAB, (kb + 1) * S5_SLAB)
        if cc == 0:
            ytb_ref[hf, :, sl] = p
        elif cc < per_slab - 1:
            ytb_ref[hf, :, sl] += p
        else:
            ytb = ytb_ref[hf, :, sl] + p
            hi = ytb.astype(bf16)
            lo = (ytb - hi.astype(f32)).astype(bf16)
            y = _dot(to_bt, hi) + _dot(to_bt, lo)
            tsl = slice(hf * ts, (hf + 1) * ts)
            if prev_ref is not None:
                y = y + prev_ref[:, tsl, sl].reshape(rows, S5_SLAB)
            else:
                y = y + dsk_ref[:, sl] * u_s[hf, :, sl]
            y_ref[:, tsl, sl] = y.reshape(bsz, ts, S5_SLAB)

    def pipeline(hf):
        lead = 2
        steps = [functools.partial(drive, hf, ck) for ck in range(lead)]
        for ck in range(nchunk):
            if ck + lead < nchunk:
                steps.append(functools.partial(drive, hf, ck + lead))
            steps.append(functools.partial(scan, hf, ck))
            if ck >= 1:
                steps.append(functools.partial(readout, hf, ck - 1))
        steps.append(functools.partial(readout, hf, nchunk - 1))
        return steps

    order = list(range(nhalf - 1, -1, -1)) if reverse else list(range(nhalf))
    head(order[0])
    for i, hf in enumerate(order):
        steps = pipeline(hf)
        cut = (2 * len(steps)) // 3
        for fn in steps[:cut]:
            fn()
        if i + 1 < nhalf:
            head(order[i + 1])
        for fn in steps[cut:]:
            fn()


def _s5_scan(h, modsa, gains, wd, wr, a_re, a_im, dskip, prev, *, bsz, nct, nlt, reverse):
    _, ls, d = h.shape
    ts = S5_TSTEPS
    nhalf = S5_SUBTILES
    ntot = nct + nlt
    nchunk2, lanes = a_re.shape[0] * 2, a_re.shape[2]

    def tmap(s):
        if reverse:
            return (0, jnp.where(s < nct, nct - 1 - s, ntot + nct - 1 - s), 0)
        return (0, s, 0)

    spec = pl.BlockSpec((bsz, nhalf * ts, d), tmap)
    full = _full_spec
    arrays = [h, modsa, gains, wd, wr, a_re, a_im, dskip]
    specs = [spec, full(modsa), full(gains), full(wd), full(wr), full(a_re), full(a_im), full(dskip)]
    aliases = {}
    if prev is not None:
        arrays.append(prev)
        specs.append(spec)
        aliases = {8: 0}
    return pl.pallas_call(
        functools.partial(_s5_kernel, reverse=reverse, accumulate=prev is not None, nct=nct, bsz=bsz),
        out_shape=jax.ShapeDtypeStruct(h.shape, f32),
        grid=(ntot,),
        in_specs=specs,
        out_specs=spec,
        scratch_shapes=[pltpu.VMEM((nhalf, nchunk2, bsz * ts, lanes), f32), pltpu.VMEM((nchunk2, bsz, lanes), f32),
                        pltpu.VMEM((nhalf, bsz * ts, d), f32), pltpu.VMEM((nhalf, bsz * ts, d), f32),
                        pltpu.VMEM((nhalf, bsz * ts, d), bf16)],
        input_output_aliases=aliases,
        compiler_params=_cparams(("arbitrary",)),
        name="s5_scan_bwd" if reverse else "s5_scan_fwd",
    )(*arrays)


def _s5_weights(b_re, b_im, cf_re, cf_im, a_re, a_im, lanes, bsz):
    g, n, c = b_re.shape
    gps = S5_SLAB // c
    nslab = g // gps
    eye = jnp.eye(gps, dtype=f32)

    def drive(bm):
        t = bm.astype(f32).reshape(nslab, gps, n, c).transpose(0, 1, 3, 2)
        t = t[:, :, :, None, :] * eye[None, :, None, :, None]
        return t.reshape(nslab, gps * c, gps * n)

    wd = jnp.concatenate([drive(b_re), drive(b_im)], axis=-1).astype(bf16)

    def read(cm):
        t = cm.reshape(nslab, gps, c, n).transpose(0, 1, 3, 2)
        t = t[:, :, :, None, :] * eye[None, :, None, :, None]
        return t.reshape(nslab, gps * n, gps * c)

    wr = jnp.stack([jnp.concatenate([read(cf_re[dd]), -read(cf_im[dd])], axis=1) for dd in range(2)]).astype(bf16)

    def decay(a):
        t = a.reshape(2, (g * n) // lanes, 1, lanes)
        return jnp.broadcast_to(t, (2, (g * n) // lanes, bsz, lanes))

    return wd, wr, decay(a_re), decay(a_im)


def _na_bias_table(rpb):
    nh = rpb.shape[0]
    qc = jnp.arange(GRID_W)[:, None]
    kc = jnp.arange(GRID_W)[None, :]
    dc = jnp.clip(kc - qc, 1 - NA_COLS, NA_COLS - 1) + (NA_COLS - 1)
    q_start = jnp.clip(qc - NA_COLS // 2, 0, GRID_W - NA_COLS)
    in_win = (kc >= q_start) & (kc < q_start + NA_COLS)
    tab = rpb.astype(f32)[:, :, dc]
    tab = jnp.where(in_win[None, None], tab, NEG_BIG)
    tab = tab.reshape(nh // 4, 4, 2 * NA_ROWS - 1, GRID_W, GRID_W)
    return tab.transpose(0, 2, 4, 1, 3).reshape(nh // 4, 2 * NA_ROWS - 1, GRID_W, 4 * GRID_W)


def _na_kernel(*refs, lc, nrows):
    q_ref, k_ref, v_ref, tab_ref, o_ref, s_scr, p_scr, m_scr = refs
    r0 = pl.program_id(1) * nrows
    rows_total = (k_ref.shape[0] - lc) // GRID_W
    nloc = NA_ROWS * GRID_W
    d = q_ref.shape[-1]
    dh = d // NA_HEADS
    npair = NA_HEADS // 2
    pw = 2 * dh
    qw = 2 * pw
    lane = lax.broadcasted_iota(jnp.int32, (GRID_W, pw), 1)
    lane4 = lax.broadcasted_iota(jnp.int32, (GRID_W, qw), 1) // dh
    kr0 = [jnp.clip(r0 + j - NA_ROWS // 2, 0, rows_total - NA_ROWS) for j in range(nrows)]
    k0 = [pl.multiple_of(lc + kr0[j] * GRID_W, GRID_W) for j in range(nrows)]
    dr0 = [kr0[j] - (r0 + j) + NA_ROWS - 1 for j in range(nrows)]
    for j in range(nrows):
        for pp in range(npair // 2):
            sl = slice(pp * qw, (pp + 1) * qw)
            q4 = q_ref[j * GRID_W:(j + 1) * GRID_W, sl]
            zero = jnp.zeros_like(q4)
            qm = jnp.concatenate([jnp.where(lane4 == i, q4, zero) for i in range(4)], axis=0)
            bias = jnp.concatenate([tab_ref[pp, dr0[j] + i] for i in range(NA_ROWS)], axis=0)
            s_loc = _dot_nt(k_ref[pl.ds(k0[j], nloc), sl], qm) + bias
            s_ctx = _dot_nt(k_ref[0:lc, sl], qm)
            s_scr[j, pp, 0:nloc, :] = s_loc
            s_scr[j, pp, nloc:nloc + lc, :] = s_ctx
            m_scr[j, pp] = jnp.maximum(jnp.max(s_loc, axis=0, keepdims=True), jnp.max(s_ctx, axis=0, keepdims=True))
    for j in range(nrows):
        for pp in range(npair // 2):
            p_scr[j, pp] = jnp.exp(s_scr[j, pp] - m_scr[j, pp]).astype(bf16)
    ones_loc = jnp.ones((nloc, pw), bf16)
    ones_ctx = jnp.ones((lc, pw), bf16)
    for j in range(nrows):
        for p in range(npair):
            sl = slice(p * pw, (p + 1) * pw)
            ps = slice((p % 2) * pw, (p % 2 + 1) * pw)
            v_loc = jnp.concatenate([v_ref[pl.ds(k0[j], nloc), sl], ones_loc], axis=1)
            v_ctx = jnp.concatenate([v_ref[0:lc, sl], ones_ctx], axis=1)
            oa = _dot_tn(p_scr[j, p // 2, 0:nloc, ps], v_loc) + _dot_tn(p_scr[j, p // 2, nloc:nloc + lc, ps], v_ctx)
            o = oa[:, 0:pw] / oa[:, pw:qw]
            o_ref[j * GRID_W:(j + 1) * GRID_W, sl] = jnp.where(lane < dh, o[0:GRID_W], o[GRID_W:pw]).astype(o_ref.dtype)


def _na_core(q, k, v, bias, *, bsz, ls, lc, nrows):
    d = q.shape[1]
    nrow = (ls - lc) // GRID_W
    nq = nrows * GRID_W
    nkeys = NA_ROWS * GRID_W + lc
    kv_spec = pl.BlockSpec((ls, d), lambda b, rp: (b, 0))
    return pl.pallas_call(
        functools.partial(_na_kernel, lc=lc, nrows=nrows),
        out_shape=jax.ShapeDtypeStruct((bsz * nrow * GRID_W, d), bf16),
        grid=(bsz, nrow // nrows),
        in_specs=[pl.BlockSpec((nq, d), lambda b, rp: (b * (ls // nq) + lc // nq + rp, 0)), kv_spec, kv_spec,
                  _full_spec(bias)],
        out_specs=pl.BlockSpec((nq, d), lambda b, rp: (b * (nrow // nrows) + rp, 0)),
        scratch_shapes=[pltpu.VMEM((nrows, NA_HEADS // 4, nkeys, 4 * GRID_W), f32),
                        pltpu.VMEM((nrows, NA_HEADS // 4, nkeys, 4 * GRID_W), bf16),
                        pltpu.VMEM((nrows, NA_HEADS // 4, 1, 4 * GRID_W), f32)],
        compiler_params=_cparams(("parallel", "arbitrary")),
        name="na_core",
    )(q, k, v, bias)


def kernel(x, c, ctx, c_ctx, ada_w, ada_b, norm_gains, mlp_w_in, mlp_w_out, sc_w_in, sc_conv, sc_w_out, hg_w_in, hg_lower_bound, hg_norm, hg_w_out, s5_lam_re, s5_lam_im, s5_log_dt, s5_b_re, s5_b_im, s5_c_re, s5_c_im, s5_d, s5_w_glu, na_w_qkv, na_rpb, na_w_out):
    bsz, seq, d = x.shape
    lc = ctx.shape[1]
    depth = ada_w.shape[0]
    tile = SUB
    assert depth == 4 and lc == tile and seq % tile == 0 and seq % GRID_W == 0 and bsz <= MOD_ROWS - 8
    assert bsz % 8 == 0 and tile % S5_TSTEPS == 0
    ls = lc + seq
    nt = ls // tile
    nsub = 2
    assert (bsz * nt) % nsub == 0 and (nt - 1) % nsub == 0 and SUB == tile

    cc = jnp.zeros((MOD_ROWS, d), f32).at[:bsz].set(c.astype(f32)).at[MOD_ROWS - 8].set(c_ctx.astype(f32))
    mods_a = _mods(cc, ada_w, ada_b).transpose(0, 2, 1, 3)
    lb_all, a_re, a_im, cf_re, cf_im = _prep(hg_lower_bound, s5_lam_re[0], s5_lam_im[0], s5_log_dt[0],
                                             s5_c_re[0], s5_c_im[0])
    gains = norm_gains.astype(f32)
    w1 = mlp_w_in.astype(bf16)
    w2 = mlp_w_out.astype(bf16)
    xl = x.reshape(bsz * seq, d)
    xc = ctx.reshape(bsz * lc, d)

    bg, u = _conv_proj(xc, xl, mods_a[0], gains[0], sc_w_in[0].astype(bf16), bsz=bsz, nt=nt, nsub=nsub)
    h = _conv_post(xc, xl, bg, u, sc_conv[0].astype(f32), mods_a[0], gains[0], sc_w_out[0].astype(bf16),
                   w1[0], w2[0], bsz=bsz, nt=nt, nsub=nsub)

    q, v, gate, f_fwd, f_bwd = _proj_flat(h, mods_a[1], gains[1], hg_w_in[0].astype(bf16),
                                          (bf16, bf16, bf16, f32, f32), bsz=bsz, nt=nt, nsub=nsub)
    lb1 = lb_all[1:2]
    o = _gla(q, v, f_fwd, lb1, None, bsz=bsz, nt=nt, reverse=False)
    o = _gla(q, v, f_bwd, lb1, o, bsz=bsz, nt=nt, reverse=True)
    h = _post_mlp("hg", h, [o, gate], [hg_norm[0].astype(f32).reshape(1, d), hg_w_out[0].astype(bf16)],
                  mods_a[1], gains[1], w1[1], w2[1], bsz=bsz, nt=nt, nsub=nsub)

    lanes = 256
    wd, wr, ar, ai = _s5_weights(s5_b_re[0], s5_b_im[0], cf_re, cf_im, a_re, a_im, lanes, bsz)
    h3 = h.reshape(bsz, ls, d)
    s5_tile = S5_SUBTILES * S5_TSTEPS
    assert lc % s5_tile == 0 and seq % s5_tile == 0
    nct, nlt = lc // s5_tile, seq // s5_tile
    dsk = s5_d[0].astype(f32).reshape(1, d)
    y = _s5_scan(h3, mods_a[2], gains[2], wd, wr[0], ar[0], ai[0], dsk, None, bsz=bsz, nct=nct, nlt=nlt, reverse=False)
    y = _s5_scan(h3, mods_a[2], gains[2], wd, wr[1], ar[1], ai[1], dsk, y, bsz=bsz, nct=nct, nlt=nlt, reverse=True)
    h = _post_mlp("s5", h, [y.reshape(bsz * ls, d)], [s5_w_glu[0].astype(bf16)],
                  mods_a[2], gains[2], w1[2], w2[2], bsz=bsz, nt=nt, nsub=nsub)

    dh = d // NA_HEADS
    qq, kk, vv = _proj_flat(h, mods_a[3], gains[3], na_w_qkv[0].astype(bf16), (bf16, bf16, bf16), bsz=bsz, nt=nt,
                            nsub=nsub, scales=(dh ** -0.5, 1.0, 1.0))
    bias = _na_bias_table(na_rpb[0])
    o = _na_core(qq, kk, vv, bias, bsz=bsz, ls=ls, lc=lc, nrows=2)
    hl = _post_mlp("na", h, [o], [na_w_out[0].astype(bf16)], mods_a[3], gains[3], w1[3], w2[3],
                   bsz=bsz, nt=nt, nsub=nsub, lat_only=True)
    return hl.reshape(bsz, seq, d)
```

```python
import functools

import jax
import jax.numpy as jnp
from jax import lax
from jax.experimental import pallas as pl
from jax.experimental.pallas import tpu as pltpu

EPS = 1e-6
N_MOD = 6
MOD_ROWS = 24
HG_HEAD_DIM = 128
HG_CHUNK = 32
S5_SLAB = 256
S5_TSTEPS = 16
S5_SUBTILES = 2
GRID_W = 64
NA_HEADS = 16
NA_ROWS = 8
NA_COLS = 16
NA_GROUP = 4
NEG_BIG = -1e30
VMEM_LIMIT = 56 * 1024 * 1024

bf16 = jnp.bfloat16
f32 = jnp.float32


def _cparams(sem):
    return pltpu.CompilerParams(dimension_semantics=sem, vmem_limit_bytes=VMEM_LIMIT)


def _rms(x, g):
    return x * lax.rsqrt(jnp.mean(x * x, axis=-1, keepdims=True) + EPS) * g


def _dot(a, b):
    return jnp.dot(a, b, preferred_element_type=f32)


def _dot_nt(a, b):
    return lax.dot_general(a, b, (((1,), (1,)), ((), ())), preferred_element_type=f32)


def _dot_tn(a, b):
    return lax.dot_general(a, b, (((0,), (0,)), ((), ())), preferred_element_type=f32)


def _mods_kernel(cc_ref, w_ref, b_ref, o_ref):
    x = cc_ref[...]
    a = (x * jax.nn.sigmoid(x)).astype(bf16)
    o_ref[...] = _dot(a, w_ref[...].astype(bf16)) + b_ref[...]


def _mods(cc, ada_w, ada_b):
    depth, d, _ = ada_w.shape
    out = pl.pallas_call(
        _mods_kernel,
        out_shape=jax.ShapeDtypeStruct((depth, N_MOD, MOD_ROWS, d), f32),
        grid=(depth, N_MOD),
        in_specs=[pl.BlockSpec((MOD_ROWS, d), lambda i, j: (0, 0)),
                  pl.BlockSpec((None, d, d), lambda i, j: (i, 0, j)),
                  pl.BlockSpec((None, 1, d), lambda i, j: (i * N_MOD + j, 0, 0))],
        out_specs=pl.BlockSpec((None, None, MOD_ROWS, d), lambda i, j: (i, j, 0, 0)),
        compiler_params=_cparams(("arbitrary", "arbitrary")),
        name="ada_mods",
    )(cc, ada_w, ada_b.reshape(depth * N_MOD, 1, d))
    return out


def _prep_kernel(lbp_ref, lre_ref, lim_ref, ldt_ref, cre_ref, cim_ref,
                 lb_ref, are_ref, aim_ref, cfre_ref, cfim_ref):
    x = lbp_ref[...]
    rows = [x[i:i + 1, :] for i in range(x.shape[0])]
    m = functools.reduce(jnp.maximum, rows)
    es = [jnp.exp(r - m) for r in rows]
    tot = functools.reduce(lambda a, b: a + b, es)
    acc = None
    first = None
    for i, e in enumerate(es):
        sm = e / tot
        acc = sm if acc is None else acc + sm
        if first is None:
            first = acc
        lb_ref[i:i + 1, :] = acc - first
    lam_re = jnp.minimum(lre_ref[...], -1e-4)
    lam_im = lim_ref[...]
    dt = jnp.exp(ldt_ref[...])
    mag = jnp.exp(lam_re * dt)
    a_re = mag * jnp.cos(lam_im * dt)
    a_im = mag * jnp.sin(lam_im * dt)
    den = lam_re * lam_re + lam_im * lam_im
    f_re = ((a_re - 1) * lam_re + a_im * lam_im) / den
    f_im = (a_im * lam_re - (a_re - 1) * lam_im) / den
    are_ref[...] = a_re
    aim_ref[...] = a_im
    c_re, c_im = cre_ref[...], cim_ref[...]
    cfre_ref[...] = c_re * f_re - c_im * f_im
    cfim_ref[...] = c_re * f_im + c_im * f_re


def _prep(hg_lower_bound, lam_re, lam_im, log_dt, c_re, c_im):
    shape = c_re.shape
    flat = (shape[0] * shape[1] * shape[2], shape[3])

    def expand(t):
        return jnp.broadcast_to(t[:, :, None, :], shape).reshape(flat)

    ldt = jnp.broadcast_to(log_dt[:, :, None, None], shape).reshape(flat)
    outs = pl.pallas_call(
        _prep_kernel,
        out_shape=[jax.ShapeDtypeStruct(hg_lower_bound.shape, f32)] + [jax.ShapeDtypeStruct(flat, f32)] * 4,
        name="param_prep",
    )(hg_lower_bound.astype(f32), expand(lam_re.astype(f32)), expand(lam_im.astype(f32)), ldt.astype(f32),
      c_re.astype(f32).reshape(flat), c_im.astype(f32).reshape(flat))
    lb, a_re, a_im, cf_re, cf_im = outs
    a_re = a_re.reshape(shape)[:, :, 0, :]
    a_im = a_im.reshape(shape)[:, :, 0, :]
    return lb, a_re, a_im, cf_re.reshape(shape), cf_im.reshape(shape)


def _mod_row(b, t):
    return jnp.where(t == 0, MOD_ROWS - 8, b)


def _full_spec(arr):
    nd = arr.ndim
    return pl.BlockSpec(arr.shape, lambda *_: (0,) * nd, pipeline_mode=pl.Buffered(1))


SUB = 256


def _sub(j):
    return slice(j * SUB, (j + 1) * SUB)


def _flat_specs(nsub, d, nt, lat_only):
    def where(i, j):
        g = i * nsub + j
        if lat_only:
            b, t = g // (nt - 1), g % (nt - 1) + 1
        else:
            b, t = g // nt, g % nt
        return b, t

    h_specs = [pl.BlockSpec((SUB, d), lambda i, j=j: (where(i, j)[0] * nt + where(i, j)[1], 0)) for j in range(nsub)]
    m_specs = [pl.BlockSpec((None, N_MOD, d), lambda i, j=j: (_mod_row(*where(i, j)), 0, 0)) for j in range(nsub)]
    return h_specs, m_specs


def _rows_spec(nsub, d):
    return pl.BlockSpec((nsub * SUB, d), lambda i: (i, 0))


def _prenorm(h, mods_ref, gains_ref, which):
    g = gains_ref[2 * which:2 * which + 1, :]
    shift = mods_ref[3 * which:3 * which + 1, :]
    scale = mods_ref[3 * which + 1:3 * which + 2, :]
    return _rms(h, g) * (1 + scale) + shift


def _residual(h, y, mods_ref, gains_ref, which):
    g = gains_ref[2 * which + 1:2 * which + 2, :]
    gate = mods_ref[3 * which + 2:3 * which + 3, :]
    return h + gate * _rms(y, g)


def _first_layer_h(t, hc_ref, hl_ref):
    return jnp.where(t == 0, hc_ref[...], hl_ref[...])


def _proj_flat_kernel(*refs, nsub, scales):
    h_refs, mods_refs = refs[:nsub], refs[nsub:2 * nsub]
    gains_ref, w_ref = refs[2 * nsub:2 * nsub + 2]
    outs, a_s = refs[2 * nsub + 2:-1], refs[-1]
    d = a_s.shape[-1]
    for j in range(nsub):
        a_s[_sub(j), :] = _prenorm(h_refs[j][...], mods_refs[j], gains_ref, 0).astype(bf16)
    for k, (o_ref, sc) in enumerate(zip(outs, scales)):
        piece = _dot(a_s[...], w_ref[:, k * d:(k + 1) * d])
        if sc != 1.0:
            piece = piece * sc
        o_ref[...] = piece.astype(o_ref.dtype)


def _proj_flat(h, mods, gains, w, out_dtypes, *, bsz, nt, nsub, scales=None):
    d = w.shape[0]
    rows = bsz * nt * SUB
    h_specs, m_specs = _flat_specs(nsub, d, nt, False)
    scales = tuple(scales or (1.0,) * len(out_dtypes))
    return pl.pallas_call(
        functools.partial(_proj_flat_kernel, nsub=nsub, scales=scales),
        out_shape=[jax.ShapeDtypeStruct((rows, d), dt) for dt in out_dtypes],
        grid=(rows // (nsub * SUB),),
        in_specs=h_specs + m_specs + [_full_spec(gains), _full_spec(w)],
        out_specs=[_rows_spec(nsub, d) for _ in out_dtypes],
        scratch_shapes=[pltpu.VMEM((nsub * SUB, d), bf16)],
        compiler_params=_cparams(("arbitrary",)),
        name="prenorm_proj",
    )(*([h] * nsub + [mods] * nsub + [gains, w]))


def _mlp_chunks(a, w1_ref, w2_ref, ff_chunk):
    nchunk = w1_ref.shape[1] // ff_chunk

    def hidden(c):
        hid = jnp.maximum(_dot(a, w1_ref[:, c * ff_chunk:(c + 1) * ff_chunk]), 0.0)
        return (hid * hid).astype(bf16)

    acc = None
    nxt = hidden(0)
    for c in range(nchunk):
        cur = nxt
        if c + 1 < nchunk:
            nxt = hidden(c + 1)
        part = _dot(cur, w2_ref[c * ff_chunk:(c + 1) * ff_chunk, :])
        acc = part if acc is None else acc + part
    return acc


def _post_mlp_kernel(*refs, kind, nsub, ff_chunk):
    h_refs, mods_refs, gains_ref = refs[:nsub], refs[nsub:2 * nsub], refs[2 * nsub]
    ins, (w1_ref, w2_ref, out_ref, h1_s, a_s) = refs[2 * nsub + 1:-5], refs[-5:]
    d = out_ref.shape[-1]
    if kind == "hg":
        o_in, gate_ref, gn_ref, wo_ref = ins
        o, gn = o_in[...], gn_ref[...]
        on = jnp.concatenate([_rms(o[:, k * HG_HEAD_DIM:(k + 1) * HG_HEAD_DIM], gn[:, k * HG_HEAD_DIM:(k + 1) * HG_HEAD_DIM])
                              for k in range(d // HG_HEAD_DIM)], axis=-1)
        g = gate_ref[...].astype(f32)
        y = _dot((on * (g * jax.nn.sigmoid(g))).astype(bf16), wo_ref[...])
    elif kind == "s5":
        y_in, wg_ref = ins
        r = _dot(jax.nn.gelu(y_in[...]).astype(bf16), wg_ref[...])
        y = r[:, :d] * jax.nn.sigmoid(r[:, d:])
    else:
        o_in, wo_ref = ins
        y = _dot(o_in[...], wo_ref[...])
    for j in range(nsub):
        h1 = _residual(h_refs[j][...], y[_sub(j)], mods_refs[j], gains_ref, 0)
        h1_s[_sub(j), :] = h1
        a_s[_sub(j), :] = _prenorm(h1, mods_refs[j], gains_ref, 1).astype(bf16)
    acc = _mlp_chunks(a_s[...], w1_ref, w2_ref, ff_chunk)
    for j in range(nsub):
        out_ref[_sub(j), :] = _residual(h1_s[_sub(j), :], acc[_sub(j)], mods_refs[j], gains_ref, 1)


def _post_mlp(kind, h, ins, consts, mods, gains, w1, w2, *, bsz, nt, nsub, lat_only=False):
    d = w1.shape[0]
    rows = bsz * (nt - 1 if lat_only else nt) * SUB
    h_specs, m_specs = _flat_specs(nsub, d, nt, lat_only)
    return pl.pallas_call(
        functools.partial(_post_mlp_kernel, kind=kind, nsub=nsub, ff_chunk=min(1024, w1.shape[1])),
        out_shape=jax.ShapeDtypeStruct((rows, d), f32),
        grid=(rows // (nsub * SUB),),
        in_specs=(h_specs + m_specs + [_full_spec(gains)] + [_rows_spec(nsub, d) for _ in ins]
                  + [_full_spec(c_) for c_ in consts] + [_full_spec(w1), _full_spec(w2)]),
        out_specs=_rows_spec(nsub, d),
        scratch_shapes=[pltpu.VMEM((nsub * SUB, d), f32), pltpu.VMEM((nsub * SUB, d), bf16)],
        compiler_params=_cparams(("arbitrary",)),
        name=kind + "_out_mlp",
    )(*([h] * nsub + [mods] * nsub + [gains] + list(ins) + list(consts) + [w1, w2]))


def _first_specs(nsub, d, nt):
    def where(i, j):
        g = i * nsub + j
        return g // nt, g % nt

    c_specs = [pl.BlockSpec((SUB, d), lambda i, j=j: (where(i, j)[0], 0)) for j in range(nsub)]
    l_specs = [pl.BlockSpec((SUB, d), lambda i, j=j: (where(i, j)[0] * (nt - 1) + jnp.maximum(where(i, j)[1] - 1, 0), 0))
               for j in range(nsub)]
    m_specs = [pl.BlockSpec((None, N_MOD, d), lambda i, j=j: (_mod_row(*where(i, j)), 0, 0)) for j in range(nsub)]
    return c_specs, l_specs, m_specs


def _sub_t(nsub, nt, j):
    return (pl.program_id(0) * nsub + j) % nt


def _conv_proj_kernel(*refs, nsub, nt):
    hc_refs, hl_refs, mods_refs = refs[:nsub], refs[nsub:2 * nsub], refs[2 * nsub:3 * nsub]
    gains_ref, w_ref, bg_ref, u_ref, a_s = refs[3 * nsub:]
    d = a_s.shape[-1]
    for j in range(nsub):
        h = _first_layer_h(_sub_t(nsub, nt, j), hc_refs[j], hl_refs[j])
        a_s[_sub(j), :] = _prenorm(h, mods_refs[j], gains_ref, 0).astype(bf16)
    a = a_s[...]
    bg_ref[...] = _dot(a, w_ref[:, 0:d]).astype(bg_ref.dtype)
    u_ref[...] = (_dot(a, w_ref[:, d:2 * d]) * _dot(a, w_ref[:, 2 * d:3 * d])).astype(u_ref.dtype)


def _conv_proj(hc, hl, mods, gains, w, *, bsz, nt, nsub):
    d = w.shape[0]
    rows = bsz * nt * SUB
    c_specs, l_specs, m_specs = _first_specs(nsub, d, nt)
    return pl.pallas_call(
        functools.partial(_conv_proj_kernel, nsub=nsub, nt=nt),
        out_shape=[jax.ShapeDtypeStruct((rows, d), bf16)] * 2,
        grid=(rows // (nsub * SUB),),
        in_specs=c_specs + l_specs + m_specs + [_full_spec(gains), _full_spec(w)],
        out_specs=[_rows_spec(nsub, d)] * 2,
        scratch_shapes=[pltpu.VMEM((nsub * SUB, d), bf16)],
        compiler_params=_cparams(("arbitrary",)),
        name="conv_proj",
    )(*([hc] * nsub + [hl] * nsub + [mods] * nsub + [gains, w]))


def _conv_post_kernel(*refs, nsub, nt, ff_chunk):
    hc_refs, hl_refs, mods_refs = refs[:nsub], refs[nsub:2 * nsub], refs[2 * nsub:3 * nsub]
    (gains_ref, bg_ref, u_ref, up_ref, un_ref, cw_ref, wo_ref, w1_ref, w2_ref, o_ref, h1_s, a_s) = refs[3 * nsub:]
    u = u_ref[...].astype(f32)
    rows = u.shape[0]
    hal = up_ref.shape[0]
    ts = [_sub_t(nsub, nt, j) for j in range(nsub)]
    starts = [t <= 1 for t in ts]
    ends = [(t == 0) | (t == nt - 1) for t in ts]
    ridx = lax.broadcasted_iota(jnp.int32, u.shape, 0)
    u_prev = jnp.where(ridx == 0, up_ref[hal - 1:hal, :].astype(f32), pltpu.roll(u, 1, 0))
    u_next = jnp.where(ridx == rows - 1, un_ref[0:1, :].astype(f32), pltpu.roll(u, rows - 1, 0))
    for j in range(nsub):
        u_prev = jnp.where((ridx == j * SUB) & starts[j], 0.0, u_prev)
        u_next = jnp.where((ridx == (j + 1) * SUB - 1) & ends[j], 0.0, u_next)
    cw = cw_ref[...]
    conv = cw[0:1, :] * u_prev + cw[1:2, :] * u + cw[2:3, :] * u_next
    y = _dot((bg_ref[...].astype(f32) * conv).astype(bf16), wo_ref[...])
    for j in range(nsub):
        h = _first_layer_h(ts[j], hc_refs[j], hl_refs[j])
        h1 = _residual(h, y[_sub(j)], mods_refs[j], gains_ref, 0)
        h1_s[_sub(j), :] = h1
        a_s[_sub(j), :] = _prenorm(h1, mods_refs[j], gains_ref, 1).astype(bf16)
    acc = _mlp_chunks(a_s[...], w1_ref, w2_ref, ff_chunk)
    for j in range(nsub):
        o_ref[_sub(j), :] = _residual(h1_s[_sub(j), :], acc[_sub(j)], mods_refs[j], gains_ref, 1)


def _conv_post(hc, hl, bg, u, conv_w, mods, gains, wo, w1, w2, *, bsz, nt, nsub):
    d = wo.shape[0]
    tile = nsub * SUB
    hal = 16
    per = tile // hal
    rows = bsz * nt * SUB
    nblk = rows // hal
    c_specs, l_specs, m_specs = _first_specs(nsub, d, nt)
    return pl.pallas_call(
        functools.partial(_conv_post_kernel, nsub=nsub, nt=nt, ff_chunk=min(1024, w1.shape[1])),
        out_shape=jax.ShapeDtypeStruct((rows, d), f32),
        grid=(rows // tile,),
        in_specs=c_specs + l_specs + m_specs + [
            _full_spec(gains), _rows_spec(nsub, d), _rows_spec(nsub, d),
            pl.BlockSpec((hal, d), lambda i: (jnp.maximum(i * per - 1, 0), 0)),
            pl.BlockSpec((hal, d), lambda i: (jnp.minimum((i + 1) * per, nblk - 1), 0)),
            _full_spec(conv_w), _full_spec(wo), _full_spec(w1), _full_spec(w2)],
        out_specs=_rows_spec(nsub, d),
        scratch_shapes=[pltpu.VMEM((tile, d), f32), pltpu.VMEM((tile, d), bf16)],
        compiler_params=_cparams(("arbitrary",)),
        name="conv_out_mlp",
    )(*([hc] * nsub + [hl] * nsub + [mods] * nsub + [gains, bg, u, u, u, conv_w, wo, w1, w2]))


def _gla_kernel(*refs, reverse, accumulate):
    if accumulate:
        (q_ref, v_ref, f_ref, lb_ref, prev_ref, o_ref, st_ref,
         qin_s, kin_s, qout_s, kst_s, q2_s, k2_s, dec_s, kv_s, sc_s, sx_s) = refs
    else:
        (q_ref, v_ref, f_ref, lb_ref, o_ref, st_ref,
         qin_s, kin_s, qout_s, kst_s, q2_s, k2_s, dec_s, kv_s, sc_s, sx_s) = refs
        prev_ref = None
    tile, d = q_ref.shape
    nh = d // HG_HEAD_DIM
    c = HG_CHUNK
    nchunk = tile // c

    @pl.when(pl.program_id(1) == 0)
    def _():
        st_ref[...] = jnp.zeros_like(st_ref)

    ri = lax.broadcasted_iota(jnp.int32, (c, c), 0)
    ci = lax.broadcasted_iota(jnp.int32, (c, c), 1)
    causal = (ci >= ri) if reverse else (ci <= ri)
    tri = jnp.where(causal, 1.0, 0.0).astype(bf16)
    mid = c // 2 if reverse else c // 2 - 1
    last = 0 if reverse else c - 1
    lb = lb_ref[...]

    ngroup = nchunk // 2
    for g in range(ngroup):
        vals = {}
        for k in (2 * g, 2 * g + 1):
            rows = slice(k * c, (k + 1) * c)
            fg = lb + (1 - lb) * jax.nn.sigmoid(f_ref[rows, :])
            kk = 1 - fg
            lf = jnp.log(fg)
            hi = lf.astype(bf16)
            md = (lf - hi.astype(f32)).astype(bf16)
            bcum = _dot(tri, hi) + _dot(tri, md)
            b_mid = bcum[mid:mid + 1, :]
            b_last = bcum[last:last + 1, :]
            qq = q_ref[rows, :].astype(f32)
            qin_s[rows, :] = (qq * jnp.exp(bcum - b_mid)).astype(bf16)
            kin_s[rows, :] = (kk * jnp.exp(b_mid - bcum)).astype(bf16)
            vals[k] = (rows, qq * jnp.exp(bcum), kk * jnp.exp(b_last - bcum), jnp.exp(b_last))
        ka, kb = (2 * g + 1, 2 * g) if reverse else (2 * g, 2 * g + 1)
        rows_a, qo_a, ks_a, dec_a = vals[ka]
        rows_b, qo_b, ks_b, dec_b = vals[kb]
        qout_s[rows_b, :] = qo_b.astype(bf16)
        kst_s[rows_a, :] = ks_a.astype(bf16)
        q2_s[rows_a, :] = qo_a.astype(bf16)
        q2_s[rows_b, :] = (qo_b * dec_a).astype(bf16)
        k2_s[rows_a, :] = (ks_a * dec_b).astype(bf16)
        k2_s[rows_b, :] = ks_b.astype(bf16)
        dec_s[g:g + 1, :] = dec_a * dec_b

    def hs(h):
        return slice(h * HG_HEAD_DIM, (h + 1) * HG_HEAD_DIM)

    def cr(k):
        return slice(k * c, (k + 1) * c)

    def ab(g):
        return (2 * g + 1, 2 * g) if reverse else (2 * g, 2 * g + 1)

    units = [(k, h) for k in range(nchunk) for h in range(nh)]
    gunits = [(g, h) for g in range(ngroup) for h in range(nh)]
    for k, h in units:
        sc = _dot_nt(qin_s[cr(k), hs(h)], kin_s[cr(k), hs(h)])
        sc_s[k, h] = jnp.where(causal, sc, 0.0).astype(bf16)
    for g, h in gunits:
        ka, kb = ab(g)
        sx_s[g, h] = _dot_nt(qout_s[cr(kb), hs(h)], kst_s[cr(ka), hs(h)]).astype(bf16)
    for k, h in units:
        o_h = _dot(sc_s[k, h], v_ref[cr(k), hs(h)])
        ka, kb = ab(k // 2)
        if k == kb:
            o_h = o_h + _dot(sx_s[k // 2, h], v_ref[cr(ka), hs(h)])
        if prev_ref is not None:
            o_h = o_h + prev_ref[cr(k), hs(h)]
        o_ref[cr(k), hs(h)] = o_h
    for g, h in gunits:
        rows = slice(2 * g * c, (2 * g + 2) * c)
        kv_s[g, h] = _dot_tn(v_ref[rows, hs(h)], k2_s[rows, hs(h)])

    for h in range(nh):
        st = st_ref[h]
        for i in range(ngroup):
            g = (ngroup - 1 - i) if reverse else i
            rows = slice(2 * g * c, (2 * g + 2) * c)
            o_ref[rows, hs(h)] += _dot_nt(q2_s[rows, hs(h)], st.astype(bf16))
            st = st * dec_s[g:g + 1, hs(h)] + kv_s[g, h]
        st_ref[h] = st


def _gla(q, v, fraw, lb, prev, *, bsz, nt, reverse):
    d = q.shape[1]
    tile = SUB
    nh = d // HG_HEAD_DIM
    npair = tile // (2 * HG_CHUNK)

    def tmap(b, s):
        t = jnp.where(s == 0, 0, nt - s) if reverse else s
        return (b * nt + t, 0)

    spec = pl.BlockSpec((tile, d), tmap)
    arrays = [q, v, fraw, lb]
    specs = [spec, spec, spec, _full_spec(lb)]
    aliases = {}
    if prev is not None:
        arrays.append(prev)
        specs.append(spec)
        aliases = {4: 0}
    return pl.pallas_call(
        functools.partial(_gla_kernel, reverse=reverse, accumulate=prev is not None),
        out_shape=jax.ShapeDtypeStruct(q.shape, f32),
        grid=(bsz, nt),
        in_specs=specs,
        out_specs=spec,
        scratch_shapes=[pltpu.VMEM((nh, HG_HEAD_DIM, HG_HEAD_DIM), f32)]
        + [pltpu.VMEM((tile, d), bf16)] * 6
        + [pltpu.VMEM((npair, d), f32),
           pltpu.VMEM((npair, nh, HG_HEAD_DIM, HG_HEAD_DIM), f32),
           pltpu.VMEM((2 * npair, nh, HG_CHUNK, HG_CHUNK), bf16),
           pltpu.VMEM((npair, nh, HG_CHUNK, HG_CHUNK), bf16)],
        input_output_aliases=aliases,
        compiler_params=_cparams(("parallel", "arbitrary")),
        name="hgrn2_gla_bwd" if reverse else "hgrn2_gla_fwd",
    )(*arrays)


def _s5_kernel(*refs, reverse, accumulate, nct, bsz):
    if accumulate:
        (h_ref, modsa_ref, gains_ref, wd_ref, wr_ref, are_ref, aim_ref, dsk_ref, prev_ref,
         y_ref, z_ref, st_ref, ytb_ref, u_s, ub_s) = refs
    else:
        (h_ref, modsa_ref, gains_ref, wd_ref, wr_ref, are_ref, aim_ref, dsk_ref,
         y_ref, z_ref, st_ref, ytb_ref, u_s, ub_s) = refs
        prev_ref = None
    s = pl.program_id(0)
    d = h_ref.shape[-1]
    ts = S5_TSTEPS
    nhalf = h_ref.shape[1] // ts
    rows = bsz * ts
    nslab = d // S5_SLAB
    lanes = z_ref.shape[3]
    nchunk = z_ref.shape[1] // 2
    per_slab = nchunk // nslab

    @pl.when(s == 0)
    def _():
        st_ref[...] = jnp.zeros_like(st_ref)

    is_ctx = s < nct
    g = gains_ref[0:1, :]
    crow = slice(MOD_ROWS - 8, MOD_ROWS - 7)
    shift = jnp.where(is_ctx, modsa_ref[crow, 0:1, :], modsa_ref[0:bsz, 0:1, :])
    scale = jnp.where(is_ctx, modsa_ref[crow, 1:2, :], modsa_ref[0:bsz, 1:2, :])
    ro = lax.broadcasted_iota(jnp.int32, (rows, rows), 0)
    co = lax.broadcasted_iota(jnp.int32, (rows, rows), 1)
    to_tb = jnp.where((ro // bsz == co % ts) & (ro % bsz == co // ts), 1.0, 0.0).astype(bf16)
    to_bt = jnp.where((co // bsz == ro % ts) & (co % bsz == ro // ts), 1.0, 0.0).astype(bf16)

    def head(hf):
        h3 = h_ref[:, hf * ts:(hf + 1) * ts, :]
        u3 = h3 * lax.rsqrt(jnp.mean(h3 * h3, axis=-1, keepdims=True) + EPS) * g * (1 + scale) + shift
        u = u3.reshape(rows, d)
        u_s[hf] = u
        ub_s[hf] = _dot(to_tb, u.astype(bf16)).astype(bf16)

    def drive(hf, ck):
        kb, cc = divmod(ck, per_slab)
        ubk = ub_s[hf, :, kb * S5_SLAB:(kb + 1) * S5_SLAB]
        z_ref[hf, ck] = _dot(ubk, wd_ref[kb, :, cc * lanes:(cc + 1) * lanes])
        z_ref[hf, nchunk + ck] = _dot(ubk, wd_ref[kb, :, (per_slab + cc) * lanes:(per_slab + cc + 1) * lanes])

    def scan(hf, ck):
        ar = are_ref[ck]
        ai = aim_ref[ck]
        zr = st_ref[ck]
        zi = st_ref[nchunk + ck]
        for t in (range(ts - 1, -1, -1) if reverse else range(ts)):
            rs = slice(t * bsz, (t + 1) * bsz)
            nzr = ar * zr - ai * zi + z_ref[hf, ck, rs, :]
            nzi = ar * zi + ai * zr + z_ref[hf, nchunk + ck, rs, :]
            zr, zi = nzr, nzi
            z_ref[hf, ck, rs, :] = zr
            z_ref[hf, nchunk + ck, rs, :] = zi
        st_ref[ck] = zr
        st_ref[nchunk + ck] = zi

    def readout(hf, ck):
        kb, cc = divmod(ck, per_slab)
        p = (_dot(z_ref[hf, ck].astype(bf16), wr_ref[kb, cc * lanes:(cc + 1) * lanes, :])
             + _dot(z_ref[hf, nchunk + ck].astype(bf16), wr_ref[kb, (per_slab + cc) * lanes:(per_slab + cc + 1) * lanes, :]))
        sl = slice(kb * S5_SLAB, (kb + 1) * S5_SLAB)
        if cc == 0:
            ytb_ref[hf, :, sl] = p
        elif cc < per_slab - 1:
            ytb_ref[hf, :, sl] += p
        else:
            ytb = ytb_ref[hf, :, sl] + p
            hi = ytb.astype(bf16)
            lo = (ytb - hi.astype(f32)).astype(bf16)
            y = _dot(to_bt, hi) + _dot(to_bt, lo)
            tsl = slice(hf * ts, (hf + 1) * ts)
            if prev_ref is not None:
                y = y + prev_ref[:, tsl, sl].reshape(rows, S5_SLAB)
            else:
                y = y + dsk_ref[:, sl] * u_s[hf, :, sl]
            y_ref[:, tsl, sl] = y.reshape(bsz, ts, S5_SLAB)

    def pipeline(hf):
        lead = 2
        steps = [functools.partial(drive, hf, ck) for ck in range(lead)]
        for ck in range(nchunk):
            if ck + lead < nchunk:
                steps.append(functools.partial(drive, hf, ck + lead))
            steps.append(functools.partial(scan, hf, ck))
            if ck >= 1:
                steps.append(functools.partial(readout, hf, ck - 1))
        steps.append(functools.partial(readout, hf, nchunk - 1))
        return steps

    order = list(range(nhalf - 1, -1, -1)) if reverse else list(range(nhalf))
    head(order[0])
    for i, hf in enumerate(order):
        steps = pipeline(hf)
        cut = (2 * len(steps)) // 3
        for fn in steps[:cut]:
            fn()
        if i + 1 < nhalf:
            head(order[i + 1])
        for fn in steps[cut:]:
            fn()


def _s5_scan(h, modsa, gains, wd, wr, a_re, a_im, dskip, prev, *, bsz, nct, nlt, reverse):
    _, ls, d = h.shape
    ts = S5_TSTEPS
    nhalf = S5_SUBTILES
    ntot = nct + nlt
    nchunk2, lanes = a_re.shape[0] * 2, a_re.shape[2]

    def tmap(s):
        if reverse:
            return (0, jnp.where(s < nct, nct - 1 - s, ntot + nct - 1 - s), 0)
        return (0, s, 0)

    spec = pl.BlockSpec((bsz, nhalf * ts, d), tmap)
    full = _full_spec
    arrays = [h, modsa, gains, wd, wr, a_re, a_im, dskip]
    specs = [spec, full(modsa), full(gains), full(wd), full(wr), full(a_re), full(a_im), full(dskip)]
    aliases = {}
    if prev is not None:
        arrays.append(prev)
        specs.append(spec)
        aliases = {8: 0}
    return pl.pallas_call(
        functools.partial(_s5_kernel, reverse=reverse, accumulate=prev is not None, nct=nct, bsz=bsz),
        out_shape=jax.ShapeDtypeStruct(h.shape, f32),
        grid=(ntot,),
        in_specs=specs,
        out_specs=spec,
        scratch_shapes=[pltpu.VMEM((nhalf, nchunk2, bsz * ts, lanes), f32), pltpu.VMEM((nchunk2, bsz, lanes), f32),
                        pltpu.VMEM((nhalf, bsz * ts, d), f32), pltpu.VMEM((nhalf, bsz * ts, d), f32),
                        pltpu.VMEM((nhalf, bsz * ts, d), bf16)],
        input_output_aliases=aliases,
        compiler_params=_cparams(("arbitrary",)),
        name="s5_scan_bwd" if reverse else "s5_scan_fwd",
    )(*arrays)


def _s5_weights(b_re, b_im, cf_re, cf_im, a_re, a_im, lanes, bsz):
    g, n, c = b_re.shape
    gps = S5_SLAB // c
    nslab = g // gps
    eye = jnp.eye(gps, dtype=f32)

    def drive(bm):
        t = bm.astype(f32).reshape(nslab, gps, n, c).transpose(0, 1, 3, 2)
        t = t[:, :, :, None, :] * eye[None, :, None, :, None]
        return t.reshape(nslab, gps * c, gps * n)

    wd = jnp.concatenate([drive(b_re), drive(b_im)], axis=-1).astype(bf16)

    def read(cm):
        t = cm.reshape(nslab, gps, c, n).transpose(0, 1, 3, 2)
        t = t[:, :, :, None, :] * eye[None, :, None, :, None]
        return t.reshape(nslab, gps * n, gps * c)

    wr = jnp.stack([jnp.concatenate([read(cf_re[dd]), -read(cf_im[dd])], axis=1) for dd in range(2)]).astype(bf16)

    def decay(a):
        t = a.reshape(2, (g * n) // lanes, 1, lanes)
        return jnp.broadcast_to(t, (2, (g * n) // lanes, bsz, lanes))

    return wd, wr, decay(a_re), decay(a_im)


def _na_bias_table(rpb):
    nh = rpb.shape[0]
    qc = jnp.arange(GRID_W)[:, None]
    kc = jnp.arange(GRID_W)[None, :]
    dc = jnp.clip(kc - qc, 1 - NA_COLS, NA_COLS - 1) + (NA_COLS - 1)
    q_start = jnp.clip(qc - NA_COLS // 2, 0, GRID_W - NA_COLS)
    in_win = (kc >= q_start) & (kc < q_start + NA_COLS)
    tab = rpb.astype(f32)[:, :, dc]
    tab = jnp.where(in_win[None, None], tab, NEG_BIG)
    tab = tab.reshape(nh // 4, 4, 2 * NA_ROWS - 1, GRID_W, GRID_W)
    return tab.transpose(0, 2, 4, 1, 3).reshape(nh // 4, 2 * NA_ROWS - 1, GRID_W, 4 * GRID_W)


def _na_kernel(*refs, lc, nrows):
    q_ref, k_ref, v_ref, tab_ref, o_ref, s_scr, p_scr, m_scr = refs
    r0 = pl.program_id(1) * nrows
    rows_total = (k_ref.shape[0] - lc) // GRID_W
    nloc = NA_ROWS * GRID_W
    d = q_ref.shape[-1]
    dh = d // NA_HEADS
    npair = NA_HEADS // 2
    pw = 2 * dh
    qw = 2 * pw
    lane = lax.broadcasted_iota(jnp.int32, (GRID_W, pw), 1)
    lane4 = lax.broadcasted_iota(jnp.int32, (GRID_W, qw), 1) // dh
    kr0 = [jnp.clip(r0 + j - NA_ROWS // 2, 0, rows_total - NA_ROWS) for j in range(nrows)]
    k0 = [pl.multiple_of(lc + kr0[j] * GRID_W, GRID_W) for j in range(nrows)]
    dr0 = [kr0[j] - (r0 + j) + NA_ROWS - 1 for j in range(nrows)]
    ones_loc = jnp.ones((nloc, pw), bf16)
    ones_ctx = jnp.ones((lc, pw), bf16)
    ngrp = s_scr.shape[0]
    for g0 in range(0, nrows, ngrp):
        grp = range(g0, g0 + ngrp)
        for j in grp:
            for pp in range(npair // 2):
                sl = slice(pp * qw, (pp + 1) * qw)
                q4 = q_ref[j * GRID_W:(j + 1) * GRID_W, sl]
                zero = jnp.zeros_like(q4)
                qm = jnp.concatenate([jnp.where(lane4 == i, q4, zero) for i in range(4)], axis=0)
                bias = jnp.concatenate([tab_ref[pp, dr0[j] + i] for i in range(NA_ROWS)], axis=0)
                s_loc = _dot_nt(k_ref[pl.ds(k0[j], nloc), sl], qm) + bias
                s_ctx = _dot_nt(k_ref[0:lc, sl], qm)
                s_scr[j - g0, pp, 0:nloc, :] = s_loc
                s_scr[j - g0, pp, nloc:nloc + lc, :] = s_ctx
                m_scr[j - g0, pp] = jnp.maximum(jnp.max(s_loc, axis=0, keepdims=True),
                                                jnp.max(s_ctx, axis=0, keepdims=True))
        for j in grp:
            for pp in range(npair // 2):
                p_scr[j - g0, pp] = jnp.exp(s_scr[j - g0, pp] - m_scr[j - g0, pp]).astype(bf16)
        for j in grp:
            for p in range(npair):
                sl = slice(p * pw, (p + 1) * pw)
                ps = slice((p % 2) * pw, (p % 2 + 1) * pw)
                v_loc = jnp.concatenate([v_ref[pl.ds(k0[j], nloc), sl], ones_loc], axis=1)
                v_ctx = jnp.concatenate([v_ref[0:lc, sl], ones_ctx], axis=1)
                oa = (_dot_tn(p_scr[j - g0, p // 2, 0:nloc, ps], v_loc)
                      + _dot_tn(p_scr[j - g0, p // 2, nloc:nloc + lc, ps], v_ctx))
                o = oa[:, 0:pw] / oa[:, pw:qw]
                o_ref[j * GRID_W:(j + 1) * GRID_W, sl] = jnp.where(lane < dh, o[0:GRID_W], o[GRID_W:pw]).astype(o_ref.dtype)


def _na_core(q, k, v, bias, *, bsz, ls, lc, nrows):
    d = q.shape[1]
    nrow = (ls - lc) // GRID_W
    nq = nrows * GRID_W
    nkeys = NA_ROWS * GRID_W + lc
    kv_spec = pl.BlockSpec((ls, d), lambda b, rp: (b, 0), pipeline_mode=pl.Buffered(1))
    return pl.pallas_call(
        functools.partial(_na_kernel, lc=lc, nrows=nrows),
        out_shape=jax.ShapeDtypeStruct((bsz * nrow * GRID_W, d), bf16),
        grid=(bsz, nrow // nrows),
        in_specs=[pl.BlockSpec((nq, d), lambda b, rp: (b * (ls // nq) + lc // nq + rp, 0)), kv_spec, kv_spec,
                  _full_spec(bias)],
        out_specs=pl.BlockSpec((nq, d), lambda b, rp: (b * (nrow // nrows) + rp, 0)),
        scratch_shapes=[pltpu.VMEM((NA_GROUP, NA_HEADS // 4, nkeys, 4 * GRID_W), f32),
                        pltpu.VMEM((NA_GROUP, NA_HEADS // 4, nkeys, 4 * GRID_W), bf16),
                        pltpu.VMEM((NA_GROUP, NA_HEADS // 4, 1, 4 * GRID_W), f32)],
        compiler_params=_cparams(("parallel", "arbitrary")),
        name="na_core",
    )(q, k, v, bias)


def kernel(x, c, ctx, c_ctx, ada_w, ada_b, norm_gains, mlp_w_in, mlp_w_out, sc_w_in, sc_conv, sc_w_out, hg_w_in, hg_lower_bound, hg_norm, hg_w_out, s5_lam_re, s5_lam_im, s5_log_dt, s5_b_re, s5_b_im, s5_c_re, s5_c_im, s5_d, s5_w_glu, na_w_qkv, na_rpb, na_w_out):
    bsz, seq, d = x.shape
    lc = ctx.shape[1]
    depth = ada_w.shape[0]
    tile = SUB
    assert depth == 4 and lc == tile and seq % tile == 0 and seq % GRID_W == 0 and bsz <= MOD_ROWS - 8
    assert bsz % 8 == 0 and tile % S5_TSTEPS == 0
    ls = lc + seq
    nt = ls // tile
    nsub = 2
    assert (bsz * nt) % nsub == 0 and (nt - 1) % nsub == 0 and SUB == tile

    cc = jnp.zeros((MOD_ROWS, d), f32).at[:bsz].set(c.astype(f32)).at[MOD_ROWS - 8].set(c_ctx.astype(f32))
    mods_a = _mods(cc, ada_w, ada_b).transpose(0, 2, 1, 3)
    lb_all, a_re, a_im, cf_re, cf_im = _prep(hg_lower_bound, s5_lam_re[0], s5_lam_im[0], s5_log_dt[0],
                                             s5_c_re[0], s5_c_im[0])
    gains = norm_gains.astype(f32)
    w1 = mlp_w_in.astype(bf16)
    w2 = mlp_w_out.astype(bf16)
    xl = x.reshape(bsz * seq, d)
    xc = ctx.reshape(bsz * lc, d)

    bg, u = _conv_proj(xc, xl, mods_a[0], gains[0], sc_w_in[0].astype(bf16), bsz=bsz, nt=nt, nsub=nsub)
    h = _conv_post(xc, xl, bg, u, sc_conv[0].astype(f32), mods_a[0], gains[0], sc_w_out[0].astype(bf16),
                   w1[0], w2[0], bsz=bsz, nt=nt, nsub=nsub)

    q, v, gate, f_fwd, f_bwd = _proj_flat(h, mods_a[1], gains[1], hg_w_in[0].astype(bf16),
                                          (bf16, bf16, bf16, f32, f32), bsz=bsz, nt=nt, nsub=nsub)
    lb1 = lb_all[1:2]
    o = _gla(q, v, f_fwd, lb1, None, bsz=bsz, nt=nt, reverse=False)
    o = _gla(q, v, f_bwd, lb1, o, bsz=bsz, nt=nt, reverse=True)
    h = _post_mlp("hg", h, [o, gate], [hg_norm[0].astype(f32).reshape(1, d), hg_w_out[0].astype(bf16)],
                  mods_a[1], gains[1], w1[1], w2[1], bsz=bsz, nt=nt, nsub=nsub)

    lanes = 256
    wd, wr, ar, ai = _s5_weights(s5_b_re[0], s5_b_im[0], cf_re, cf_im, a_re, a_im, lanes, bsz)
    h3 = h.reshape(bsz, ls, d)
    s5_tile = S5_SUBTILES * S5_TSTEPS
    assert lc % s5_tile == 0 and seq % s5_tile == 0
    nct, nlt = lc // s5_tile, seq // s5_tile
    dsk = s5_d[0].astype(f32).reshape(1, d)
    y = _s5_scan(h3, mods_a[2], gains[2], wd, wr[0], ar[0], ai[0], dsk, None, bsz=bsz, nct=nct, nlt=nlt, reverse=False)
    y = _s5_scan(h3, mods_a[2], gains[2], wd, wr[1], ar[1], ai[1], dsk, y, bsz=bsz, nct=nct, nlt=nlt, reverse=True)
    h = _post_mlp("s5", h, [y.reshape(bsz * ls, d)], [s5_w_glu[0].astype(bf16)],
                  mods_a[2], gains[2], w1[2], w2[2], bsz=bsz, nt=nt, nsub=nsub)

    dh = d // NA_HEADS
    qq, kk, vv = _proj_flat(h, mods_a[3], gains[3], na_w_qkv[0].astype(bf16), (bf16, bf16, bf16), bsz=bsz, nt=nt,
                            nsub=nsub, scales=(dh ** -0.5, 1.0, 1.0))
    bias = _na_bias_table(na_rpb[0])
    o = _na_core(qq, kk, vv, bias, bsz=bsz, ls=ls, lc=lc, nrows=NA_GROUP)
    hl = _post_mlp("na", h, [o], [na_w_out[0].astype(bf16)], mods_a[3], gains[3], w1[3], w2[3],
                   bsz=bsz, nt=nt, nsub=nsub, lat_only=True)
    return hl.reshape(bsz, seq, d)
```

```python
import functools

import jax
import jax.numpy as jnp
from jax import lax
from jax.experimental import pallas as pl
from jax.experimental.pallas import tpu as pltpu

EPS = 1e-6
N_MOD = 6
FF_CHUNK = 1024
MOD_ROWS = 24
HG_HEAD_DIM = 128
HG_CHUNK = 32
HG_BATCH = 2
S5_SLAB = 256
S5_TSTEPS = 16
S5_SUBTILES = 2
GRID_W = 64
NA_HEADS = 16
NA_ROWS = 8
NA_COLS = 16
NA_GROUP = 4
NEG_BIG = -1e30
VMEM_LIMIT = 56 * 1024 * 1024

bf16 = jnp.bfloat16
f32 = jnp.float32


def _cparams(sem):
    return pltpu.CompilerParams(dimension_semantics=sem, vmem_limit_bytes=VMEM_LIMIT)


def _rms(x, g):
    return x * lax.rsqrt(jnp.mean(x * x, axis=-1, keepdims=True) + EPS) * g


def _dot(a, b):
    return jnp.dot(a, b, preferred_element_type=f32)


def _dot_nt(a, b):
    return lax.dot_general(a, b, (((1,), (1,)), ((), ())), preferred_element_type=f32)


def _dot_tn(a, b):
    return lax.dot_general(a, b, (((0,), (0,)), ((), ())), preferred_element_type=f32)


def _mods_kernel(cc_ref, w_ref, b_ref, o_ref):
    x = cc_ref[...]
    a = (x * jax.nn.sigmoid(x)).astype(bf16)
    o_ref[...] = _dot(a, w_ref[...].astype(bf16)) + b_ref[...]


def _mods(cc, ada_w, ada_b):
    depth, d, _ = ada_w.shape
    out = pl.pallas_call(
        _mods_kernel,
        out_shape=jax.ShapeDtypeStruct((depth, N_MOD, MOD_ROWS, d), f32),
        grid=(depth, N_MOD),
        in_specs=[pl.BlockSpec((MOD_ROWS, d), lambda i, j: (0, 0)),
                  pl.BlockSpec((None, d, d), lambda i, j: (i, 0, j)),
                  pl.BlockSpec((None, 1, d), lambda i, j: (i * N_MOD + j, 0, 0))],
        out_specs=pl.BlockSpec((None, None, MOD_ROWS, d), lambda i, j: (i, j, 0, 0)),
        compiler_params=_cparams(("arbitrary", "arbitrary")),
        name="ada_mods",
    )(cc, ada_w, ada_b.reshape(depth * N_MOD, 1, d))
    return out


def _prep_kernel(lbp_ref, lre_ref, lim_ref, ldt_ref, cre_ref, cim_ref,
                 lb_ref, are_ref, aim_ref, cfre_ref, cfim_ref):
    x = lbp_ref[...]
    rows = [x[i:i + 1, :] for i in range(x.shape[0])]
    m = functools.reduce(jnp.maximum, rows)
    es = [jnp.exp(r - m) for r in rows]
    tot = functools.reduce(lambda a, b: a + b, es)
    acc = None
    first = None
    for i, e in enumerate(es):
        sm = e / tot
        acc = sm if acc is None else acc + sm
        if first is None:
            first = acc
        lb_ref[i:i + 1, :] = acc - first
    lam_re = jnp.minimum(lre_ref[...], -1e-4)
    lam_im = lim_ref[...]
    dt = jnp.exp(ldt_ref[...])
    mag = jnp.exp(lam_re * dt)
    a_re = mag * jnp.cos(lam_im * dt)
    a_im = mag * jnp.sin(lam_im * dt)
    den = lam_re * lam_re + lam_im * lam_im
    f_re = ((a_re - 1) * lam_re + a_im * lam_im) / den
    f_im = (a_im * lam_re - (a_re - 1) * lam_im) / den
    are_ref[...] = a_re
    aim_ref[...] = a_im
    c_re, c_im = cre_ref[...], cim_ref[...]
    cfre_ref[...] = c_re * f_re - c_im * f_im
    cfim_ref[...] = c_re * f_im + c_im * f_re


def _prep(hg_lower_bound, lam_re, lam_im, log_dt, c_re, c_im):
    shape = c_re.shape
    flat = (shape[0] * shape[1] * shape[2], shape[3])

    def expand(t):
        return jnp.broadcast_to(t[:, :, None, :], shape).reshape(flat)

    ldt = jnp.broadcast_to(log_dt[:, :, None, None], shape).reshape(flat)
    outs = pl.pallas_call(
        _prep_kernel,
        out_shape=[jax.ShapeDtypeStruct(hg_lower_bound.shape, f32)] + [jax.ShapeDtypeStruct(flat, f32)] * 4,
        name="param_prep",
    )(hg_lower_bound.astype(f32), expand(lam_re.astype(f32)), expand(lam_im.astype(f32)), ldt.astype(f32),
      c_re.astype(f32).reshape(flat), c_im.astype(f32).reshape(flat))
    lb, a_re, a_im, cf_re, cf_im = outs
    a_re = a_re.reshape(shape)[:, :, 0, :]
    a_im = a_im.reshape(shape)[:, :, 0, :]
    return lb, a_re, a_im, cf_re.reshape(shape), cf_im.reshape(shape)


def _mod_row(b, t):
    return jnp.where(t == 0, MOD_ROWS - 8, b)


def _full_spec(arr):
    nd = arr.ndim
    return pl.BlockSpec(arr.shape, lambda *_: (0,) * nd, pipeline_mode=pl.Buffered(1))


def _layer_spec(arr, i):
    nd = arr.ndim
    return pl.BlockSpec((None,) + arr.shape[1:], lambda *_: (i,) + (0,) * (nd - 1), pipeline_mode=pl.Buffered(1))


SUB = 256


def _sub(j):
    return slice(j * SUB, (j + 1) * SUB)


def _flat_specs(nsub, d, nt, lat_only):
    def where(i, j):
        g = i * nsub + j
        if lat_only:
            b, t = g // (nt - 1), g % (nt - 1) + 1
        else:
            b, t = g // nt, g % nt
        return b, t

    h_specs = [pl.BlockSpec((SUB, d), lambda i, j=j: (where(i, j)[0] * nt + where(i, j)[1], 0)) for j in range(nsub)]
    m_specs = [pl.BlockSpec((None, N_MOD, d), lambda i, j=j: (_mod_row(*where(i, j)), 0, 0)) for j in range(nsub)]
    return h_specs, m_specs


def _rows_spec(nsub, d):
    return pl.BlockSpec((nsub * SUB, d), lambda i: (i, 0))


def _prenorm(h, mods_ref, gains_ref, which):
    g = gains_ref[2 * which:2 * which + 1, :]
    shift = mods_ref[3 * which:3 * which + 1, :]
    scale = mods_ref[3 * which + 1:3 * which + 2, :]
    return _rms(h, g) * (1 + scale) + shift


def _residual(h, y, mods_ref, gains_ref, which):
    g = gains_ref[2 * which + 1:2 * which + 2, :]
    gate = mods_ref[3 * which + 2:3 * which + 3, :]
    return h + gate * _rms(y, g)


def _first_layer_h(t, hc_ref, hl_ref):
    return jnp.where(t == 0, hc_ref[...], hl_ref[...])


def _proj_flat_kernel(*refs, nsub, scales):
    h_refs, mods_refs = refs[:nsub], refs[nsub:2 * nsub]
    gains_ref, w_ref = refs[2 * nsub:2 * nsub + 2]
    outs, a_s = refs[2 * nsub + 2:-1], refs[-1]
    d = a_s.shape[-1]
    for j in range(nsub):
        a_s[_sub(j), :] = _prenorm(h_refs[j][...], mods_refs[j], gains_ref, 0).astype(bf16)
    for k, (o_ref, sc) in enumerate(zip(outs, scales)):
        piece = _dot(a_s[...], w_ref[:, k * d:(k + 1) * d])
        if sc != 1.0:
            piece = piece * sc
        o_ref[...] = piece.astype(o_ref.dtype)


def _proj_flat(h, mods, gains, w, out_dtypes, *, bsz, nt, nsub, scales=None):
    d = w.shape[0]
    rows = bsz * nt * SUB
    h_specs, m_specs = _flat_specs(nsub, d, nt, False)
    scales = tuple(scales or (1.0,) * len(out_dtypes))
    return pl.pallas_call(
        functools.partial(_proj_flat_kernel, nsub=nsub, scales=scales),
        out_shape=[jax.ShapeDtypeStruct((rows, d), dt) for dt in out_dtypes],
        grid=(rows // (nsub * SUB),),
        in_specs=h_specs + m_specs + [_full_spec(gains), _full_spec(w)],
        out_specs=[_rows_spec(nsub, d) for _ in out_dtypes],
        scratch_shapes=[pltpu.VMEM((nsub * SUB, d), bf16)],
        compiler_params=_cparams(("arbitrary",)),
        name="prenorm_proj",
    )(*([h] * nsub + [mods] * nsub + [gains, w]))


def _mlp_chunks(a, w1_ref, w2_ref, ff_chunk):
    nchunk = w1_ref.shape[1] // ff_chunk

    def hidden(c):
        hid = jnp.maximum(_dot(a, w1_ref[:, c * ff_chunk:(c + 1) * ff_chunk]), 0.0)
        return (hid * hid).astype(bf16)

    acc = None
    nxt = hidden(0)
    for c in range(nchunk):
        cur = nxt
        if c + 1 < nchunk:
            nxt = hidden(c + 1)
        part = _dot(cur, w2_ref[c * ff_chunk:(c + 1) * ff_chunk, :])
        acc = part if acc is None else acc + part
    return acc


def _post_mlp_kernel(*refs, kind, nsub, ff_chunk):
    h_refs, mods_refs, gains_ref = refs[:nsub], refs[nsub:2 * nsub], refs[2 * nsub]
    ins, (w1_ref, w2_ref, out_ref, h1_s, a_s) = refs[2 * nsub + 1:-5], refs[-5:]
    d = out_ref.shape[-1]
    if kind == "hg":
        o_in, gate_ref, gn_ref, wo_ref = ins
        o, gn = o_in[...], gn_ref[...]
        on = jnp.concatenate([_rms(o[:, k * HG_HEAD_DIM:(k + 1) * HG_HEAD_DIM], gn[:, k * HG_HEAD_DIM:(k + 1) * HG_HEAD_DIM])
                              for k in range(d // HG_HEAD_DIM)], axis=-1)
        g = gate_ref[...].astype(f32)
        y = _dot((on * (g * jax.nn.sigmoid(g))).astype(bf16), wo_ref[...])
    elif kind == "s5":
        y_in, wg_ref = ins
        r = _dot(jax.nn.gelu(y_in[...]).astype(bf16), wg_ref[...])
        y = r[:, :d] * jax.nn.sigmoid(r[:, d:])
    else:
        o_in, wo_ref = ins
        y = _dot(o_in[...], wo_ref[...])
    for j in range(nsub):
        h1 = _residual(h_refs[j][...], y[_sub(j)], mods_refs[j], gains_ref, 0)
        h1_s[_sub(j), :] = h1
        a_s[_sub(j), :] = _prenorm(h1, mods_refs[j], gains_ref, 1).astype(bf16)
    acc = _mlp_chunks(a_s[...], w1_ref, w2_ref, ff_chunk)
    for j in range(nsub):
        out_ref[_sub(j), :] = _residual(h1_s[_sub(j), :], acc[_sub(j)], mods_refs[j], gains_ref, 1)


def _post_mlp(kind, h, ins, consts, mods, gains, w1, w2, layer, *, bsz, nt, nsub, lat_only=False):
    d = w1.shape[1]
    rows = bsz * (nt - 1 if lat_only else nt) * SUB
    h_specs, m_specs = _flat_specs(nsub, d, nt, lat_only)
    return pl.pallas_call(
        functools.partial(_post_mlp_kernel, kind=kind, nsub=nsub, ff_chunk=min(FF_CHUNK, w1.shape[-1])),
        out_shape=jax.ShapeDtypeStruct((rows, d), f32),
        grid=(rows // (nsub * SUB),),
        in_specs=(h_specs + m_specs + [_full_spec(gains)] + [_rows_spec(nsub, d) for _ in ins]
                  + [_full_spec(c_) for c_ in consts] + [_layer_spec(w1, layer), _layer_spec(w2, layer)]),
        out_specs=_rows_spec(nsub, d),
        scratch_shapes=[pltpu.VMEM((nsub * SUB, d), f32), pltpu.VMEM((nsub * SUB, d), bf16)],
        compiler_params=_cparams(("arbitrary",)),
        name=kind + "_out_mlp",
    )(*([h] * nsub + [mods] * nsub + [gains] + list(ins) + list(consts) + [w1, w2]))


def _first_specs(nsub, d, nt):
    def where(i, j):
        g = i * nsub + j
        return g // nt, g % nt

    c_specs = [pl.BlockSpec((SUB, d), lambda i, j=j: (where(i, j)[0], 0)) for j in range(nsub)]
    l_specs = [pl.BlockSpec((SUB, d), lambda i, j=j: (where(i, j)[0] * (nt - 1) + jnp.maximum(where(i, j)[1] - 1, 0), 0))
               for j in range(nsub)]
    m_specs = [pl.BlockSpec((None, N_MOD, d), lambda i, j=j: (_mod_row(*where(i, j)), 0, 0)) for j in range(nsub)]
    return c_specs, l_specs, m_specs


def _sub_t(nsub, nt, j):
    return (pl.program_id(0) * nsub + j) % nt


def _conv_proj_kernel(*refs, nsub, nt):
    hc_refs, hl_refs, mods_refs = refs[:nsub], refs[nsub:2 * nsub], refs[2 * nsub:3 * nsub]
    gains_ref, w_ref, bg_ref, u_ref, a_s = refs[3 * nsub:]
    d = a_s.shape[-1]
    for j in range(nsub):
        h = _first_layer_h(_sub_t(nsub, nt, j), hc_refs[j], hl_refs[j])
        a_s[_sub(j), :] = _prenorm(h, mods_refs[j], gains_ref, 0).astype(bf16)
    a = a_s[...]
    bg_ref[...] = _dot(a, w_ref[:, 0:d]).astype(bg_ref.dtype)
    u_ref[...] = (_dot(a, w_ref[:, d:2 * d]) * _dot(a, w_ref[:, 2 * d:3 * d])).astype(u_ref.dtype)


def _conv_proj(hc, hl, mods, gains, w, *, bsz, nt, nsub):
    d = w.shape[0]
    rows = bsz * nt * SUB
    c_specs, l_specs, m_specs = _first_specs(nsub, d, nt)
    return pl.pallas_call(
        functools.partial(_conv_proj_kernel, nsub=nsub, nt=nt),
        out_shape=[jax.ShapeDtypeStruct((rows, d), bf16)] * 2,
        grid=(rows // (nsub * SUB),),
        in_specs=c_specs + l_specs + m_specs + [_full_spec(gains), _full_spec(w)],
        out_specs=[_rows_spec(nsub, d)] * 2,
        scratch_shapes=[pltpu.VMEM((nsub * SUB, d), bf16)],
        compiler_params=_cparams(("arbitrary",)),
        name="conv_proj",
    )(*([hc] * nsub + [hl] * nsub + [mods] * nsub + [gains, w]))


def _conv_post_kernel(*refs, nsub, nt, ff_chunk):
    hc_refs, hl_refs, mods_refs = refs[:nsub], refs[nsub:2 * nsub], refs[2 * nsub:3 * nsub]
    (gains_ref, bg_ref, u_ref, up_ref, un_ref, cw_ref, wo_ref, w1_ref, w2_ref, o_ref, h1_s, a_s) = refs[3 * nsub:]
    u = u_ref[...].astype(f32)
    rows = u.shape[0]
    hal = up_ref.shape[0]
    ts = [_sub_t(nsub, nt, j) for j in range(nsub)]
    starts = [t <= 1 for t in ts]
    ends = [(t == 0) | (t == nt - 1) for t in ts]
    ridx = lax.broadcasted_iota(jnp.int32, u.shape, 0)
    u_prev = jnp.where(ridx == 0, up_ref[hal - 1:hal, :].astype(f32), pltpu.roll(u, 1, 0))
    u_next = jnp.where(ridx == rows - 1, un_ref[0:1, :].astype(f32), pltpu.roll(u, rows - 1, 0))
    for j in range(nsub):
        u_prev = jnp.where((ridx == j * SUB) & starts[j], 0.0, u_prev)
        u_next = jnp.where((ridx == (j + 1) * SUB - 1) & ends[j], 0.0, u_next)
    cw = cw_ref[...]
    conv = cw[0:1, :] * u_prev + cw[1:2, :] * u + cw[2:3, :] * u_next
    y = _dot((bg_ref[...].astype(f32) * conv).astype(bf16), wo_ref[...])
    for j in range(nsub):
        h = _first_layer_h(ts[j], hc_refs[j], hl_refs[j])
        h1 = _residual(h, y[_sub(j)], mods_refs[j], gains_ref, 0)
        h1_s[_sub(j), :] = h1
        a_s[_sub(j), :] = _prenorm(h1, mods_refs[j], gains_ref, 1).astype(bf16)
    acc = _mlp_chunks(a_s[...], w1_ref, w2_ref, ff_chunk)
    for j in range(nsub):
        o_ref[_sub(j), :] = _residual(h1_s[_sub(j), :], acc[_sub(j)], mods_refs[j], gains_ref, 1)


def _conv_post(hc, hl, bg, u, conv_w, mods, gains, wo, w1, w2, layer, *, bsz, nt, nsub):
    d = wo.shape[0]
    tile = nsub * SUB
    hal = 16
    per = tile // hal
    rows = bsz * nt * SUB
    nblk = rows // hal
    c_specs, l_specs, m_specs = _first_specs(nsub, d, nt)
    return pl.pallas_call(
        functools.partial(_conv_post_kernel, nsub=nsub, nt=nt, ff_chunk=min(FF_CHUNK, w1.shape[-1])),
        out_shape=jax.ShapeDtypeStruct((rows, d), f32),
        grid=(rows // tile,),
        in_specs=c_specs + l_specs + m_specs + [
            _full_spec(gains), _rows_spec(nsub, d), _rows_spec(nsub, d),
            pl.BlockSpec((hal, d), lambda i: (jnp.maximum(i * per - 1, 0), 0)),
            pl.BlockSpec((hal, d), lambda i: (jnp.minimum((i + 1) * per, nblk - 1), 0)),
            _full_spec(conv_w), _full_spec(wo), _layer_spec(w1, layer), _layer_spec(w2, layer)],
        out_specs=_rows_spec(nsub, d),
        scratch_shapes=[pltpu.VMEM((tile, d), f32), pltpu.VMEM((tile, d), bf16)],
        compiler_params=_cparams(("arbitrary",)),
        name="conv_out_mlp",
    )(*([hc] * nsub + [hl] * nsub + [mods] * nsub + [gains, bg, u, u, u, conv_w, wo, w1, w2]))


def _gla_kernel(*refs, reverse, accumulate):
    if accumulate:
        (q_ref, v_ref, f_ref, lb_ref, prev_ref, o_ref, st_ref,
         qin_s, kin_s, qout_s, kst_s, q2_s, k2_s, dec_s, kv_s, sc_s, sx_s) = refs
    else:
        (q_ref, v_ref, f_ref, lb_ref, o_ref, st_ref,
         qin_s, kin_s, qout_s, kst_s, q2_s, k2_s, dec_s, kv_s, sc_s, sx_s) = refs
        prev_ref = None
    nb, tile, d = q_ref.shape
    nh = d // HG_HEAD_DIM
    c = HG_CHUNK
    nchunk = tile // c

    @pl.when(pl.program_id(1) == 0)
    def _():
        st_ref[...] = jnp.zeros_like(st_ref)

    ri = lax.broadcasted_iota(jnp.int32, (c, c), 0)
    ci = lax.broadcasted_iota(jnp.int32, (c, c), 1)
    causal = (ci >= ri) if reverse else (ci <= ri)
    tri = jnp.where(causal, 1.0, 0.0).astype(bf16)
    mid = c // 2 if reverse else c // 2 - 1
    last = 0 if reverse else c - 1
    lb = lb_ref[...]

    ngroup = nchunk // 2
    for bi in range(nb):
        for g in range(ngroup):
            vals = {}
            for k in (2 * g, 2 * g + 1):
                rows = slice(k * c, (k + 1) * c)
                fg = lb + (1 - lb) * jax.nn.sigmoid(f_ref[bi, rows, :])
                kk = 1 - fg
                lf = jnp.log(fg)
                hi = lf.astype(bf16)
                md = (lf - hi.astype(f32)).astype(bf16)
                bcum = _dot(tri, hi) + _dot(tri, md)
                b_mid = bcum[mid:mid + 1, :]
                b_last = bcum[last:last + 1, :]
                q_mid = q_ref[bi, rows, :].astype(f32) * jnp.exp(bcum - b_mid)
                k_mid = kk * jnp.exp(b_mid - bcum)
                qin_s[bi, rows, :] = q_mid.astype(bf16)
                kin_s[bi, rows, :] = k_mid.astype(bf16)
                vals[k] = (rows, q_mid * jnp.exp(b_mid), k_mid * jnp.exp(b_last - b_mid), jnp.exp(b_last))
            ka, kb = (2 * g + 1, 2 * g) if reverse else (2 * g, 2 * g + 1)
            rows_a, qo_a, ks_a, dec_a = vals[ka]
            rows_b, qo_b, ks_b, dec_b = vals[kb]
            qout_s[bi, rows_b, :] = qo_b.astype(bf16)
            kst_s[bi, rows_a, :] = ks_a.astype(bf16)
            q2_s[bi, rows_a, :] = qo_a.astype(bf16)
            q2_s[bi, rows_b, :] = (qo_b * dec_a).astype(bf16)
            k2_s[bi, rows_a, :] = (ks_a * dec_b).astype(bf16)
            k2_s[bi, rows_b, :] = ks_b.astype(bf16)
            dec_s[bi, g:g + 1, :] = dec_a * dec_b

    def hs(h):
        return slice(h * HG_HEAD_DIM, (h + 1) * HG_HEAD_DIM)

    def cr(k):
        return slice(k * c, (k + 1) * c)

    def ab(g):
        return (2 * g + 1, 2 * g) if reverse else (2 * g, 2 * g + 1)

    units = [(bi, k, h) for bi in range(nb) for k in range(nchunk) for h in range(nh)]
    gunits = [(bi, g, h) for bi in range(nb) for g in range(ngroup) for h in range(nh)]
    for bi, k, h in units:
        sc = _dot_nt(qin_s[bi, cr(k), hs(h)], kin_s[bi, cr(k), hs(h)])
        sc_s[bi, k, h] = jnp.where(causal, sc, 0.0).astype(bf16)
    for bi, g, h in gunits:
        ka, kb = ab(g)
        sx_s[bi, g, h] = _dot_nt(qout_s[bi, cr(kb), hs(h)], kst_s[bi, cr(ka), hs(h)]).astype(bf16)
    for bi, k, h in units:
        o_h = _dot(sc_s[bi, k, h], v_ref[bi, cr(k), hs(h)])
        ka, kb = ab(k // 2)
        if k == kb:
            o_h = o_h + _dot(sx_s[bi, k // 2, h], v_ref[bi, cr(ka), hs(h)])
        if prev_ref is not None:
            o_h = o_h + prev_ref[bi, cr(k), hs(h)]
        o_ref[bi, cr(k), hs(h)] = o_h
    for bi, g, h in gunits:
        rows = slice(2 * g * c, (2 * g + 2) * c)
        kv_s[bi, g, h] = _dot_tn(v_ref[bi, rows, hs(h)], k2_s[bi, rows, hs(h)])

    for bi in range(nb):
        for h in range(nh):
            st = st_ref[bi, h]
            for i in range(ngroup):
                g = (ngroup - 1 - i) if reverse else i
                rows = slice(2 * g * c, (2 * g + 2) * c)
                o_ref[bi, rows, hs(h)] += _dot_nt(q2_s[bi, rows, hs(h)], st.astype(bf16))
                st = st * dec_s[bi, g:g + 1, hs(h)] + kv_s[bi, g, h]
            st_ref[bi, h] = st


def _gla(q, v, fraw, lb, prev, *, bsz, nt, reverse):
    d = q.shape[-1]
    tile = SUB
    nb = HG_BATCH
    nh = d // HG_HEAD_DIM
    npair = tile // (2 * HG_CHUNK)

    def tmap(b, s):
        t = jnp.where(s == 0, 0, nt - s) if reverse else s
        return (b, t, 0)

    spec = pl.BlockSpec((nb, tile, d), tmap)
    arrays = [q, v, fraw, lb]
    specs = [spec, spec, spec, _full_spec(lb)]
    aliases = {}
    if prev is not None:
        arrays.append(prev)
        specs.append(spec)
        aliases = {4: 0}
    return pl.pallas_call(
        functools.partial(_gla_kernel, reverse=reverse, accumulate=prev is not None),
        out_shape=jax.ShapeDtypeStruct(q.shape, f32),
        grid=(bsz // nb, nt),
        in_specs=specs,
        out_specs=spec,
        scratch_shapes=[pltpu.VMEM((nb, nh, HG_HEAD_DIM, HG_HEAD_DIM), f32)]
        + [pltpu.VMEM((nb, tile, d), bf16)] * 6
        + [pltpu.VMEM((nb, npair, d), f32),
           pltpu.VMEM((nb, npair, nh, HG_HEAD_DIM, HG_HEAD_DIM), f32),
           pltpu.VMEM((nb, 2 * npair, nh, HG_CHUNK, HG_CHUNK), bf16),
           pltpu.VMEM((nb, npair, nh, HG_CHUNK, HG_CHUNK), bf16)],
        input_output_aliases=aliases,
        compiler_params=_cparams(("parallel", "arbitrary")),
        name="hgrn2_gla_bwd" if reverse else "hgrn2_gla_fwd",
    )(*arrays)


def _s5_kernel(*refs, reverse, accumulate, nct, bsz):
    if accumulate:
        (h_ref, modsa_ref, gains_ref, wd_ref, wr_ref, are_ref, aim_ref, dsk_ref, prev_ref,
         y_ref, z_ref, st_ref, ytb_ref, u_s, ub_s) = refs
    else:
        (h_ref, modsa_ref, gains_ref, wd_ref, wr_ref, are_ref, aim_ref, dsk_ref,
         y_ref, z_ref, st_ref, ytb_ref, u_s, ub_s) = refs
        prev_ref = None
    s = pl.program_id(0)
    d = h_ref.shape[-1]
    ts = S5_TSTEPS
    nhalf = h_ref.shape[1] // ts
    rows = bsz * ts
    nslab = d // S5_SLAB
    lanes = z_ref.shape[3]
    nchunk = z_ref.shape[1] // 2
    per_slab = nchunk // nslab

    @pl.when(s == 0)
    def _():
        st_ref[...] = jnp.zeros_like(st_ref)

    is_ctx = s < nct
    g = gains_ref[0:1, :]
    crow = slice(MOD_ROWS - 8, MOD_ROWS - 7)
    shift = jnp.where(is_ctx, modsa_ref[crow, 0:1, :], modsa_ref[0:bsz, 0:1, :])
    scale = jnp.where(is_ctx, modsa_ref[crow, 1:2, :], modsa_ref[0:bsz, 1:2, :])
    ro = lax.broadcasted_iota(jnp.int32, (rows, rows), 0)
    co = lax.broadcasted_iota(jnp.int32, (rows, rows), 1)
    to_tb = jnp.where((ro // bsz == co % ts) & (ro % bsz == co // ts), 1.0, 0.0).astype(bf16)
    to_bt = jnp.where((co // bsz == ro % ts) & (co % bsz == ro // ts), 1.0, 0.0).astype(bf16)

    def head(hf):
        h3 = h_ref[:, hf * ts:(hf + 1) * ts, :]
        u3 = h3 * lax.rsqrt(jnp.mean(h3 * h3, axis=-1, keepdims=True) + EPS) * g * (1 + scale) + shift
        u = u3.reshape(rows, d)
        u_s[hf] = u
        ub_s[hf] = _dot(to_tb, u.astype(bf16)).astype(bf16)

    def drive(hf, ck):
        kb, cc = divmod(ck, per_slab)
        ubk = ub_s[hf, :, kb * S5_SLAB:(kb + 1) * S5_SLAB]
        z_ref[hf, ck] = _dot(ubk, wd_ref[kb, :, cc * lanes:(cc + 1) * lanes])
        z_ref[hf, nchunk + ck] = _dot(ubk, wd_ref[kb, :, (per_slab + cc) * lanes:(per_slab + cc + 1) * lanes])

    def scan(hf, ck):
        ar = are_ref[ck]
        ai = aim_ref[ck]
        zr = st_ref[ck]
        zi = st_ref[nchunk + ck]
        for t in (range(ts - 1, -1, -1) if reverse else range(ts)):
            rs = slice(t * bsz, (t + 1) * bsz)
            nzr = ar * zr - ai * zi + z_ref[hf, ck, rs, :]
            nzi = ar * zi + ai * zr + z_ref[hf, nchunk + ck, rs, :]
            zr, zi = nzr, nzi
            z_ref[hf, ck, rs, :] = zr
            z_ref[hf, nchunk + ck, rs, :] = zi
        st_ref[ck] = zr
        st_ref[nchunk + ck] = zi

    def readout(hf, ck):
        kb, cc = divmod(ck, per_slab)
        p = (_dot(z_ref[hf, ck].astype(bf16), wr_ref[kb, cc * lanes:(cc + 1) * lanes, :])
             + _dot(z_ref[hf, nchunk + ck].astype(bf16), wr_ref[kb, (per_slab + cc) * lanes:(per_slab + cc + 1) * lanes, :]))
        sl = slice(kb * S5_SLAB, (kb + 1) * S5_SLAB)
        if cc == 0:
            ytb_ref[hf, :, sl] = p
        elif cc < per_slab - 1:
            ytb_ref[hf, :, sl] += p
        else:
            ytb = ytb_ref[hf, :, sl] + p
            hi = ytb.astype(bf16)
            lo = (ytb - hi.astype(f32)).astype(bf16)
            y = _dot(to_bt, hi) + _dot(to_bt, lo)
            tsl = slice(hf * ts, (hf + 1) * ts)
            if prev_ref is not None:
                y = y + prev_ref[:, tsl, sl].reshape(rows, S5_SLAB)
            else:
                y = y + dsk_ref[:, sl] * u_s[hf, :, sl]
            y_ref[:, tsl, sl] = y.reshape(bsz, ts, S5_SLAB)

    def pipeline(hf):
        lead = 2
        steps = [functools.partial(drive, hf, ck) for ck in range(lead)]
        for ck in range(nchunk):
            if ck + lead < nchunk:
                steps.append(functools.partial(drive, hf, ck + lead))
            steps.append(functools.partial(scan, hf, ck))
            if ck >= 1:
                steps.append(functools.partial(readout, hf, ck - 1))
        steps.append(functools.partial(readout, hf, nchunk - 1))
        return steps

    order = list(range(nhalf - 1, -1, -1)) if reverse else list(range(nhalf))
    head(order[0])
    for i, hf in enumerate(order):
        steps = pipeline(hf)
        cut = (2 * len(steps)) // 3
        for fn in steps[:cut]:
            fn()
        if i + 1 < nhalf:
            head(order[i + 1])
        for fn in steps[cut:]:
            fn()


def _s5_scan(h, modsa, gains, wd, wr, a_re, a_im, dskip, prev, *, bsz, nct, nlt, reverse):
    _, ls, d = h.shape
    ts = S5_TSTEPS
    nhalf = S5_SUBTILES
    ntot = nct + nlt
    nchunk2, lanes = a_re.shape[1] * 2, a_re.shape[3]

    def tmap(s):
        if reverse:
            return (0, jnp.where(s < nct, nct - 1 - s, ntot + nct - 1 - s), 0)
        return (0, s, 0)

    spec = pl.BlockSpec((bsz, nhalf * ts, d), tmap)
    full = _full_spec
    arrays = [h, modsa, gains, wd, wr, a_re, a_im, dskip]
    dd = 1 if reverse else 0
    specs = [spec, full(modsa), full(gains), full(wd), _layer_spec(wr, dd), _layer_spec(a_re, dd),
             _layer_spec(a_im, dd), full(dskip)]
    aliases = {}
    if prev is not None:
        arrays.append(prev)
        specs.append(spec)
        aliases = {8: 0}
    return pl.pallas_call(
        functools.partial(_s5_kernel, reverse=reverse, accumulate=prev is not None, nct=nct, bsz=bsz),
        out_shape=jax.ShapeDtypeStruct(h.shape, f32),
        grid=(ntot,),
        in_specs=specs,
        out_specs=spec,
        scratch_shapes=[pltpu.VMEM((nhalf, nchunk2, bsz * ts, lanes), f32), pltpu.VMEM((nchunk2, bsz, lanes), f32),
                        pltpu.VMEM((nhalf, bsz * ts, d), f32), pltpu.VMEM((nhalf, bsz * ts, d), f32),
                        pltpu.VMEM((nhalf, bsz * ts, d), bf16)],
        input_output_aliases=aliases,
        compiler_params=_cparams(("arbitrary",)),
        name="s5_scan_bwd" if reverse else "s5_scan_fwd",
    )(*arrays)


def _s5_weights(b_re, b_im, cf_re, cf_im, a_re, a_im, lanes, bsz):
    g, n, c = b_re.shape
    gps = S5_SLAB // c
    nslab = g // gps
    eye = jnp.eye(gps, dtype=f32)

    def drive(bm):
        t = bm.astype(f32).reshape(nslab, gps, n, c).transpose(0, 1, 3, 2)
        t = t[:, :, :, None, :] * eye[None, :, None, :, None]
        return t.reshape(nslab, gps * c, gps * n)

    wd = jnp.concatenate([drive(b_re), drive(b_im)], axis=-1).astype(bf16)

    def read(cm):
        t = cm.reshape(nslab, gps, c, n).transpose(0, 1, 3, 2)
        t = t[:, :, :, None, :] * eye[None, :, None, :, None]
        return t.reshape(nslab, gps * n, gps * c)

    wr = jnp.stack([jnp.concatenate([read(cf_re[dd]), -read(cf_im[dd])], axis=1) for dd in range(2)]).astype(bf16)

    def decay(a):
        t = a.reshape(2, (g * n) // lanes, 1, lanes)
        return jnp.broadcast_to(t, (2, (g * n) // lanes, bsz, lanes))

    return wd, wr, decay(a_re), decay(a_im)


def _na_bias_table(rpb):
    nh = rpb.shape[0]
    qc = jnp.arange(GRID_W)[:, None]
    kc = jnp.arange(GRID_W)[None, :]
    dc = jnp.clip(kc - qc, 1 - NA_COLS, NA_COLS - 1) + (NA_COLS - 1)
    q_start = jnp.clip(qc - NA_COLS // 2, 0, GRID_W - NA_COLS)
    in_win = (kc >= q_start) & (kc < q_start + NA_COLS)
    rp = rpb.astype(f32)
    tab = sum(jnp.where(dc == ci, rp[:, :, ci, None, None], 0.0) for ci in range(2 * NA_COLS - 1))
    tab = jnp.where(in_win[None, None], tab, NEG_BIG)
    tab = tab.reshape(nh // 4, 4, 2 * NA_ROWS - 1, GRID_W, GRID_W)
    return tab.transpose(0, 2, 4, 1, 3).reshape(nh // 4, 2 * NA_ROWS - 1, GRID_W, 4 * GRID_W)


def _na_kernel(*refs, lc, nrows):
    q_ref, k_ref, v_ref, tab_ref, o_ref, s_scr, p_scr, m_scr = refs
    r0 = pl.program_id(1) * nrows
    rows_total = (k_ref.shape[0] - lc) // GRID_W
    nloc = NA_ROWS * GRID_W
    d = q_ref.shape[-1]
    dh = d // NA_HEADS
    npair = NA_HEADS // 2
    pw = 2 * dh
    qw = 2 * pw
    lane = lax.broadcasted_iota(jnp.int32, (GRID_W, pw), 1)
    lane4 = lax.broadcasted_iota(jnp.int32, (GRID_W, qw), 1) // dh
    kr0 = [jnp.clip(r0 + j - NA_ROWS // 2, 0, rows_total - NA_ROWS) for j in range(nrows)]
    k0 = [pl.multiple_of(lc + kr0[j] * GRID_W, GRID_W) for j in range(nrows)]
    dr0 = [kr0[j] - (r0 + j) + NA_ROWS - 1 for j in range(nrows)]
    ones_loc = jnp.ones((nloc, pw), bf16)
    ones_ctx = jnp.ones((lc, pw), bf16)
    ngrp = s_scr.shape[0]
    for g0 in range(0, nrows, ngrp):
        grp = range(g0, g0 + ngrp)
        for j in grp:
            for pp in range(npair // 2):
                sl = slice(pp * qw, (pp + 1) * qw)
                q4 = q_ref[j * GRID_W:(j + 1) * GRID_W, sl]
                zero = jnp.zeros_like(q4)
                qm = jnp.concatenate([jnp.where(lane4 == i, q4, zero) for i in range(4)], axis=0)
                bias = jnp.concatenate([tab_ref[pp, dr0[j] + i] for i in range(NA_ROWS)], axis=0)
                s_loc = _dot_nt(k_ref[pl.ds(k0[j], nloc), sl], qm) + bias
                s_ctx = _dot_nt(k_ref[0:lc, sl], qm)
                s_scr[j - g0, pp, 0:nloc, :] = s_loc
                s_scr[j - g0, pp, nloc:nloc + lc, :] = s_ctx
                m_scr[j - g0, pp] = jnp.maximum(jnp.max(s_loc, axis=0, keepdims=True),
                                                jnp.max(s_ctx, axis=0, keepdims=True))
        for j in grp:
            for pp in range(npair // 2):
                p_scr[j - g0, pp] = jnp.exp(s_scr[j - g0, pp] - m_scr[j - g0, pp]).astype(bf16)
        for j in grp:
            for p in range(npair):
                sl = slice(p * pw, (p + 1) * pw)
                ps = slice((p % 2) * pw, (p % 2 + 1) * pw)
                v_loc = jnp.concatenate([v_ref[pl.ds(k0[j], nloc), sl], ones_loc], axis=1)
                v_ctx = jnp.concatenate([v_ref[0:lc, sl], ones_ctx], axis=1)
                oa = (_dot_tn(p_scr[j - g0, p // 2, 0:nloc, ps], v_loc)
                      + _dot_tn(p_scr[j - g0, p // 2, nloc:nloc + lc, ps], v_ctx))
                o = oa[:, 0:pw] / oa[:, pw:qw]
                o_ref[j * GRID_W:(j + 1) * GRID_W, sl] = jnp.where(lane < dh, o[0:GRID_W], o[GRID_W:pw]).astype(o_ref.dtype)


def _na_core(q, k, v, bias, *, bsz, ls, lc, nrows):
    d = q.shape[1]
    nrow = (ls - lc) // GRID_W
    nq = nrows * GRID_W
    nkeys = NA_ROWS * GRID_W + lc
    kv_spec = pl.BlockSpec((ls, d), lambda b, rp: (b, 0), pipeline_mode=pl.Buffered(1))
    return pl.pallas_call(
        functools.partial(_na_kernel, lc=lc, nrows=nrows),
        out_shape=jax.ShapeDtypeStruct((bsz * nrow * GRID_W, d), bf16),
        grid=(bsz, nrow // nrows),
        in_specs=[pl.BlockSpec((nq, d), lambda b, rp: (b * (ls // nq) + lc // nq + rp, 0)), kv_spec, kv_spec,
                  _full_spec(bias)],
        out_specs=pl.BlockSpec((nq, d), lambda b, rp: (b * (nrow // nrows) + rp, 0)),
        scratch_shapes=[pltpu.VMEM((NA_GROUP, NA_HEADS // 4, nkeys, 4 * GRID_W), f32),
                        pltpu.VMEM((NA_GROUP, NA_HEADS // 4, nkeys, 4 * GRID_W), bf16),
                        pltpu.VMEM((NA_GROUP, NA_HEADS // 4, 1, 4 * GRID_W), f32)],
        compiler_params=_cparams(("parallel", "arbitrary")),
        name="na_core",
    )(q, k, v, bias)


def kernel(x, c, ctx, c_ctx, ada_w, ada_b, norm_gains, mlp_w_in, mlp_w_out, sc_w_in, sc_conv, sc_w_out, hg_w_in, hg_lower_bound, hg_norm, hg_w_out, s5_lam_re, s5_lam_im, s5_log_dt, s5_b_re, s5_b_im, s5_c_re, s5_c_im, s5_d, s5_w_glu, na_w_qkv, na_rpb, na_w_out):
    bsz, seq, d = x.shape
    lc = ctx.shape[1]
    depth = ada_w.shape[0]
    tile = SUB
    assert depth == 4 and lc == tile and seq % tile == 0 and seq % GRID_W == 0 and bsz <= MOD_ROWS - 8
    assert bsz % 8 == 0 and tile % S5_TSTEPS == 0
    ls = lc + seq
    nt = ls // tile
    nsub = 2
    assert (bsz * nt) % nsub == 0 and (nt - 1) % nsub == 0 and SUB == tile

    cc = jnp.zeros((MOD_ROWS, d), f32).at[:bsz].set(c.astype(f32)).at[MOD_ROWS - 8].set(c_ctx.astype(f32))
    mods_a = _mods(cc, ada_w, ada_b).transpose(0, 2, 1, 3)
    lb_all, a_re, a_im, cf_re, cf_im = _prep(hg_lower_bound, s5_lam_re[0], s5_lam_im[0], s5_log_dt[0],
                                             s5_c_re[0], s5_c_im[0])
    gains = norm_gains.astype(f32)
    w1 = mlp_w_in.astype(bf16)
    w2 = mlp_w_out.astype(bf16)
    xl = x.reshape(bsz * seq, d)
    xc = ctx.reshape(bsz * lc, d)

    bg, u = _conv_proj(xc, xl, mods_a[0], gains[0], sc_w_in[0].astype(bf16), bsz=bsz, nt=nt, nsub=nsub)
    h = _conv_post(xc, xl, bg, u, sc_conv[0].astype(f32), mods_a[0], gains[0], sc_w_out[0].astype(bf16),
                   w1, w2, 0, bsz=bsz, nt=nt, nsub=nsub)

    q, v, gate, f_fwd, f_bwd = _proj_flat(h, mods_a[1], gains[1], hg_w_in[0].astype(bf16),
                                          (bf16, bf16, bf16, f32, f32), bsz=bsz, nt=nt, nsub=nsub)
    lb1 = lb_all[1:2]
    q, v, f_fwd, f_bwd = (t.reshape(bsz, ls, d) for t in (q, v, f_fwd, f_bwd))
    o = _gla(q, v, f_fwd, lb1, None, bsz=bsz, nt=nt, reverse=False)
    o = _gla(q, v, f_bwd, lb1, o, bsz=bsz, nt=nt, reverse=True).reshape(bsz * ls, d)
    h = _post_mlp("hg", h, [o, gate], [hg_norm[0].astype(f32).reshape(1, d), hg_w_out[0].astype(bf16)],
                  mods_a[1], gains[1], w1, w2, 1, bsz=bsz, nt=nt, nsub=nsub)

    lanes = 256
    wd, wr, ar, ai = _s5_weights(s5_b_re[0], s5_b_im[0], cf_re, cf_im, a_re, a_im, lanes, bsz)
    h3 = h.reshape(bsz, ls, d)
    s5_tile = S5_SUBTILES * S5_TSTEPS
    assert lc % s5_tile == 0 and seq % s5_tile == 0
    nct, nlt = lc // s5_tile, seq // s5_tile
    dsk = s5_d[0].astype(f32).reshape(1, d)
    y = _s5_scan(h3, mods_a[2], gains[2], wd, wr, ar, ai, dsk, None, bsz=bsz, nct=nct, nlt=nlt, reverse=False)
    y = _s5_scan(h3, mods_a[2], gains[2], wd, wr, ar, ai, dsk, y, bsz=bsz, nct=nct, nlt=nlt, reverse=True)
    h = _post_mlp("s5", h, [y.reshape(bsz * ls, d)], [s5_w_glu[0].astype(bf16)],
                  mods_a[2], gains[2], w1, w2, 2, bsz=bsz, nt=nt, nsub=nsub)

    dh = d // NA_HEADS
    qq, kk, vv = _proj_flat(h, mods_a[3], gains[3], na_w_qkv[0].astype(bf16), (bf16, bf16, bf16), bsz=bsz, nt=nt,
                            nsub=nsub, scales=(dh ** -0.5, 1.0, 1.0))
    bias = _na_bias_table(na_rpb[0])
    o = _na_core(qq, kk, vv, bias, bsz=bsz, ls=ls, lc=lc, nrows=NA_GROUP)
    hl = _post_mlp("na", h, [o], [na_w_out[0].astype(bf16)], mods_a[3], gains[3], w1, w2, 3,
                   bsz=bsz, nt=nt, nsub=nsub, lat_only=True)
    return hl.reshape(bsz, seq, d)
```

```python
import functools

import jax
import jax.numpy as jnp
from jax import lax
from jax.experimental import pallas as pl
from jax.experimental.pallas import tpu as pltpu

EPS = 1e-6
N_MOD = 6
FF_CHUNK = 1024
MOD_ROWS = 24
HG_HEAD_DIM = 128
HG_CHUNK = 32
HG_BATCH = 2
S5_SLAB = 256
S5_TSTEPS = 16
S5_SUBTILES = 2
GRID_W = 64
NA_HEADS = 16
NA_ROWS = 8
NA_COLS = 16
NA_GROUP = 4
NEG_BIG = -1e30
VMEM_LIMIT = 56 * 1024 * 1024

bf16 = jnp.bfloat16
f32 = jnp.float32


def _cparams(sem):
    return pltpu.CompilerParams(dimension_semantics=sem, vmem_limit_bytes=VMEM_LIMIT)


def _rms(x, g):
    return x * lax.rsqrt(jnp.mean(x * x, axis=-1, keepdims=True) + EPS) * g


def _dot(a, b):
    return jnp.dot(a, b, preferred_element_type=f32)


def _dot_nt(a, b):
    return lax.dot_general(a, b, (((1,), (1,)), ((), ())), preferred_element_type=f32)


def _dot_tn(a, b):
    return lax.dot_general(a, b, (((0,), (0,)), ((), ())), preferred_element_type=f32)


def _mods_kernel(cc_ref, w_ref, b_ref, o_ref):
    x = cc_ref[...]
    a = (x * jax.nn.sigmoid(x)).astype(bf16)
    o_ref[...] = _dot(a, w_ref[...].astype(bf16)) + b_ref[...]


def _mods(cc, ada_w, ada_b):
    depth, d, _ = ada_w.shape
    out = pl.pallas_call(
        _mods_kernel,
        out_shape=jax.ShapeDtypeStruct((depth, N_MOD, MOD_ROWS, d), f32),
        grid=(depth, N_MOD),
        in_specs=[pl.BlockSpec((MOD_ROWS, d), lambda i, j: (0, 0)),
                  pl.BlockSpec((None, d, d), lambda i, j: (i, 0, j)),
                  pl.BlockSpec((None, 1, d), lambda i, j: (i * N_MOD + j, 0, 0))],
        out_specs=pl.BlockSpec((None, None, MOD_ROWS, d), lambda i, j: (i, j, 0, 0)),
        compiler_params=_cparams(("arbitrary", "arbitrary")),
        name="ada_mods",
    )(cc, ada_w, ada_b.reshape(depth * N_MOD, 1, d))
    return out


def _prep_kernel(lbp_ref, lre_ref, lim_ref, ldt_ref, cre_ref, cim_ref,
                 lb_ref, are_ref, aim_ref, cfre_ref, cfim_ref):
    x = lbp_ref[...]
    rows = [x[i:i + 1, :] for i in range(x.shape[0])]
    m = functools.reduce(jnp.maximum, rows)
    es = [jnp.exp(r - m) for r in rows]
    tot = functools.reduce(lambda a, b: a + b, es)
    acc = None
    first = None
    for i, e in enumerate(es):
        sm = e / tot
        acc = sm if acc is None else acc + sm
        if first is None:
            first = acc
        lb_ref[i:i + 1, :] = acc - first
    lam_re = jnp.minimum(lre_ref[...], -1e-4)
    lam_im = lim_ref[...]
    dt = jnp.exp(ldt_ref[...])
    mag = jnp.exp(lam_re * dt)
    a_re = mag * jnp.cos(lam_im * dt)
    a_im = mag * jnp.sin(lam_im * dt)
    den = lam_re * lam_re + lam_im * lam_im
    f_re = ((a_re - 1) * lam_re + a_im * lam_im) / den
    f_im = (a_im * lam_re - (a_re - 1) * lam_im) / den
    are_ref[...] = a_re
    aim_ref[...] = a_im
    c_re, c_im = cre_ref[...], cim_ref[...]
    cfre_ref[...] = c_re * f_re - c_im * f_im
    cfim_ref[...] = c_re * f_im + c_im * f_re


def _prep(hg_lower_bound, lam_re, lam_im, log_dt, c_re, c_im):
    shape = c_re.shape
    flat = (shape[0] * shape[1] * shape[2], shape[3])

    def expand(t):
        return jnp.broadcast_to(t[:, :, None, :], shape).reshape(flat)

    ldt = jnp.broadcast_to(log_dt[:, :, None, None], shape).reshape(flat)
    outs = pl.pallas_call(
        _prep_kernel,
        out_shape=[jax.ShapeDtypeStruct(hg_lower_bound.shape, f32)] + [jax.ShapeDtypeStruct(flat, f32)] * 4,
        name="param_prep",
    )(hg_lower_bound.astype(f32), expand(lam_re.astype(f32)), expand(lam_im.astype(f32)), ldt.astype(f32),
      c_re.astype(f32).reshape(flat), c_im.astype(f32).reshape(flat))
    lb, a_re, a_im, cf_re, cf_im = outs
    a_re = a_re.reshape(shape)[:, :, 0, :]
    a_im = a_im.reshape(shape)[:, :, 0, :]
    return lb, a_re, a_im, cf_re.reshape(shape), cf_im.reshape(shape)


def _mod_row(b, t):
    return jnp.where(t == 0, MOD_ROWS - 8, b)


def _full_spec(arr):
    nd = arr.ndim
    return pl.BlockSpec(arr.shape, lambda *_: (0,) * nd, pipeline_mode=pl.Buffered(1))


def _layer_spec(arr, i):
    nd = arr.ndim
    return pl.BlockSpec((None,) + arr.shape[1:], lambda *_: (i,) + (0,) * (nd - 1), pipeline_mode=pl.Buffered(1))


SUB = 256


def _sub(j):
    return slice(j * SUB, (j + 1) * SUB)


def _flat_specs(nsub, d, nt, lat_only):
    def where(i, j):
        g = i * nsub + j
        if lat_only:
            b, t = g // (nt - 1), g % (nt - 1) + 1
        else:
            b, t = g // nt, g % nt
        return b, t

    h_specs = [pl.BlockSpec((SUB, d), lambda i, j=j: (where(i, j)[0] * nt + where(i, j)[1], 0)) for j in range(nsub)]
    m_specs = [pl.BlockSpec((None, N_MOD, d), lambda i, j=j: (_mod_row(*where(i, j)), 0, 0)) for j in range(nsub)]
    return h_specs, m_specs


def _rows_spec(nsub, d):
    return pl.BlockSpec((nsub * SUB, d), lambda i: (i, 0))


def _prenorm(h, mods_ref, gains_ref, which):
    g = gains_ref[2 * which:2 * which + 1, :]
    shift = mods_ref[3 * which:3 * which + 1, :]
    scale = mods_ref[3 * which + 1:3 * which + 2, :]
    return _rms(h, g) * (1 + scale) + shift


def _residual(h, y, mods_ref, gains_ref, which):
    g = gains_ref[2 * which + 1:2 * which + 2, :]
    gate = mods_ref[3 * which + 2:3 * which + 3, :]
    return h + gate * _rms(y, g)


def _first_layer_h(t, hc_ref, hl_ref):
    return jnp.where(t == 0, hc_ref[...], hl_ref[...])


def _proj_flat_kernel(*refs, nsub, scales):
    h_refs, mods_refs = refs[:nsub], refs[nsub:2 * nsub]
    gains_ref, w_ref = refs[2 * nsub:2 * nsub + 2]
    outs, a_s = refs[2 * nsub + 2:-1], refs[-1]
    d = a_s.shape[-1]
    for j in range(nsub):
        a_s[_sub(j), :] = _prenorm(h_refs[j][...], mods_refs[j], gains_ref, 0).astype(bf16)
    for k, (o_ref, sc) in enumerate(zip(outs, scales)):
        piece = _dot(a_s[...], w_ref[:, k * d:(k + 1) * d])
        if sc != 1.0:
            piece = piece * sc
        o_ref[...] = piece.astype(o_ref.dtype)


def _proj_flat(h, mods, gains, w, out_dtypes, *, bsz, nt, nsub, scales=None):
    d = w.shape[0]
    rows = bsz * nt * SUB
    h_specs, m_specs = _flat_specs(nsub, d, nt, False)
    scales = tuple(scales or (1.0,) * len(out_dtypes))
    return pl.pallas_call(
        functools.partial(_proj_flat_kernel, nsub=nsub, scales=scales),
        out_shape=[jax.ShapeDtypeStruct((rows, d), dt) for dt in out_dtypes],
        grid=(rows // (nsub * SUB),),
        in_specs=h_specs + m_specs + [_full_spec(gains), _full_spec(w)],
        out_specs=[_rows_spec(nsub, d) for _ in out_dtypes],
        scratch_shapes=[pltpu.VMEM((nsub * SUB, d), bf16)],
        compiler_params=_cparams(("arbitrary",)),
        name="prenorm_proj",
    )(*([h] * nsub + [mods] * nsub + [gains, w]))


def _mlp_chunks(a, w1_ref, w2_ref, ff_chunk):
    nchunk = w1_ref.shape[1] // ff_chunk

    def hidden(c):
        hid = jnp.maximum(_dot(a, w1_ref[:, c * ff_chunk:(c + 1) * ff_chunk]), 0.0)
        return (hid * hid).astype(bf16)

    acc = None
    nxt = hidden(0)
    for c in range(nchunk):
        cur = nxt
        if c + 1 < nchunk:
            nxt = hidden(c + 1)
        part = _dot(cur, w2_ref[c * ff_chunk:(c + 1) * ff_chunk, :])
        acc = part if acc is None else acc + part
    return acc


def _post_mlp_kernel(*refs, kind, nsub, ff_chunk):
    h_refs, mods_refs, gains_ref = refs[:nsub], refs[nsub:2 * nsub], refs[2 * nsub]
    ins, (w1_ref, w2_ref, out_ref, h1_s, a_s) = refs[2 * nsub + 1:-5], refs[-5:]
    d = out_ref.shape[-1]
    if kind == "hg":
        o_in, gate_ref, gn_ref, wo_ref = ins
        o, gn = o_in[...], gn_ref[...]
        on = jnp.concatenate([_rms(o[:, k * HG_HEAD_DIM:(k + 1) * HG_HEAD_DIM], gn[:, k * HG_HEAD_DIM:(k + 1) * HG_HEAD_DIM])
                              for k in range(d // HG_HEAD_DIM)], axis=-1)
        g = gate_ref[...].astype(f32)
        y = _dot((on * (g * jax.nn.sigmoid(g))).astype(bf16), wo_ref[...])
    elif kind == "s5":
        y_in, wg_ref = ins
        r = _dot(jax.nn.gelu(y_in[...]).astype(bf16), wg_ref[...])
        y = r[:, :d] * jax.nn.sigmoid(r[:, d:])
    else:
        o_in, wo_ref = ins
        y = _dot(o_in[...], wo_ref[...])
    for j in range(nsub):
        h1 = _residual(h_refs[j][...], y[_sub(j)], mods_refs[j], gains_ref, 0)
        h1_s[_sub(j), :] = h1
        a_s[_sub(j), :] = _prenorm(h1, mods_refs[j], gains_ref, 1).astype(bf16)
    acc = _mlp_chunks(a_s[...], w1_ref, w2_ref, ff_chunk)
    for j in range(nsub):
        out_ref[_sub(j), :] = _residual(h1_s[_sub(j), :], acc[_sub(j)], mods_refs[j], gains_ref, 1)


def _post_mlp(kind, h, ins, consts, mods, gains, w1, w2, layer, *, bsz, nt, nsub, lat_only=False):
    d = w1.shape[1]
    rows = bsz * (nt - 1 if lat_only else nt) * SUB
    h_specs, m_specs = _flat_specs(nsub, d, nt, lat_only)
    return pl.pallas_call(
        functools.partial(_post_mlp_kernel, kind=kind, nsub=nsub, ff_chunk=min(FF_CHUNK, w1.shape[-1])),
        out_shape=jax.ShapeDtypeStruct((rows, d), f32),
        grid=(rows // (nsub * SUB),),
        in_specs=(h_specs + m_specs + [_full_spec(gains)] + [_rows_spec(nsub, d) for _ in ins]
                  + [_full_spec(c_) for c_ in consts] + [_layer_spec(w1, layer), _layer_spec(w2, layer)]),
        out_specs=_rows_spec(nsub, d),
        scratch_shapes=[pltpu.VMEM((nsub * SUB, d), f32), pltpu.VMEM((nsub * SUB, d), bf16)],
        compiler_params=_cparams(("arbitrary",)),
        name=kind + "_out_mlp",
    )(*([h] * nsub + [mods] * nsub + [gains] + list(ins) + list(consts) + [w1, w2]))


def _first_specs(nsub, d, nt):
    def where(i, j):
        g = i * nsub + j
        return g // nt, g % nt

    c_specs = [pl.BlockSpec((SUB, d), lambda i, j=j: (where(i, j)[0], 0)) for j in range(nsub)]
    l_specs = [pl.BlockSpec((SUB, d), lambda i, j=j: (where(i, j)[0] * (nt - 1) + jnp.maximum(where(i, j)[1] - 1, 0), 0))
               for j in range(nsub)]
    m_specs = [pl.BlockSpec((None, N_MOD, d), lambda i, j=j: (_mod_row(*where(i, j)), 0, 0)) for j in range(nsub)]
    return c_specs, l_specs, m_specs


def _sub_t(nsub, nt, j):
    return (pl.program_id(0) * nsub + j) % nt


def _conv_proj_kernel(*refs, nsub, nt):
    hc_refs, hl_refs, mods_refs = refs[:nsub], refs[nsub:2 * nsub], refs[2 * nsub:3 * nsub]
    gains_ref, w_ref, bg_ref, u_ref, a_s = refs[3 * nsub:]
    d = a_s.shape[-1]
    for j in range(nsub):
        h = _first_layer_h(_sub_t(nsub, nt, j), hc_refs[j], hl_refs[j])
        a_s[_sub(j), :] = _prenorm(h, mods_refs[j], gains_ref, 0).astype(bf16)
    a = a_s[...]
    bg_ref[...] = _dot(a, w_ref[:, 0:d]).astype(bg_ref.dtype)
    u_ref[...] = (_dot(a, w_ref[:, d:2 * d]) * _dot(a, w_ref[:, 2 * d:3 * d])).astype(u_ref.dtype)


def _conv_proj(hc, hl, mods, gains, w, *, bsz, nt, nsub):
    d = w.shape[0]
    rows = bsz * nt * SUB
    c_specs, l_specs, m_specs = _first_specs(nsub, d, nt)
    return pl.pallas_call(
        functools.partial(_conv_proj_kernel, nsub=nsub, nt=nt),
        out_shape=[jax.ShapeDtypeStruct((rows, d), bf16)] * 2,
        grid=(rows // (nsub * SUB),),
        in_specs=c_specs + l_specs + m_specs + [_full_spec(gains), _full_spec(w)],
        out_specs=[_rows_spec(nsub, d)] * 2,
        scratch_shapes=[pltpu.VMEM((nsub * SUB, d), bf16)],
        compiler_params=_cparams(("arbitrary",)),
        name="conv_proj",
    )(*([hc] * nsub + [hl] * nsub + [mods] * nsub + [gains, w]))


def _conv_post_kernel(*refs, nsub, nt, ff_chunk):
    hc_refs, hl_refs, mods_refs = refs[:nsub], refs[nsub:2 * nsub], refs[2 * nsub:3 * nsub]
    (gains_ref, bg_ref, u_ref, up_ref, un_ref, cw_ref, wo_ref, w1_ref, w2_ref, o_ref, h1_s, a_s) = refs[3 * nsub:]
    u = u_ref[...].astype(f32)
    rows = u.shape[0]
    hal = up_ref.shape[0]
    ts = [_sub_t(nsub, nt, j) for j in range(nsub)]
    starts = [t <= 1 for t in ts]
    ends = [(t == 0) | (t == nt - 1) for t in ts]
    ridx = lax.broadcasted_iota(jnp.int32, u.shape, 0)
    u_prev = jnp.where(ridx == 0, up_ref[hal - 1:hal, :].astype(f32), pltpu.roll(u, 1, 0))
    u_next = jnp.where(ridx == rows - 1, un_ref[0:1, :].astype(f32), pltpu.roll(u, rows - 1, 0))
    for j in range(nsub):
        u_prev = jnp.where((ridx == j * SUB) & starts[j], 0.0, u_prev)
        u_next = jnp.where((ridx == (j + 1) * SUB - 1) & ends[j], 0.0, u_next)
    cw = cw_ref[...]
    conv = cw[0:1, :] * u_prev + cw[1:2, :] * u + cw[2:3, :] * u_next
    y = _dot((bg_ref[...].astype(f32) * conv).astype(bf16), wo_ref[...])
    for j in range(nsub):
        h = _first_layer_h(ts[j], hc_refs[j], hl_refs[j])
        h1 = _residual(h, y[_sub(j)], mods_refs[j], gains_ref, 0)
        h1_s[_sub(j), :] = h1
        a_s[_sub(j), :] = _prenorm(h1, mods_refs[j], gains_ref, 1).astype(bf16)
    acc = _mlp_chunks(a_s[...], w1_ref, w2_ref, ff_chunk)
    for j in range(nsub):
        o_ref[_sub(j), :] = _residual(h1_s[_sub(j), :], acc[_sub(j)], mods_refs[j], gains_ref, 1)


def _conv_post(hc, hl, bg, u, conv_w, mods, gains, wo, w1, w2, layer, *, bsz, nt, nsub):
    d = wo.shape[0]
    tile = nsub * SUB
    hal = 16
    per = tile // hal
    rows = bsz * nt * SUB
    nblk = rows // hal
    c_specs, l_specs, m_specs = _first_specs(nsub, d, nt)
    return pl.pallas_call(
        functools.partial(_conv_post_kernel, nsub=nsub, nt=nt, ff_chunk=min(FF_CHUNK, w1.shape[-1])),
        out_shape=jax.ShapeDtypeStruct((rows, d), f32),
        grid=(rows // tile,),
        in_specs=c_specs + l_specs + m_specs + [
            _full_spec(gains), _rows_spec(nsub, d), _rows_spec(nsub, d),
            pl.BlockSpec((hal, d), lambda i: (jnp.maximum(i * per - 1, 0), 0)),
            pl.BlockSpec((hal, d), lambda i: (jnp.minimum((i + 1) * per, nblk - 1), 0)),
            _full_spec(conv_w), _full_spec(wo), _layer_spec(w1, layer), _layer_spec(w2, layer)],
        out_specs=_rows_spec(nsub, d),
        scratch_shapes=[pltpu.VMEM((tile, d), f32), pltpu.VMEM((tile, d), bf16)],
        compiler_params=_cparams(("arbitrary",)),
        name="conv_out_mlp",
    )(*([hc] * nsub + [hl] * nsub + [mods] * nsub + [gains, bg, u, u, u, conv_w, wo, w1, w2]))


def _gla_kernel(*refs, reverse, accumulate):
    if accumulate:
        (q_ref, v_ref, f_ref, lb_ref, prev_ref, o_ref, st_ref,
         qin_s, kin_s, qout_s, kst_s, q2_s, k2_s, dec_s, kv_s, sc_s, sx_s) = refs
    else:
        (q_ref, v_ref, f_ref, lb_ref, o_ref, st_ref,
         qin_s, kin_s, qout_s, kst_s, q2_s, k2_s, dec_s, kv_s, sc_s, sx_s) = refs
        prev_ref = None
    nb, tile, d = q_ref.shape
    nh = d // HG_HEAD_DIM
    c = HG_CHUNK
    nchunk = tile // c

    @pl.when(pl.program_id(1) == 0)
    def _():
        st_ref[...] = jnp.zeros_like(st_ref)

    ri = lax.broadcasted_iota(jnp.int32, (c, c), 0)
    ci = lax.broadcasted_iota(jnp.int32, (c, c), 1)
    causal = (ci >= ri) if reverse else (ci <= ri)
    tri = jnp.where(causal, 1.0, 0.0).astype(bf16)
    mid = c // 2 if reverse else c // 2 - 1
    last = 0 if reverse else c - 1
    lb = lb_ref[...]

    ngroup = nchunk // 2
    for bi in range(nb):
        for g in range(ngroup):
            vals = {}
            for k in (2 * g, 2 * g + 1):
                rows = slice(k * c, (k + 1) * c)
                fg = lb + (1 - lb) * jax.nn.sigmoid(f_ref[bi, rows, :])
                kk = 1 - fg
                lf = jnp.log(fg)
                hi = lf.astype(bf16)
                md = (lf - hi.astype(f32)).astype(bf16)
                bcum = _dot(tri, hi) + _dot(tri, md)
                b_mid = bcum[mid:mid + 1, :]
                b_last = bcum[last:last + 1, :]
                q_mid = q_ref[bi, rows, :].astype(f32) * jnp.exp(bcum - b_mid)
                k_mid = kk * jnp.exp(b_mid - bcum)
                qin_s[bi, rows, :] = q_mid.astype(bf16)
                kin_s[bi, rows, :] = k_mid.astype(bf16)
                vals[k] = (rows, q_mid * jnp.exp(b_mid), k_mid * jnp.exp(b_last - b_mid), jnp.exp(b_last))
            ka, kb = (2 * g + 1, 2 * g) if reverse else (2 * g, 2 * g + 1)
            rows_a, qo_a, ks_a, dec_a = vals[ka]
            rows_b, qo_b, ks_b, dec_b = vals[kb]
            qout_s[bi, rows_b, :] = qo_b.astype(bf16)
            kst_s[bi, rows_a, :] = ks_a.astype(bf16)
            q2_s[bi, rows_a, :] = qo_a.astype(bf16)
            q2_s[bi, rows_b, :] = (qo_b * dec_a).astype(bf16)
            k2_s[bi, rows_a, :] = (ks_a * dec_b).astype(bf16)
            k2_s[bi, rows_b, :] = ks_b.astype(bf16)
            dec_s[bi, g:g + 1, :] = dec_a * dec_b

    def hs(h):
        return slice(h * HG_HEAD_DIM, (h + 1) * HG_HEAD_DIM)

    def cr(k):
        return slice(k * c, (k + 1) * c)

    def ab(g):
        return (2 * g + 1, 2 * g) if reverse else (2 * g, 2 * g + 1)

    units = [(bi, k, h) for bi in range(nb) for k in range(nchunk) for h in range(nh)]
    gunits = [(bi, g, h) for bi in range(nb) for g in range(ngroup) for h in range(nh)]
    for bi, k, h in units:
        sc = _dot_nt(qin_s[bi, cr(k), hs(h)], kin_s[bi, cr(k), hs(h)])
        sc_s[bi, k, h] = jnp.where(causal, sc, 0.0).astype(bf16)
    for bi, g, h in gunits:
        ka, kb = ab(g)
        sx_s[bi, g, h] = _dot_nt(qout_s[bi, cr(kb), hs(h)], kst_s[bi, cr(ka), hs(h)]).astype(bf16)
    for bi, k, h in units:
        o_h = _dot(sc_s[bi, k, h], v_ref[bi, cr(k), hs(h)])
        ka, kb = ab(k // 2)
        if k == kb:
            o_h = o_h + _dot(sx_s[bi, k // 2, h], v_ref[bi, cr(ka), hs(h)])
        if prev_ref is not None:
            o_h = o_h + prev_ref[bi, cr(k), hs(h)]
        o_ref[bi, cr(k), hs(h)] = o_h
    for bi, g, h in gunits:
        rows = slice(2 * g * c, (2 * g + 2) * c)
        kv_s[bi, g, h] = _dot_tn(v_ref[bi, rows, hs(h)], k2_s[bi, rows, hs(h)])

    for bi in range(nb):
        for h in range(nh):
            st = st_ref[bi, h]
            for i in range(ngroup):
                g = (ngroup - 1 - i) if reverse else i
                rows = slice(2 * g * c, (2 * g + 2) * c)
                o_ref[bi, rows, hs(h)] += _dot_nt(q2_s[bi, rows, hs(h)], st.astype(bf16))
                st = st * dec_s[bi, g:g + 1, hs(h)] + kv_s[bi, g, h]
            st_ref[bi, h] = st


def _gla(q, v, fraw, lb, prev, *, bsz, nt, reverse):
    d = q.shape[-1]
    tile = SUB
    nb = HG_BATCH
    nh = d // HG_HEAD_DIM
    npair = tile // (2 * HG_CHUNK)

    def tmap(b, s):
        t = jnp.where(s == 0, 0, nt - s) if reverse else s
        return (b, t, 0)

    spec = pl.BlockSpec((nb, tile, d), tmap)
    arrays = [q, v, fraw, lb]
    specs = [spec, spec, spec, _full_spec(lb)]
    aliases = {}
    if prev is not None:
        arrays.append(prev)
        specs.append(spec)
        aliases = {4: 0}
    return pl.pallas_call(
        functools.partial(_gla_kernel, reverse=reverse, accumulate=prev is not None),
        out_shape=jax.ShapeDtypeStruct(q.shape, f32),
        grid=(bsz // nb, nt),
        in_specs=specs,
        out_specs=spec,
        scratch_shapes=[pltpu.VMEM((nb, nh, HG_HEAD_DIM, HG_HEAD_DIM), f32)]
        + [pltpu.VMEM((nb, tile, d), bf16)] * 6
        + [pltpu.VMEM((nb, npair, d), f32),
           pltpu.VMEM((nb, npair, nh, HG_HEAD_DIM, HG_HEAD_DIM), f32),
           pltpu.VMEM((nb, 2 * npair, nh, HG_CHUNK, HG_CHUNK), bf16),
           pltpu.VMEM((nb, npair, nh, HG_CHUNK, HG_CHUNK), bf16)],
        input_output_aliases=aliases,
        compiler_params=_cparams(("parallel", "arbitrary")),
        name="hgrn2_gla_bwd" if reverse else "hgrn2_gla_fwd",
    )(*arrays)


def _s5_kernel(*refs, reverse, accumulate, nct, bsz):
    if accumulate:
        (h_ref, modsa_ref, gains_ref, wd_ref, wr_ref, are_ref, aim_ref, dsk_ref, prev_ref,
         y_ref, z_ref, st_ref, ytb_ref, u_s, ub_s) = refs
    else:
        (h_ref, modsa_ref, gains_ref, wd_ref, wr_ref, are_ref, aim_ref, dsk_ref,
         y_ref, z_ref, st_ref, ytb_ref, u_s, ub_s) = refs
        prev_ref = None
    s = pl.program_id(0)
    d = h_ref.shape[-1]
    ts = S5_TSTEPS
    nhalf = h_ref.shape[1] // ts
    rows = bsz * ts
    nslab = d // S5_SLAB
    lanes = z_ref.shape[3]
    nchunk = z_ref.shape[1] // 2
    per_slab = nchunk // nslab

    @pl.when(s == 0)
    def _():
        st_ref[...] = jnp.zeros_like(st_ref)

    is_ctx = s < nct
    g = gains_ref[0:1, :]
    crow = slice(MOD_ROWS - 8, MOD_ROWS - 7)
    shift = jnp.where(is_ctx, modsa_ref[crow, 0:1, :], modsa_ref[0:bsz, 0:1, :])
    scale = jnp.where(is_ctx, modsa_ref[crow, 1:2, :], modsa_ref[0:bsz, 1:2, :])
    ro = lax.broadcasted_iota(jnp.int32, (rows, rows), 0)
    co = lax.broadcasted_iota(jnp.int32, (rows, rows), 1)
    to_tb = jnp.where((ro // bsz == co % ts) & (ro % bsz == co // ts), 1.0, 0.0).astype(bf16)
    to_bt = jnp.where((co // bsz == ro % ts) & (co % bsz == ro // ts), 1.0, 0.0).astype(bf16)

    def head(hf):
        h3 = h_ref[:, hf * ts:(hf + 1) * ts, :]
        u3 = h3 * lax.rsqrt(jnp.mean(h3 * h3, axis=-1, keepdims=True) + EPS) * g * (1 + scale) + shift
        u = u3.reshape(rows, d)
        u_s[hf] = u
        ub_s[hf] = _dot(to_tb, u.astype(bf16)).astype(bf16)

    def drive(hf, ck):
        kb, cc = divmod(ck, per_slab)
        ubk = ub_s[hf, :, kb * S5_SLAB:(kb + 1) * S5_SLAB]
        z_ref[hf, ck] = _dot(ubk, wd_ref[kb, :, cc * lanes:(cc + 1) * lanes])
        z_ref[hf, nchunk + ck] = _dot(ubk, wd_ref[kb, :, (per_slab + cc) * lanes:(per_slab + cc + 1) * lanes])

    def scan(hf, ck):
        ar = are_ref[ck]
        ai = aim_ref[ck]
        zr = st_ref[ck]
        zi = st_ref[nchunk + ck]
        for t in (range(ts - 1, -1, -1) if reverse else range(ts)):
            rs = slice(t * bsz, (t + 1) * bsz)
            nzr = ar * zr - ai * zi + z_ref[hf, ck, rs, :]
            nzi = ar * zi + ai * zr + z_ref[hf, nchunk + ck, rs, :]
            zr, zi = nzr, nzi
            z_ref[hf, ck, rs, :] = zr
            z_ref[hf, nchunk + ck, rs, :] = zi
        st_ref[ck] = zr
        st_ref[nchunk + ck] = zi

    def readout(hf, ck):
        kb, cc = divmod(ck, per_slab)
        p = (_dot(z_ref[hf, ck].astype(bf16), wr_ref[kb, cc * lanes:(cc + 1) * lanes, :])
             + _dot(z_ref[hf, nchunk + ck].astype(bf16), wr_ref[kb, (per_slab + cc) * lanes:(per_slab + cc + 1) * lanes, :]))
        sl = slice(kb * S5_SLAB, (kb + 1) * S5_SLAB)
        if cc == 0:
            ytb_ref[hf, :, sl] = p
        elif cc < per_slab - 1:
            ytb_ref[hf, :, sl] += p
        else:
            ytb = ytb_ref[hf, :, sl] + p
            hi = ytb.astype(bf16)
            lo = (ytb - hi.astype(f32)).astype(bf16)
            y = _dot(to_bt, hi) + _dot(to_bt, lo)
            tsl = slice(hf * ts, (hf + 1) * ts)
            if prev_ref is not None:
                y = y + prev_ref[:, tsl, sl].reshape(rows, S5_SLAB)
            else:
                y = y + dsk_ref[:, sl] * u_s[hf, :, sl]
            y_ref[:, tsl, sl] = y.reshape(bsz, ts, S5_SLAB)

    def pipeline(hf):
        lead = 2
        steps = [functools.partial(drive, hf, ck) for ck in range(lead)]
        for ck in range(nchunk):
            if ck + lead < nchunk:
                steps.append(functools.partial(drive, hf, ck + lead))
            steps.append(functools.partial(scan, hf, ck))
            if ck >= 1:
                steps.append(functools.partial(readout, hf, ck - 1))
        steps.append(functools.partial(readout, hf, nchunk - 1))
        return steps

    order = list(range(nhalf - 1, -1, -1)) if reverse else list(range(nhalf))
    head(order[0])
    for i, hf in enumerate(order):
        steps = pipeline(hf)
        cut = (2 * len(steps)) // 3
        for fn in steps[:cut]:
            fn()
        if i + 1 < nhalf:
            head(order[i + 1])
        for fn in steps[cut:]:
            fn()


def _s5_scan(h, modsa, gains, wd, wr, a_re, a_im, dskip, prev, *, bsz, nct, nlt, reverse):
    _, ls, d = h.shape
    ts = S5_TSTEPS
    nhalf = S5_SUBTILES
    ntot = nct + nlt
    nchunk2, lanes = a_re.shape[1] * 2, a_re.shape[3]

    def tmap(s):
        if reverse:
            return (0, jnp.where(s < nct, nct - 1 - s, ntot + nct - 1 - s), 0)
        return (0, s, 0)

    spec = pl.BlockSpec((bsz, nhalf * ts, d), tmap)
    full = _full_spec
    arrays = [h, modsa, gains, wd, wr, a_re, a_im, dskip]
    dd = 1 if reverse else 0
    specs = [spec, full(modsa), full(gains), full(wd), _layer_spec(wr, dd), _layer_spec(a_re, dd),
             _layer_spec(a_im, dd), full(dskip)]
    aliases = {}
    if prev is not None:
        arrays.append(prev)
        specs.append(spec)
        aliases = {8: 0}
    return pl.pallas_call(
        functools.partial(_s5_kernel, reverse=reverse, accumulate=prev is not None, nct=nct, bsz=bsz),
        out_shape=jax.ShapeDtypeStruct(h.shape, f32),
        grid=(ntot,),
        in_specs=specs,
        out_specs=spec,
        scratch_shapes=[pltpu.VMEM((nhalf, nchunk2, bsz * ts, lanes), f32), pltpu.VMEM((nchunk2, bsz, lanes), f32),
                        pltpu.VMEM((nhalf, bsz * ts, d), f32), pltpu.VMEM((nhalf, bsz * ts, d), f32),
                        pltpu.VMEM((nhalf, bsz * ts, d), bf16)],
        input_output_aliases=aliases,
        compiler_params=_cparams(("arbitrary",)),
        name="s5_scan_bwd" if reverse else "s5_scan_fwd",
    )(*arrays)


def _s5_weights(b_re, b_im, cf_re, cf_im, a_re, a_im, lanes, bsz):
    g, n, c = b_re.shape
    gps = S5_SLAB // c
    nslab = g // gps
    eye = jnp.eye(gps, dtype=f32)

    def drive(bm):
        t = bm.astype(f32).reshape(nslab, gps, n, c).transpose(0, 1, 3, 2)
        t = t[:, :, :, None, :] * eye[None, :, None, :, None]
        return t.reshape(nslab, gps * c, gps * n)

    wd = jnp.concatenate([drive(b_re), drive(b_im)], axis=-1).astype(bf16)

    def read(cm):
        t = cm.reshape(nslab, gps, c, n).transpose(0, 1, 3, 2)
        t = t[:, :, :, None, :] * eye[None, :, None, :, None]
        return t.reshape(nslab, gps * n, gps * c)

    wr = jnp.stack([jnp.concatenate([read(cf_re[dd]), -read(cf_im[dd])], axis=1) for dd in range(2)]).astype(bf16)

    def decay(a):
        t = a.reshape(2, (g * n) // lanes, 1, lanes)
        return jnp.broadcast_to(t, (2, (g * n) // lanes, bsz, lanes))

    return wd, wr, decay(a_re), decay(a_im)


def _na_bias_table(rpb):
    nh = rpb.shape[0]
    qc = jnp.arange(GRID_W)[:, None]
    kc = jnp.arange(GRID_W)[None, :]
    dc = jnp.clip(kc - qc, 1 - NA_COLS, NA_COLS - 1) + (NA_COLS - 1)
    q_start = jnp.clip(qc - NA_COLS // 2, 0, GRID_W - NA_COLS)
    in_win = (kc >= q_start) & (kc < q_start + NA_COLS)
    rp = rpb.astype(f32)
    tab = sum(jnp.where(dc == ci, rp[:, :, ci, None, None], 0.0) for ci in range(2 * NA_COLS - 1))
    tab = jnp.where(in_win[None, None], tab, NEG_BIG)
    tab = tab.reshape(nh // 4, 4, 2 * NA_ROWS - 1, GRID_W, GRID_W)
    return tab.transpose(0, 2, 4, 1, 3).reshape(nh // 4, 2 * NA_ROWS - 1, GRID_W, 4 * GRID_W)


def _na_kernel(*refs, lc, nrows):
    q_ref, k_ref, v_ref, tab_ref, o_ref, s_scr, p_scr, m_scr = refs
    r0 = pl.program_id(1) * nrows
    rows_total = (k_ref.shape[0] - lc) // GRID_W
    nloc = NA_ROWS * GRID_W
    d = q_ref.shape[-1]
    dh = d // NA_HEADS
    npair = NA_HEADS // 2
    pw = 2 * dh
    qw = 2 * pw
    lane = lax.broadcasted_iota(jnp.int32, (GRID_W, pw), 1)
    lane4 = lax.broadcasted_iota(jnp.int32, (GRID_W, qw), 1) // dh
    kr0 = [jnp.clip(r0 + j - NA_ROWS // 2, 0, rows_total - NA_ROWS) for j in range(nrows)]
    k0 = [pl.multiple_of(lc + kr0[j] * GRID_W, GRID_W) for j in range(nrows)]
    dr0 = [kr0[j] - (r0 + j) + NA_ROWS - 1 for j in range(nrows)]
    ones_loc = jnp.ones((nloc, pw), bf16)
    ones_ctx = jnp.ones((lc, pw), bf16)
    ngrp = s_scr.shape[0]
    for g0 in range(0, nrows, ngrp):
        grp = range(g0, g0 + ngrp)
        for j in grp:
            for pp in range(npair // 2):
                sl = slice(pp * qw, (pp + 1) * qw)
                q4 = q_ref[j * GRID_W:(j + 1) * GRID_W, sl]
                zero = jnp.zeros_like(q4)
                qm = jnp.concatenate([jnp.where(lane4 == i, q4, zero) for i in range(4)], axis=0)
                bias = jnp.concatenate([tab_ref[pp, dr0[j] + i] for i in range(NA_ROWS)], axis=0)
                s_loc = _dot_nt(k_ref[pl.ds(k0[j], nloc), sl], qm) + bias
                s_ctx = _dot_nt(k_ref[0:lc, sl], qm)
                s_scr[j - g0, pp, 0:nloc, :] = s_loc
                s_scr[j - g0, pp, nloc:nloc + lc, :] = s_ctx
                m_scr[j - g0, pp] = jnp.maximum(jnp.max(s_loc, axis=0, keepdims=True),
                                                jnp.max(s_ctx, axis=0, keepdims=True))
        for j in grp:
            for pp in range(npair // 2):
                p_scr[j - g0, pp] = jnp.exp(s_scr[j - g0, pp] - m_scr[j - g0, pp]).astype(bf16)
        for j in grp:
            for p in range(npair):
                sl = slice(p * pw, (p + 1) * pw)
                ps = slice((p % 2) * pw, (p % 2 + 1) * pw)
                v_loc = jnp.concatenate([v_ref[pl.ds(k0[j], nloc), sl], ones_loc], axis=1)
                v_ctx = jnp.concatenate([v_ref[0:lc, sl], ones_ctx], axis=1)
                oa = (_dot_tn(p_scr[j - g0, p // 2, 0:nloc, ps], v_loc)
                      + _dot_tn(p_scr[j - g0, p // 2, nloc:nloc + lc, ps], v_ctx))
                o = oa[:, 0:pw] / oa[:, pw:qw]
                o_ref[j * GRID_W:(j + 1) * GRID_W, sl] = jnp.where(lane < dh, o[0:GRID_W], o[GRID_W:pw]).astype(o_ref.dtype)


def _na_core(q, k, v, bias, *, bsz, ls, lc, nrows):
    d = q.shape[1]
    nrow = (ls - lc) // GRID_W
    nq = nrows * GRID_W
    nkeys = NA_ROWS * GRID_W + lc
    kv_spec = pl.BlockSpec((ls, d), lambda b, rp: (b, 0), pipeline_mode=pl.Buffered(1))
    return pl.pallas_call(
        functools.partial(_na_kernel, lc=lc, nrows=nrows),
        out_shape=jax.ShapeDtypeStruct((bsz * nrow * GRID_W, d), bf16),
        grid=(bsz, nrow // nrows),
        in_specs=[pl.BlockSpec((nq, d), lambda b, rp: (b * (ls // nq) + lc // nq + rp, 0)),
                  pl.BlockSpec((ls, d), lambda b, rp: (b, 0)), kv_spec, _full_spec(bias)],
        out_specs=pl.BlockSpec((nq, d), lambda b, rp: (b * (nrow // nrows) + rp, 0)),
        scratch_shapes=[pltpu.VMEM((NA_GROUP, NA_HEADS // 4, nkeys, 4 * GRID_W), f32),
                        pltpu.VMEM((NA_GROUP, NA_HEADS // 4, nkeys, 4 * GRID_W), bf16),
                        pltpu.VMEM((NA_GROUP, NA_HEADS // 4, 1, 4 * GRID_W), f32)],
        compiler_params=_cparams(("parallel", "arbitrary")),
        name="na_core",
    )(q, k, v, bias)


def kernel(x, c, ctx, c_ctx, ada_w, ada_b, norm_gains, mlp_w_in, mlp_w_out, sc_w_in, sc_conv, sc_w_out, hg_w_in, hg_lower_bound, hg_norm, hg_w_out, s5_lam_re, s5_lam_im, s5_log_dt, s5_b_re, s5_b_im, s5_c_re, s5_c_im, s5_d, s5_w_glu, na_w_qkv, na_rpb, na_w_out):
    bsz, seq, d = x.shape
    lc = ctx.shape[1]
    depth = ada_w.shape[0]
    tile = SUB
    assert depth == 4 and lc == tile and seq % tile == 0 and seq % GRID_W == 0 and bsz <= MOD_ROWS - 8
    assert bsz % 8 == 0 and tile % S5_TSTEPS == 0
    ls = lc + seq
    nt = ls // tile
    nsub = 2
    assert (bsz * nt) % nsub == 0 and (nt - 1) % nsub == 0 and SUB == tile

    cc = jnp.zeros((MOD_ROWS, d), f32).at[:bsz].set(c.astype(f32)).at[MOD_ROWS - 8].set(c_ctx.astype(f32))
    mods_a = _mods(cc, ada_w, ada_b).transpose(0, 2, 1, 3)
    lb_all, a_re, a_im, cf_re, cf_im = _prep(hg_lower_bound, s5_lam_re[0], s5_lam_im[0], s5_log_dt[0],
                                             s5_c_re[0], s5_c_im[0])
    gains = norm_gains.astype(f32)
    w1 = mlp_w_in.astype(bf16)
    w2 = mlp_w_out.astype(bf16)
    xl = x.reshape(bsz * seq, d)
    xc = ctx.reshape(bsz * lc, d)

    bg, u = _conv_proj(xc, xl, mods_a[0], gains[0], sc_w_in[0].astype(bf16), bsz=bsz, nt=nt, nsub=2 * nsub)
    h = _conv_post(xc, xl, bg, u, sc_conv[0].astype(f32), mods_a[0], gains[0], sc_w_out[0].astype(bf16),
                   w1, w2, 0, bsz=bsz, nt=nt, nsub=nsub)

    q, v, gate, f_fwd, f_bwd = _proj_flat(h, mods_a[1], gains[1], hg_w_in[0].astype(bf16),
                                          (bf16, bf16, bf16, f32, f32), bsz=bsz, nt=nt, nsub=nsub)
    lb1 = lb_all[1:2]
    q, v, f_fwd, f_bwd = (t.reshape(bsz, ls, d) for t in (q, v, f_fwd, f_bwd))
    o = _gla(q, v, f_fwd, lb1, None, bsz=bsz, nt=nt, reverse=False)
    o = _gla(q, v, f_bwd, lb1, o, bsz=bsz, nt=nt, reverse=True).reshape(bsz * ls, d)
    h = _post_mlp("hg", h, [o, gate], [hg_norm[0].astype(f32).reshape(1, d), hg_w_out[0].astype(bf16)],
                  mods_a[1], gains[1], w1, w2, 1, bsz=bsz, nt=nt, nsub=nsub)

    lanes = 256
    wd, wr, ar, ai = _s5_weights(s5_b_re[0], s5_b_im[0], cf_re, cf_im, a_re, a_im, lanes, bsz)
    h3 = h.reshape(bsz, ls, d)
    s5_tile = S5_SUBTILES * S5_TSTEPS
    assert lc % s5_tile == 0 and seq % s5_tile == 0
    nct, nlt = lc // s5_tile, seq // s5_tile
    dsk = s5_d[0].astype(f32).reshape(1, d)
    y = _s5_scan(h3, mods_a[2], gains[2], wd, wr, ar, ai, dsk, None, bsz=bsz, nct=nct, nlt=nlt, reverse=False)
    y = _s5_scan(h3, mods_a[2], gains[2], wd, wr, ar, ai, dsk, y, bsz=bsz, nct=nct, nlt=nlt, reverse=True)
    h = _post_mlp("s5", h, [y.reshape(bsz * ls, d)], [s5_w_glu[0].astype(bf16)],
                  mods_a[2], gains[2], w1, w2, 2, bsz=bsz, nt=nt, nsub=nsub)

    dh = d // NA_HEADS
    qq, kk, vv = _proj_flat(h, mods_a[3], gains[3], na_w_qkv[0].astype(bf16), (bf16, bf16, bf16), bsz=bsz, nt=nt,
                            nsub=2 * nsub, scales=(dh ** -0.5, 1.0, 1.0))
    bias = _na_bias_table(na_rpb[0])
    o = _na_core(qq, kk, vv, bias, bsz=bsz, ls=ls, lc=lc, nrows=NA_GROUP)
    hl = _post_mlp("na", h, [o], [na_w_out[0].astype(bf16)], mods_a[3], gains[3], w1, w2, 3,
                   bsz=bsz, nt=nt, nsub=nsub, lat_only=True)
    return hl.reshape(bsz, seq, d)
```
